```python
import jax
import jax.numpy as jnp
from jax import lax
import numpy as np

D_MODEL = 2048
BATCH = 2
SEQ = 4096
DEPTH = 4

GRID_W = 64
CTX_LEN = 256
NORM_EPS = 1e-6
N_MOD = 6

HEAD_DIM = 128
ATTN_HEADS = 8
ATTN_KV_HEADS = 2
ATTN_GROUP = ATTN_HEADS // ATTN_KV_HEADS
WINDOW = 128
ATTN_BLOCK = 128
ROPE_BASE = 10000.0

POOL_GROUPS = 4
POOL_GROUP_DIM = 128
POOL_WIDTH = POOL_GROUPS * POOL_GROUP_DIM
POOL_SPANS = (2, 4, 8, 16)

RET_HEADS = 4
RET_QK_DIM = 64
RET_V_DIM = 128
RET_CHUNK = 128

N_BRANCHES = 3
ATTN_Q_WIDTH = ATTN_HEADS * HEAD_DIM
ATTN_KV_WIDTH = ATTN_KV_HEADS * HEAD_DIM
RET_QK_WIDTH = RET_HEADS * RET_QK_DIM
RET_V_WIDTH = RET_HEADS * RET_V_DIM
IN_SIZES = (ATTN_Q_WIDTH, ATTN_KV_WIDTH, ATTN_KV_WIDTH, POOL_WIDTH,
            RET_QK_WIDTH, RET_QK_WIDTH, RET_V_WIDTH, RET_V_WIDTH, N_BRANCHES * D_MODEL)
IN_WIDTH = sum(IN_SIZES)

MOE_GROUPS = 4
MOE_EXPERTS_PER_GROUP = 8
MOE_EXPERTS = MOE_GROUPS * MOE_EXPERTS_PER_GROUP
MOE_TOP_K = 2
MOE_D_FF = 512
MOE_BLOCK = 128

kernel_name = "hybrid_prefix_flow_block"


def rms_norm(x, g):
    xf = x.astype(jnp.float32)
    y = xf * lax.rsqrt(jnp.mean(jnp.square(xf), axis=-1, keepdims=True) + NORM_EPS)
    return y.astype(x.dtype) * g


def modulate(h, shift, scale):
    return h * (1.0 + scale) + shift


def split_heads(t, n_heads):
    B, L, _ = t.shape
    return t.reshape(B, L, n_heads, -1).transpose(0, 2, 1, 3)


def axial_rope_tables(n_tokens, dtype):
    rows = n_tokens // GRID_W
    row = jnp.repeat(jnp.arange(rows, dtype=jnp.float32), GRID_W)
    col = jnp.tile(jnp.arange(GRID_W, dtype=jnp.float32), rows)
    n_freq = HEAD_DIM // 4
    inv_freq = ROPE_BASE ** (-jnp.arange(n_freq, dtype=jnp.float32) / n_freq)
    ang = jnp.concatenate([row[:, None] * inv_freq, col[:, None] * inv_freq], axis=-1)
    return jnp.cos(ang).astype(dtype), jnp.sin(ang).astype(dtype)


def apply_rope(x, cos, sin):
    x1, x2 = jnp.split(x, 2, axis=-1)
    cos = cos[None, :, None, :]
    sin = sin[None, :, None, :]
    return jnp.concatenate([x1 * cos - x2 * sin, x2 * cos + x1 * sin], axis=-1)


def softmax_with_sink(scores, sink_kg):
    sink_col = jnp.broadcast_to(sink_kg[:, :, None, None], scores.shape[:-1] + (1,))
    p = jax.nn.softmax(jnp.concatenate([scores, sink_col], axis=-1), axis=-1)
    return p[..., :-1]


def windowed_attention(q, k, v, k_ctx, v_ctx, sink_kg):
    B, L, H, Dh = q.shape
    blk = ATTN_BLOCK
    nb = L // blk
    qb = (q * Dh ** -0.5).reshape(B, nb, blk, ATTN_KV_HEADS, ATTN_GROUP, Dh)
    pad = ((0, 0), (blk, blk), (0, 0), (0, 0))
    kp = jnp.pad(k, pad).reshape(B, nb + 2, blk, ATTN_KV_HEADS, Dh)
    vp = jnp.pad(v, pad).reshape(B, nb + 2, blk, ATTN_KV_HEADS, Dh)
    k_band = jnp.concatenate([kp[:, :-2], kp[:, 1:-1], kp[:, 2:]], axis=2)
    v_band = jnp.concatenate([vp[:, :-2], vp[:, 1:-1], vp[:, 2:]], axis=2)
    q_pos = jnp.arange(nb)[:, None] * blk + jnp.arange(blk)[None, :]
    k_pos = jnp.arange(nb)[:, None] * blk - blk + jnp.arange(3 * blk)[None, :]
    rel = q_pos[:, :, None] - k_pos[:, None, :]
    valid = (jnp.abs(rel) <= WINDOW) & (k_pos[:, None, :] >= 0) & (k_pos[:, None, :] < L)
    s_band = jnp.einsum('bnqkgd,bnskd->bnkgqs', qb, k_band).astype(jnp.float32)
    s_band = jnp.where(valid[None, :, None, None], s_band, -jnp.inf)
    s_ctx = jnp.einsum('bnqkgd,bckd->bnkgqc', qb, k_ctx).astype(jnp.float32)
    p = softmax_with_sink(jnp.concatenate([s_band, s_ctx], axis=-1), sink_kg).astype(v.dtype)
    n_win = 3 * blk
    out = (jnp.einsum('bnkgqs,bnskd->bnqkgd', p[..., :n_win], v_band)
           + jnp.einsum('bnkgqc,bckd->bnqkgd', p[..., n_win:], v_ctx))
    return out.reshape(B, L, H * Dh)


def context_attention(q, k, v, sink_kg):
    B, Lc, H, Dh = q.shape
    qg = (q * Dh ** -0.5).reshape(B, Lc, ATTN_KV_HEADS, ATTN_GROUP, Dh)
    s = jnp.einsum('bqkgd,bckd->bkgqc', qg, k).astype(jnp.float32)
    p = softmax_with_sink(s, sink_kg).astype(v.dtype)
    return jnp.einsum('bkgqc,bckd->bqkgd', p, v).reshape(B, Lc, H * Dh)


def multiscale_pool(u, pool_w, pool_scale):
    B, L, _ = u.shape
    ug = u.reshape(B, L, POOL_GROUPS, POOL_GROUP_DIM).astype(jnp.float32)
    cs = jnp.concatenate([jnp.zeros((B, 1, POOL_GROUPS, POOL_GROUP_DIM), jnp.float32),
                          jnp.cumsum(ug, axis=1)], axis=1)
    radius = jnp.array(POOL_SPANS, dtype=jnp.int32) // 2
    t = jnp.arange(L, dtype=jnp.int32)
    lo = jnp.clip(t[:, None] - radius[None, :], 0, L)
    hi = jnp.clip(t[:, None] + radius[None, :] + 1, 0, L)
    gidx = jnp.arange(POOL_GROUPS)[None, :]
    window_sum = cs[:, hi, gidx] - cs[:, lo, gidx]
    count = (hi - lo).astype(jnp.float32)[None, :, :, None]
    pooled = (window_sum / count - ug).astype(u.dtype)
    mixed = jnp.einsum('blgc,gcd->blgd', pooled, pool_w)
    return mixed.reshape(B, L, POOL_WIDTH) * pool_scale


def retention_final_state(k, v, log_gamma):
    L = k.shape[2]
    w = jnp.exp(log_gamma[:, None] * (L - 1.0 - jnp.arange(L, dtype=jnp.float32)))
    return jnp.einsum('bhld,bhlv->bhdv', k.astype(jnp.float32) * w[None, :, :, None], v.astype(jnp.float32))


def retention_chunkwise(q, k, v, log_gamma, s0):
    B, H, L, dk = q.shape
    dv = v.shape[-1]
    C = RET_CHUNK
    nc = L // C

    def chunks(t):
        return t.astype(jnp.float32).reshape(B, H, nc, C, t.shape[-1]).transpose(2, 0, 1, 3, 4)

    pos = jnp.arange(C, dtype=jnp.float32)
    diff = pos[:, None] - pos[None, :]
    intra = jnp.where(diff >= 0, jnp.exp(log_gamma[:, None, None] * jnp.maximum(diff, 0.0)), 0.0)
    q_decay = jnp.exp(log_gamma[:, None] * (pos + 1.0))[None, :, :, None]
    k_decay = jnp.exp(log_gamma[:, None] * (C - 1.0 - pos))[None, :, :, None]
    chunk_decay = jnp.exp(log_gamma * C)[None, :, None, None]

    def step(state, blk):
        qb, kb, vb = blk
        scores = jnp.einsum('bhid,bhjd->bhij', qb, kb) * intra
        out = (jnp.einsum('bhij,bhjv->bhiv', scores, vb)
               + jnp.einsum('bhid,bhdv->bhiv', qb, state) * q_decay)
        state = state * chunk_decay + jnp.einsum('bhjd,bhjv->bhdv', kb * k_decay, vb)
        return state, out

    _, out = lax.scan(step, s0, (chunks(q), chunks(k), chunks(v)))
    return out.transpose(1, 2, 0, 3, 4).reshape(B, H, L, dv)


def retention_branch(q, k, v, g, lg_f, lg_b, s_fwd, s_bwd):
    fwd = retention_chunkwise(q, k, v, lg_f, s_fwd)
    bwd = jnp.flip(retention_chunkwise(jnp.flip(q, 2), jnp.flip(k, 2), jnp.flip(v, 2), lg_b, s_bwd), 2)
    y = fwd + bwd
    mu = jnp.mean(y, axis=-1, keepdims=True)
    var = jnp.mean(jnp.square(y - mu), axis=-1, keepdims=True)
    y = (y - mu) * lax.rsqrt(var + NORM_EPS)
    B, H, L, dv = y.shape
    y = y.transpose(0, 2, 1, 3).reshape(B, L, H * dv).astype(g.dtype)
    return jax.nn.silu(g) * y


def merge_branches(y_attn, y_pool, y_ret, gate_logits, w_br_attn, w_br_pool, w_br_ret, w_out):
    g_attn, g_pool, g_ret = jnp.split(jax.nn.sigmoid(gate_logits), N_BRANCHES, axis=-1)
    merged = g_attn * (y_attn @ w_br_attn) + g_pool * (y_pool @ w_br_pool) + g_ret * (y_ret @ w_br_ret)
    return merged @ w_out


def token_mixer(h_lat, h_ctx, w_in, sink, pool_w, pool_scale, ret_decay_fwd, ret_decay_bwd,
                w_br_attn, w_br_pool, w_br_ret, w_out, rope_cos, rope_sin, with_ctx):
    B, L, _ = h_lat.shape
    Lc = h_ctx.shape[1]
    splits = np.cumsum(IN_SIZES)[:-1].tolist()
    proj = jnp.concatenate([h_ctx, h_lat], axis=1) @ w_in
    qa_c, ka_c, va_c, u_c, qr_c, kr_c, vr_c, gr_c, gates_c = jnp.split(proj[:, :Lc], splits, axis=-1)
    qa_l, ka_l, va_l, u_l, qr_l, kr_l, vr_l, gr_l, gates_l = jnp.split(proj[:, Lc:], splits, axis=-1)

    sink_kg = sink.astype(jnp.float32).reshape(ATTN_KV_HEADS, ATTN_GROUP)
    k_ctx = ka_c.reshape(B, Lc, ATTN_KV_HEADS, HEAD_DIM)
    v_ctx = va_c.reshape(B, Lc, ATTN_KV_HEADS, HEAD_DIM)
    q_lat = apply_rope(qa_l.reshape(B, L, ATTN_HEADS, HEAD_DIM), rope_cos, rope_sin)
    k_lat = apply_rope(ka_l.reshape(B, L, ATTN_KV_HEADS, HEAD_DIM), rope_cos, rope_sin)
    v_lat = va_l.reshape(B, L, ATTN_KV_HEADS, HEAD_DIM)
    ya_l = windowed_attention(q_lat, k_lat, v_lat, k_ctx, v_ctx, sink_kg)

    yp_l = multiscale_pool(u_l, pool_w, pool_scale)

    lg_f = jax.nn.log_sigmoid(ret_decay_fwd.astype(jnp.float32))
    lg_b = jax.nn.log_sigmoid(ret_decay_bwd.astype(jnp.float32))
    k_scale = RET_QK_DIM ** -0.5
    kc = split_heads(kr_c, RET_HEADS) * k_scale
    vc = split_heads(vr_c, RET_HEADS)
    s_fwd = retention_final_state(kc, vc, lg_f)
    s_bwd = retention_final_state(jnp.flip(kc, 2), jnp.flip(vc, 2), lg_b)
    yr_l = retention_branch(split_heads(qr_l, RET_HEADS), split_heads(kr_l, RET_HEADS) * k_scale,
                            split_heads(vr_l, RET_HEADS), gr_l, lg_f, lg_b, s_fwd, s_bwd)

    y_lat = merge_branches(ya_l, yp_l, yr_l, gates_l, w_br_attn, w_br_pool, w_br_ret, w_out)
    if not with_ctx:
        return y_lat, None

    ya_c = context_attention(qa_c.reshape(B, Lc, ATTN_HEADS, HEAD_DIM), k_ctx, v_ctx, sink_kg)
    yp_c = multiscale_pool(u_c, pool_w, pool_scale)
    zero_state = jnp.zeros_like(s_fwd)
    yr_c = retention_branch(split_heads(qr_c, RET_HEADS), kc, vc, gr_c, lg_f, lg_b, zero_state, zero_state)
    y_ctx = merge_branches(ya_c, yp_c, yr_c, gates_c, w_br_attn, w_br_pool, w_br_ret, w_out)
    return y_lat, y_ctx


def expert_dispatch(t, expert, weight, w_gate, w_up, w_down):
    T, D = t.shape
    A = T * MOE_TOP_K
    flat_e = expert.reshape(A)
    flat_tok = jnp.arange(A, dtype=jnp.int32) // MOE_TOP_K
    flat_w = weight.reshape(A)
    order = jnp.argsort(flat_e)
    e_sorted = flat_e[order]
    counts = jax.ops.segment_sum(jnp.ones((A,), jnp.int32), flat_e, num_segments=MOE_EXPERTS)
    padded = (counts + MOE_BLOCK - 1) // MOE_BLOCK * MOE_BLOCK
    start = jnp.cumsum(counts) - counts
    padded_end = jnp.cumsum(padded)
    padded_start = padded_end - padded
    dest = padded_start[e_sorted] + jnp.arange(A, dtype=jnp.int32) - start[e_sorted]
    n_rows = -(-(A + MOE_EXPERTS * (MOE_BLOCK - 1)) // MOE_BLOCK) * MOE_BLOCK
    n_blocks = n_rows // MOE_BLOCK
    row_tok = jnp.full((n_rows,), T, jnp.int32).at[dest].set(flat_tok[order])
    row_w = jnp.zeros((n_rows,), t.dtype).at[dest].set(flat_w[order])
    block_start = jnp.arange(n_blocks, dtype=jnp.int32) * MOE_BLOCK
    block_expert = jnp.minimum(jnp.searchsorted(padded_end, block_start, side='right'), MOE_EXPERTS - 1)
    t_pad = jnp.concatenate([t, jnp.zeros((1, D), t.dtype)], axis=0)
    xs = t_pad[row_tok].reshape(n_blocks, MOE_BLOCK, D)

    def run_block(args):
        xb, e = args
        h = jax.nn.silu(xb @ w_gate[e]) * (xb @ w_up[e])
        return h @ w_down[e]

    ys = lax.map(run_block, (xs, block_expert)).reshape(n_rows, D)
    out = jnp.zeros((T + 1, D), t.dtype).at[row_tok].add(ys * row_w[:, None])
    return out[:T]


def hierarchical_moe(t, w_route_group, b_route_group, w_route_expert, b_route_expert,
                     w_expert_gate, w_expert_up, w_expert_down):
    T = t.shape[0]
    g_logits = (t @ w_route_group + b_route_group).astype(jnp.float32)
    g_prob = jax.nn.softmax(g_logits, axis=-1)
    grp = jnp.argmax(g_logits, axis=-1).astype(jnp.int32)
    p_grp = jnp.take_along_axis(g_prob, grp[:, None], axis=-1)
    e_logits = (t @ w_route_expert + b_route_expert).astype(jnp.float32).reshape(T, MOE_GROUPS, MOE_EXPERTS_PER_GROUP)
    e_in_grp = jnp.take_along_axis(e_logits, grp[:, None, None], axis=1)[:, 0]
    top_val, top_idx = lax.top_k(e_in_grp, MOE_TOP_K)
    weight = (jax.nn.softmax(top_val, axis=-1) * p_grp).astype(t.dtype)
    expert = grp[:, None] * MOE_EXPERTS_PER_GROUP + top_idx.astype(jnp.int32)
    return expert_dispatch(t, expert, weight, w_expert_gate, w_expert_up, w_expert_down)


def setup_inputs(seed: int = 0) -> dict:
    key = jax.random.key(seed)
    ks = jax.random.split(key, 32)
    D = D_MODEL

    def normal(k, shape, scale):
        return jax.random.normal(k, shape, jnp.float32) * scale

    decay_init = jnp.log(2.0 ** (5.0 + jnp.arange(RET_HEADS, dtype=jnp.float32)) - 1.0)
    return {
        "x": normal(ks[0], (BATCH, SEQ, D), 1.0),
        "c": normal(ks[1], (BATCH, D), 1.0),
        "ctx": normal(ks[2], (BATCH, CTX_LEN, D), 1.0),
        "c_ctx": normal(ks[3], (D,), 1.0),
        "w_mod": normal(ks[4], (DEPTH, D, N_MOD * D), 0.5 * D ** -0.5),
        "b_mod": normal(ks[5], (DEPTH, N_MOD * D), 0.02),
        "norm1_g": 1.0 + normal(ks[6], (DEPTH, D), 0.02),
        "norm2_g": 1.0 + normal(ks[7], (DEPTH, D), 0.02),
        "w_in": normal(ks[8], (DEPTH, D, IN_WIDTH), D ** -0.5),
        "attn_sink": normal(ks[9], (DEPTH, ATTN_HEADS), 0.5),
        "pool_w": normal(ks[10], (DEPTH, POOL_GROUPS, POOL_GROUP_DIM, POOL_GROUP_DIM), POOL_GROUP_DIM ** -0.5),
        "pool_scale": 1.0 + normal(ks[11], (DEPTH, POOL_WIDTH), 0.02),
        "ret_decay_fwd": decay_init + normal(ks[12], (DEPTH, RET_HEADS), 0.1),
        "ret_decay_bwd": decay_init + normal(ks[13], (DEPTH, RET_HEADS), 0.1),
        "w_br_attn": normal(ks[14], (DEPTH, ATTN_Q_WIDTH, D), ATTN_Q_WIDTH ** -0.5),
        "w_br_pool": normal(ks[15], (DEPTH, POOL_WIDTH, D), POOL_WIDTH ** -0.5),
        "w_br_ret": normal(ks[16], (DEPTH, RET_V_WIDTH, D), RET_V_WIDTH ** -0.5),
        "w_out": normal(ks[17], (DEPTH, D, D), D ** -0.5),
        "w_route_group": normal(ks[18], (DEPTH, D, MOE_GROUPS), D ** -0.5),
        "b_route_group": normal(ks[19], (DEPTH, MOE_GROUPS), 0.01),
        "w_route_expert": normal(ks[20], (DEPTH, D, MOE_EXPERTS), D ** -0.5),
        "b_route_expert": normal(ks[21], (DEPTH, MOE_EXPERTS), 0.01),
        "w_expert_gate": normal(ks[22], (DEPTH, MOE_EXPERTS, D, MOE_D_FF), D ** -0.5),
        "w_expert_up": normal(ks[23], (DEPTH, MOE_EXPERTS, D, MOE_D_FF), D ** -0.5),
        "w_expert_down": normal(ks[24], (DEPTH, MOE_EXPERTS, MOE_D_FF, D), MOE_D_FF ** -0.5),
        "final_norm_g": 1.0 + normal(ks[25], (D,), 0.02),
    }


def reference(x, c, ctx, c_ctx, w_mod, b_mod, norm1_g, norm2_g, w_in, attn_sink, pool_w, pool_scale,
              ret_decay_fwd, ret_decay_bwd, w_br_attn, w_br_pool, w_br_ret, w_out,
              w_route_group, b_route_group, w_route_expert, b_route_expert,
              w_expert_gate, w_expert_up, w_expert_down, final_norm_g):
    B, L, D = x.shape
    Lc = ctx.shape[1]
    rope_cos, rope_sin = axial_rope_tables(L, x.dtype)
    cond_lat = jax.nn.silu(c)
    cond_ctx = jax.nn.silu(c_ctx)
    xc = ctx
    for l in range(DEPTH):
        with_ctx = l < DEPTH - 1
        mod_l = jnp.split(cond_lat @ w_mod[l] + b_mod[l], N_MOD, axis=-1)
        mod_c = jnp.split(cond_ctx @ w_mod[l] + b_mod[l], N_MOD, axis=-1)
        h_l = modulate(rms_norm(x, norm1_g[l]), mod_l[0][:, None], mod_l[1][:, None])
        h_c = modulate(rms_norm(xc, norm1_g[l]), mod_c[0], mod_c[1])
        y_l, y_c = token_mixer(h_l, h_c, w_in[l], attn_sink[l], pool_w[l], pool_scale[l],
                               ret_decay_fwd[l], ret_decay_bwd[l], w_br_attn[l], w_br_pool[l],
                               w_br_ret[l], w_out[l], rope_cos, rope_sin, with_ctx)
        x = x + mod_l[2][:, None] * y_l
        h2_l = modulate(rms_norm(x, norm2_g[l]), mod_l[3][:, None], mod_l[4][:, None])
        if with_ctx:
            xc = xc + mod_c[2] * y_c
            h2_c = modulate(rms_norm(xc, norm2_g[l]), mod_c[3], mod_c[4])
            tokens = jnp.concatenate([h2_c.reshape(-1, D), h2_l.reshape(-1, D)], axis=0)
            out = hierarchical_moe(tokens, w_route_group[l], b_route_group[l], w_route_expert[l],
                                   b_route_expert[l], w_expert_gate[l], w_expert_up[l], w_expert_down[l])
            n_c = B * Lc
            xc = xc + mod_c[5] * out[:n_c].reshape(B, Lc, D)
            x = x + mod_l[5][:, None] * out[n_c:].reshape(B, L, D)
        else:
            out = hierarchical_moe(h2_l.reshape(-1, D), w_route_group[l], b_route_group[l], w_route_expert[l],
                                   b_route_expert[l], w_expert_gate[l], w_expert_up[l], w_expert_down[l])
            x = x + mod_l[5][:, None] * out.reshape(B, L, D)
    return rms_norm(x, final_norm_g)
```

```python
import functools

import jax
import jax.numpy as jnp
from jax import lax
from jax.experimental import pallas as pl
from jax.experimental.pallas import tpu as pltpu

f32 = jnp.float32
bf16 = jnp.bfloat16
i32 = jnp.int32
u32 = jnp.uint32

D_MODEL = 2048
BATCH = 2
SEQ = 4096
DEPTH = 4
GRID_W = 64
CTX_LEN = 256
NORM_EPS = 1e-6
N_MOD = 6
HEAD_DIM = 128
ATTN_HEADS = 8
ATTN_KV_HEADS = 2
ATTN_GROUP = ATTN_HEADS // ATTN_KV_HEADS
ATTN_BLOCK = 128
ROPE_BASE = 10000.0
POOL_GROUPS = 4
POOL_GROUP_DIM = 128
POOL_WIDTH = POOL_GROUPS * POOL_GROUP_DIM
POOL_SPANS = (2, 4, 8, 16)
RET_HEADS = 4
RET_QK_DIM = 64
RET_V_DIM = 128
N_BRANCHES = 3
ATTN_Q_WIDTH = ATTN_HEADS * HEAD_DIM
ATTN_KV_WIDTH = ATTN_KV_HEADS * HEAD_DIM
RET_QK_WIDTH = RET_HEADS * RET_QK_DIM
RET_V_WIDTH = RET_HEADS * RET_V_DIM
IN_SIZES = (ATTN_Q_WIDTH, ATTN_KV_WIDTH, ATTN_KV_WIDTH, POOL_WIDTH,
            RET_QK_WIDTH, RET_QK_WIDTH, RET_V_WIDTH, RET_V_WIDTH, N_BRANCHES * D_MODEL)
IN_WIDTH = sum(IN_SIZES)
MOE_GROUPS = 4
MOE_EXPERTS_PER_GROUP = 8
MOE_EXPERTS = MOE_GROUPS * MOE_EXPERTS_PER_GROUP
MOE_TOP_K = 2
MOE_D_FF = 512

COL_QA = 0
COL_KA = COL_QA + ATTN_Q_WIDTH
COL_VA = COL_KA + ATTN_KV_WIDTH
COL_U = COL_VA + ATTN_KV_WIDTH
COL_QR = COL_U + POOL_WIDTH
COL_KR = COL_QR + RET_QK_WIDTH
COL_VR = COL_KR + RET_QK_WIDTH
COL_GR = COL_VR + RET_V_WIDTH
COL_GATES = COL_GR + RET_V_WIDTH

V7X_LANES = 128
V7X_VMEM_LIMIT_BYTES = 56 * 1024 * 1024

N_LAT = BATCH * SEQ
N_CTX = BATCH * CTX_LEN
N_TOK = N_LAT + N_CTX
TM_LAT = 1024
TM_CTX = N_CTX
TN_PROJ = 512
TN_MERGE = 512
TN_OUT = 512
TN_MOD = 2048
ROW_CHUNK = 128
RET_CHUNK = 256
TM_ROUTE = 512
TM_COMBINE = 256
MOE_BLK = 256
MOE_ROWS = (N_TOK * MOE_TOP_K + MOE_EXPERTS * (MOE_BLK - 1)) // MOE_BLK * MOE_BLK
PACKED = D_MODEL // 2

assert CTX_LEN == RET_CHUNK and SEQ % RET_CHUNK == 0
assert IN_WIDTH % TN_PROJ == 0 and COL_GATES % TN_MERGE == 0


def _cparams(sem, vmem=V7X_VMEM_LIMIT_BYTES):
    return pltpu.CompilerParams(dimension_semantics=sem, vmem_limit_bytes=vmem)


def _dot(a, b):
    return jnp.dot(a, b, preferred_element_type=f32)


def _dot_nt(a, b):
    return lax.dot_general(a, b, (((1,), (1,)), ((), ())), preferred_element_type=f32)


def _dot_tn(a, b):
    return lax.dot_general(a, b, (((0,), (0,)), ((), ())), preferred_element_type=f32)


def _mod_kernel(cond_ref, w_ref, b_ref, o_ref):
    c = cond_ref[...]
    s = (c * jax.nn.sigmoid(c)).astype(bf16)
    o_ref[0] = _dot(s, w_ref[0].astype(bf16)) + b_ref[0]


def _modulation(cond8, w_mod, b_mod):
    n = N_MOD * D_MODEL
    return pl.pallas_call(
        _mod_kernel,
        grid=(DEPTH, n // TN_MOD),
        in_specs=[
            pl.BlockSpec((8, D_MODEL), lambda l, j: (0, 0)),
            pl.BlockSpec((1, D_MODEL, TN_MOD), lambda l, j: (l, 0, j)),
            pl.BlockSpec((1, 1, TN_MOD), lambda l, j: (l, 0, j)),
        ],
        out_specs=pl.BlockSpec((1, 8, TN_MOD), lambda l, j: (l, 0, j)),
        out_shape=jax.ShapeDtypeStruct((DEPTH, 8, n), f32),
        compiler_params=_cparams(("parallel", "parallel")),
        name="modulation",
    )(cond8, w_mod, b_mod.reshape(DEPTH, 1, n))


def _norm_mod_rows(x, g, shift, scale):
    ms = jnp.mean(x * x, axis=-1, keepdims=True)
    y = x * lax.rsqrt(ms + NORM_EPS) * g
    return y * (1.0 + scale) + shift


def _norm_proj_kernel(x_ref, mod_ref, g_ref, w_ref, o_ref, h_ref, *, tm):
    @pl.when(pl.program_id(1) == 0)
    def _():
        g = g_ref[...]
        shift = mod_ref[0, 0:1, :]
        scale = mod_ref[0, 1:2, :]

        def body(r, carry):
            rows = pl.ds(pl.multiple_of(r * ROW_CHUNK, ROW_CHUNK), ROW_CHUNK)
            h_ref[rows, :] = _norm_mod_rows(x_ref[rows, :], g, shift, scale).astype(bf16)
            return carry

        lax.fori_loop(0, tm // ROW_CHUNK, body, 0)

    o_ref[...] = _dot(h_ref[...], w_ref[...].astype(bf16))


def _norm_proj(x, mod_l, g1, w_in, *, tm, row_block0, n_tiles, mod_sel):
    return pl.pallas_call(
        functools.partial(_norm_proj_kernel, tm=tm),
        grid=(n_tiles, IN_WIDTH // TN_PROJ),
        in_specs=[
            pl.BlockSpec((tm, D_MODEL), lambda i, j: (row_block0 + i, 0)),
            pl.BlockSpec((1, N_MOD, D_MODEL), lambda i, j: (mod_sel(i), 0, 0)),
            pl.BlockSpec((1, D_MODEL), lambda i, j: (0, 0)),
            pl.BlockSpec((D_MODEL, TN_PROJ), lambda i, j: (0, j)),
        ],
        out_specs=pl.BlockSpec((tm, TN_PROJ), lambda i, j: (i, j)),
        out_shape=jax.ShapeDtypeStruct((n_tiles * tm, IN_WIDTH), f32),
        scratch_shapes=[pltpu.VMEM((tm, D_MODEL), bf16)],
        compiler_params=_cparams(("parallel", "arbitrary")),
        name="norm_proj",
    )(x, mod_l, g1, w_in)


def _rope(t, cos_full, sin_signed):
    return t * cos_full + pltpu.roll(t, HEAD_DIM // 2, 1) * sin_signed


def _sink_column(sink_ref, g, rows_per_head):
    n = ATTN_GROUP * rows_per_head
    row = lax.broadcasted_iota(i32, (n, 1), 0)
    col = jnp.full((n, 1), sink_ref[g * ATTN_GROUP + ATTN_GROUP - 1], f32)
    for j in range(ATTN_GROUP - 2, -1, -1):
        col = jnp.where(row < (j + 1) * rows_per_head, sink_ref[g * ATTN_GROUP + j], col)
    return col


def _softmax_pv(scores, values, sink_col):
    m = sink_col
    for s in scores:
        m = jnp.maximum(m, jnp.max(s, axis=-1, keepdims=True))
    es = [jnp.exp(s - m) for s in scores]
    denom = jnp.exp(sink_col - m)
    for e in es:
        denom = denom + jnp.sum(e, axis=-1, keepdims=True)
    inv = 1.0 / denom
    out = None
    for e, v in zip(es, values):
        o = _dot((e * inv).astype(bf16), v)
        out = o if out is None else out + o
    return out


def _attn_lat_kernel(sink_ref, q_ref, kp_ref, kc_ref, kn_ref, vp_ref, vc_ref, vn_ref, kx_ref, vx_ref,
                     cq_ref, sq_ref, cp_ref, sp_ref, cn_ref, sn_ref, o_ref, *, n_blocks):
    g = pl.program_id(1)
    n = pl.program_id(2)
    blk = ATTN_BLOCK
    scale = HEAD_DIM ** -0.5
    cq = cq_ref[...]
    sq = sq_ref[...]
    q = q_ref[...]
    qs = jnp.concatenate(
        [(_rope(q[:, j * HEAD_DIM:(j + 1) * HEAD_DIM], cq, sq) * scale).astype(bf16) for j in range(ATTN_GROUP)],
        axis=0)
    k_prev = _rope(kp_ref[...], cp_ref[...], sp_ref[...]).astype(bf16)
    k_cur = _rope(kc_ref[...], cq, sq).astype(bf16)
    k_next = _rope(kn_ref[...], cn_ref[...], sn_ref[...]).astype(bf16)
    k_ctx = kx_ref[...].astype(bf16)

    rows = ATTN_GROUP * blk
    qi = lax.broadcasted_iota(i32, (rows, blk), 0) % blk
    kj = lax.broadcasted_iota(i32, (rows, blk), 1)
    neg = jnp.float32(-jnp.inf)
    s_prev = jnp.where((kj >= qi) & (n > 0), _dot_nt(qs, k_prev), neg)
    s_cur = _dot_nt(qs, k_cur)
    s_next = jnp.where((kj <= qi) & (n < n_blocks - 1), _dot_nt(qs, k_next), neg)
    s_ctx = _dot_nt(qs, k_ctx)

    out = _softmax_pv(
        [s_prev, s_cur, s_next, s_ctx],
        [vp_ref[...].astype(bf16), vc_ref[...].astype(bf16), vn_ref[...].astype(bf16), vx_ref[...].astype(bf16)],
        _sink_column(sink_ref, g, blk))
    for j in range(ATTN_GROUP):
        o_ref[:, j * HEAD_DIM:(j + 1) * HEAD_DIM] = out[j * blk:(j + 1) * blk, :].astype(o_ref.dtype)


def _attn_lat(p_lat, p_ctx, sink, cos_full, sin_signed):
    nb = SEQ // ATTN_BLOCK
    blk = ATTN_BLOCK
    kcol = COL_KA // HEAD_DIM
    vcol = COL_VA // HEAD_DIM

    def kv_spec(col0, shift):
        def imap(b, g, n):
            return (b * nb + jnp.clip(n + shift, 0, nb - 1), col0 + g)
        return pl.BlockSpec((blk, HEAD_DIM), imap)

    def tab_spec(shift):
        return pl.BlockSpec((blk, HEAD_DIM), lambda b, g, n: (jnp.clip(n + shift, 0, nb - 1), 0))

    gw = ATTN_GROUP * HEAD_DIM
    return pl.pallas_call(
        functools.partial(_attn_lat_kernel, n_blocks=nb),
        grid=(BATCH, ATTN_KV_HEADS, nb),
        in_specs=[
            pl.BlockSpec(memory_space=pltpu.SMEM),
            pl.BlockSpec((blk, gw), lambda b, g, n: (b * nb + n, g)),
            kv_spec(kcol, -1), kv_spec(kcol, 0), kv_spec(kcol, 1),
            kv_spec(vcol, -1), kv_spec(vcol, 0), kv_spec(vcol, 1),
            pl.BlockSpec((CTX_LEN, HEAD_DIM), lambda b, g, n: (b, kcol + g)),
            pl.BlockSpec((CTX_LEN, HEAD_DIM), lambda b, g, n: (b, vcol + g)),
            tab_spec(0), tab_spec(0), tab_spec(-1), tab_spec(-1), tab_spec(1), tab_spec(1),
        ],
        out_specs=pl.BlockSpec((blk, gw), lambda b, g, n: (b * nb + n, g)),
        out_shape=jax.ShapeDtypeStruct((N_LAT, ATTN_Q_WIDTH), bf16),
        compiler_params=_cparams(("parallel", "parallel", "arbitrary")),
        name="attn_lat",
    )(sink, p_lat, p_lat, p_lat, p_lat, p_lat, p_lat, p_lat, p_ctx, p_ctx,
      cos_full, sin_signed, cos_full, sin_signed, cos_full, sin_signed)


def _attn_ctx_kernel(sink_ref, q_ref, kx_ref, vx_ref, o_ref):
    g = pl.program_id(1)
    scale = HEAD_DIM ** -0.5
    q = q_ref[...]
    qs = jnp.concatenate(
        [(q[:, j * HEAD_DIM:(j + 1) * HEAD_DIM] * scale).astype(bf16) for j in range(ATTN_GROUP)], axis=0)
    s_ctx = _dot_nt(qs, kx_ref[...].astype(bf16))
    out = _softmax_pv([s_ctx], [vx_ref[...].astype(bf16)], _sink_column(sink_ref, g, CTX_LEN))
    for j in range(ATTN_GROUP):
        o_ref[:, j * HEAD_DIM:(j + 1) * HEAD_DIM] = out[j * CTX_LEN:(j + 1) * CTX_LEN, :].astype(o_ref.dtype)


def _attn_ctx(p_ctx, sink):
    gw = ATTN_GROUP * HEAD_DIM
    kcol = COL_KA // HEAD_DIM
    vcol = COL_VA // HEAD_DIM
    return pl.pallas_call(
        _attn_ctx_kernel,
        grid=(BATCH, ATTN_KV_HEADS),
        in_specs=[
            pl.BlockSpec(memory_space=pltpu.SMEM),
            pl.BlockSpec((CTX_LEN, gw), lambda b, g: (b, g)),
            pl.BlockSpec((CTX_LEN, HEAD_DIM), lambda b, g: (b, kcol + g)),
            pl.BlockSpec((CTX_LEN, HEAD_DIM), lambda b, g: (b, vcol + g)),
        ],
        out_specs=pl.BlockSpec((CTX_LEN, gw), lambda b, g: (b, g)),
        out_shape=jax.ShapeDtypeStruct((N_CTX, ATTN_Q_WIDTH), bf16),
        compiler_params=_cparams(("parallel", "parallel")),
        name="attn_ctx",
    )(sink, p_ctx, p_ctx, p_ctx)


POOL_HALO = 8
POOL_ROWS = 256


def _pool_kernel(u_ref, w_ref, s_ref, o_ref, pad_ref, *, seq_len):
    zeros = jnp.zeros((POOL_HALO, POOL_GROUP_DIM), f32)
    for gi in range(POOL_GROUPS):
        radius = POOL_SPANS[gi] // 2
        cols = slice(gi * POOL_GROUP_DIM, (gi + 1) * POOL_GROUP_DIM)
        pad_ref[0:POOL_HALO, :] = zeros
        pad_ref[POOL_HALO + seq_len:POOL_HALO + seq_len + POOL_HALO, :] = zeros
        pad_ref[POOL_HALO:POOL_HALO + seq_len, :] = u_ref[:, cols]
        w = w_ref[gi].astype(bf16)
        sc = s_ref[:, cols]
        for c in range(seq_len // POOL_ROWS):
            base = c * POOL_ROWS
            acc = pad_ref[POOL_HALO + base:POOL_HALO + base + POOL_ROWS, :]
            center = acc
            for d in range(1, radius + 1):
                acc = acc + pad_ref[POOL_HALO + base - d:POOL_HALO + base - d + POOL_ROWS, :]
                acc = acc + pad_ref[POOL_HALO + base + d:POOL_HALO + base + d + POOL_ROWS, :]
            t = lax.broadcasted_iota(i32, (POOL_ROWS, 1), 0) + base
            lo = jnp.maximum(t - radius, 0)
            hi = jnp.minimum(t + radius + 1, seq_len)
            count = (hi - lo).astype(f32)
            pooled = (acc / count - center).astype(bf16)
            o_ref[base:base + POOL_ROWS, cols] = (_dot(pooled, w) * sc).astype(o_ref.dtype)


def _pool(p, pool_w, pool_scale, *, n_seq, seq_len):
    ucol = COL_U // POOL_WIDTH
    return pl.pallas_call(
        functools.partial(_pool_kernel, seq_len=seq_len),
        grid=(n_seq,),
        in_specs=[
            pl.BlockSpec((seq_len, POOL_WIDTH), lambda s: (s, ucol)),
            pl.BlockSpec((POOL_GROUPS, POOL_GROUP_DIM, POOL_GROUP_DIM), lambda s: (0, 0, 0)),
            pl.BlockSpec((1, POOL_WIDTH), lambda s: (0, 0)),
        ],
        out_specs=pl.BlockSpec((seq_len, POOL_WIDTH), lambda s: (s, 0)),
        out_shape=jax.ShapeDtypeStruct((n_seq * seq_len, POOL_WIDTH), bf16),
        scratch_shapes=[pltpu.VMEM((seq_len + 2 * POOL_HALO, POOL_GROUP_DIM), f32)],
        compiler_params=_cparams(("parallel",)),
        name="pool",
    )(p, pool_w, pool_scale)


def _ret_tables(lg_ref, intra_ref, qdec_ref, kdec_ref, *, reverse):
    c = RET_CHUNK
    i = lax.broadcasted_iota(i32, (c, c), 0).astype(f32)
    j = lax.broadcasted_iota(i32, (c, c), 1).astype(f32)
    pos = lax.broadcasted_iota(i32, (c, V7X_LANES), 0).astype(f32)
    for h in range(RET_HEADS):
        lg = lg_ref[h]
        if reverse:
            diff = j - i
            qd = c - pos
            kd = pos
        else:
            diff = i - j
            qd = pos + 1.0
            kd = c - 1.0 - pos
        intra_ref[h] = jnp.where(diff >= 0, jnp.exp(lg * jnp.maximum(diff, 0.0)), 0.0)
        qdec_ref[h] = jnp.exp(lg * qd)
        kdec_ref[h] = jnp.exp(lg * kd)


def _ret_chunk(q_ref, k_ref, v_ref, lg_ref, intra_ref, qdec_ref, kdec_ref, state_ref):
    outs = []
    k_scale = RET_QK_DIM ** -0.5
    for h in range(RET_HEADS):
        q = q_ref[:, h * RET_QK_DIM:(h + 1) * RET_QK_DIM].astype(bf16)
        k = k_ref[:, h * RET_QK_DIM:(h + 1) * RET_QK_DIM] * k_scale
        v = v_ref[:, h * RET_V_DIM:(h + 1) * RET_V_DIM].astype(bf16)
        scores = _dot_nt(q, k.astype(bf16)) * intra_ref[h]
        state = state_ref[h]
        out = _dot(scores.astype(bf16), v) + _dot(q, state.astype(bf16)) * qdec_ref[h]
        kd = (k * kdec_ref[h][:, :RET_QK_DIM]).astype(bf16)
        chunk_decay = jnp.exp(jnp.full((1, RET_V_DIM), lg_ref[h] * RET_CHUNK, f32))
        state_ref[h] = state * chunk_decay + _dot_tn(kd, v)
        outs.append(out)
    return outs


def _ret_bwd_kernel(lg_ref, ql_ref, kl_ref, vl_ref, qc_ref, kc_ref, vc_ref, ol_ref, oc_ref,
                    intra_ref, qdec_ref, kdec_ref, state_ref):
    t = pl.program_id(1)

    @pl.when((pl.program_id(0) == 0) & (t == 0))
    def _():
        _ret_tables(lg_ref, intra_ref, qdec_ref, kdec_ref, reverse=True)

    @pl.when(t == 0)
    def _():
        state_ref[...] = jnp.zeros_like(state_ref)
        outs = _ret_chunk(qc_ref, kc_ref, vc_ref, lg_ref, intra_ref, qdec_ref, kdec_ref, state_ref)
        for h in range(RET_HEADS):
            oc_ref[:, h * RET_V_DIM:(h + 1) * RET_V_DIM] = outs[h]

    @pl.when(t > 0)
    def _():
        outs = _ret_chunk(ql_ref, kl_ref, vl_ref, lg_ref, intra_ref, qdec_ref, kdec_ref, state_ref)
        for h in range(RET_HEADS):
            ol_ref[:, h * RET_V_DIM:(h + 1) * RET_V_DIM] = outs[h]


def _ret_finish(outs, yb_ref, g_ref, o_ref):
    for h in range(RET_HEADS):
        cols = slice(h * RET_V_DIM, (h + 1) * RET_V_DIM)
        y = outs[h] + yb_ref[:, cols]
        mu = jnp.mean(y, axis=-1, keepdims=True)
        yc = y - mu
        var = jnp.mean(yc * yc, axis=-1, keepdims=True)
        yn = yc * lax.rsqrt(var + NORM_EPS)
        g = g_ref[:, cols]
        o_ref[:, cols] = (g * jax.nn.sigmoid(g) * yn).astype(o_ref.dtype)


def _ret_fwd_kernel(lg_ref, ql_ref, kl_ref, vl_ref, gl_ref, ybl_ref, qc_ref, kc_ref, vc_ref, gc_ref, ybc_ref,
                    ol_ref, oc_ref, intra_ref, qdec_ref, kdec_ref, state_ref):
    t = pl.program_id(1)

    @pl.when((pl.program_id(0) == 0) & (t == 0))
    def _():
        _ret_tables(lg_ref, intra_ref, qdec_ref, kdec_ref, reverse=False)

    @pl.when(t == 0)
    def _():
        state_ref[...] = jnp.zeros_like(state_ref)
        outs = _ret_chunk(qc_ref, kc_ref, vc_ref, lg_ref, intra_ref, qdec_ref, kdec_ref, state_ref)
        _ret_finish(outs, ybc_ref, gc_ref, oc_ref)

    @pl.when(t > 0)
    def _():
        outs = _ret_chunk(ql_ref, kl_ref, vl_ref, lg_ref, intra_ref, qdec_ref, kdec_ref, state_ref)
        _ret_finish(outs, ybl_ref, gl_ref, ol_ref)


def _ret_scratch():
    c = RET_CHUNK
    return [
        pltpu.VMEM((RET_HEADS, c, c), f32),
        pltpu.VMEM((RET_HEADS, c, V7X_LANES), f32),
        pltpu.VMEM((RET_HEADS, c, V7X_LANES), f32),
        pltpu.VMEM((RET_HEADS, RET_QK_DIM, RET_V_DIM), f32),
    ]


def _retention(p_lat, p_ctx, lg_fwd, lg_bwd):
    c = RET_CHUNK
    nc = SEQ // c
    qcol, kcol = COL_QR // RET_QK_WIDTH, COL_KR // RET_QK_WIDTH
    vcol, gcol = COL_VR // RET_V_WIDTH, COL_GR // RET_V_WIDTH

    def lat_row(reverse):
        def f(b, t):
            s = jnp.maximum(t, 1)
            return b * nc + ((nc - s) if reverse else (s - 1))
        return f

    def specs(reverse, with_gate):
        row = lat_row(reverse)
        lat = [pl.BlockSpec((c, RET_QK_WIDTH), lambda b, t: (row(b, t), qcol)),
               pl.BlockSpec((c, RET_QK_WIDTH), lambda b, t: (row(b, t), kcol)),
               pl.BlockSpec((c, RET_V_WIDTH), lambda b, t: (row(b, t), vcol))]
        ctx = [pl.BlockSpec((c, RET_QK_WIDTH), lambda b, t: (b, qcol)),
               pl.BlockSpec((c, RET_QK_WIDTH), lambda b, t: (b, kcol)),
               pl.BlockSpec((c, RET_V_WIDTH), lambda b, t: (b, vcol))]
        if with_gate:
            lat += [pl.BlockSpec((c, RET_V_WIDTH), lambda b, t: (row(b, t), gcol)),
                    pl.BlockSpec((c, RET_V_WIDTH), lambda b, t: (row(b, t), 0))]
            ctx += [pl.BlockSpec((c, RET_V_WIDTH), lambda b, t: (b, gcol)),
                    pl.BlockSpec((c, RET_V_WIDTH), lambda b, t: (b, 0))]
        outs = [pl.BlockSpec((c, RET_V_WIDTH), lambda b, t: (row(b, t), 0)),
                pl.BlockSpec((c, RET_V_WIDTH), lambda b, t: (b, 0))]
        return [pl.BlockSpec(memory_space=pltpu.SMEM)] + lat + ctx, outs

    in_specs, out_specs = specs(True, False)
    yb_lat, yb_ctx = pl.pallas_call(
        _ret_bwd_kernel,
        grid=(BATCH, nc + 1),
        in_specs=in_specs,
        out_specs=out_specs,
        out_shape=[jax.ShapeDtypeStruct((N_LAT, RET_V_WIDTH), f32),
                   jax.ShapeDtypeStruct((N_CTX, RET_V_WIDTH), f32)],
        scratch_shapes=_ret_scratch(),
        compiler_params=_cparams(("arbitrary", "arbitrary")),
        name="retention_bwd",
    )(lg_bwd, p_lat, p_lat, p_lat, p_ctx, p_ctx, p_ctx)

    in_specs, out_specs = specs(False, True)
    return pl.pallas_call(
        _ret_fwd_kernel,
        grid=(BATCH, nc + 1),
        in_specs=in_specs,
        out_specs=out_specs,
        out_shape=[jax.ShapeDtypeStruct((N_LAT, RET_V_WIDTH), bf16),
                   jax.ShapeDtypeStruct((N_CTX, RET_V_WIDTH), bf16)],
        scratch_shapes=_ret_scratch(),
        compiler_params=_cparams(("arbitrary", "arbitrary")),
        name="retention_fwd",
    )(lg_fwd, p_lat, p_lat, p_lat, p_lat, yb_lat, p_ctx, p_ctx, p_ctx, p_ctx, yb_ctx)


def _merge_kernel(ya_ref, yp_ref, yr_ref, ga_ref, gp_ref, gr_ref, wa_ref, wp_ref, wr_ref, o_ref):
    acc = jax.nn.sigmoid(ga_ref[...]) * _dot(ya_ref[...], wa_ref[...].astype(bf16))
    acc = acc + jax.nn.sigmoid(gp_ref[...]) * _dot(yp_ref[...], wp_ref[...].astype(bf16))
    acc = acc + jax.nn.sigmoid(gr_ref[...]) * _dot(yr_ref[...], wr_ref[...].astype(bf16))
    o_ref[...] = acc.astype(o_ref.dtype)


def _merge(ya, yp, yr, p, w_a, w_p, w_r, *, tm, n_tiles):
    gate0 = COL_GATES // TN_MERGE
    per = D_MODEL // TN_MERGE

    def gate_spec(br):
        return pl.BlockSpec((tm, TN_MERGE), lambda i, j: (i, gate0 + br * per + j))

    return pl.pallas_call(
        _merge_kernel,
        grid=(n_tiles, per),
        in_specs=[
            pl.BlockSpec((tm, ATTN_Q_WIDTH), lambda i, j: (i, 0)),
            pl.BlockSpec((tm, POOL_WIDTH), lambda i, j: (i, 0)),
            pl.BlockSpec((tm, RET_V_WIDTH), lambda i, j: (i, 0)),
            gate_spec(0), gate_spec(1), gate_spec(2),
            pl.BlockSpec((ATTN_Q_WIDTH, TN_MERGE), lambda i, j: (0, j)),
            pl.BlockSpec((POOL_WIDTH, TN_MERGE), lambda i, j: (0, j)),
            pl.BlockSpec((RET_V_WIDTH, TN_MERGE), lambda i, j: (0, j)),
        ],
        out_specs=pl.BlockSpec((tm, TN_MERGE), lambda i, j: (i, j)),
        out_shape=jax.ShapeDtypeStruct((n_tiles * tm, D_MODEL), bf16),
        compiler_params=_cparams(("parallel", "parallel")),
        name="merge",
    )(ya, yp, yr, p, p, p, w_a, w_p, w_r)


def _out_proj_kernel(m_ref, w_ref, x_ref, mod_ref, o_ref):
    y = _dot(m_ref[...], w_ref[...].astype(bf16))
    o_ref[...] = x_ref[...] + mod_ref[0, 2:3, :] * y


def _out_proj(merged, w_out, x, mod_l, *, tm, row_block0, n_tiles, mod_sel):
    return pl.pallas_call(
        _out_proj_kernel,
        grid=(n_tiles, D_MODEL // TN_OUT),
        in_specs=[
            pl.BlockSpec((tm, D_MODEL), lambda i, j: (i, 0)),
            pl.BlockSpec((D_MODEL, TN_OUT), lambda i, j: (0, j)),
            pl.BlockSpec((tm, TN_OUT), lambda i, j: (row_block0 + i, j)),
            pl.BlockSpec((1, N_MOD, TN_OUT), lambda i, j: (mod_sel(i), 0, j)),
        ],
        out_specs=pl.BlockSpec((tm, TN_OUT), lambda i, j: (row_block0 + i, j)),
        out_shape=jax.ShapeDtypeStruct(x.shape, f32),
        input_output_aliases={2: 0},
        compiler_params=_cparams(("parallel", "parallel")),
        name="out_proj",
    )(merged, w_out, x, mod_l)


ROUTE_LANES = V7X_LANES
LANE_E1, LANE_E2, LANE_W1, LANE_W2, LANE_R1, LANE_R2 = 0, 1, 2, 3, 4, 5


def _route_kernel(x_ref, mod_ref, g_ref, w_ref, b_ref, h_ref, info_ref, cnt_ref, hf_ref, tri_ref, run_ref):
    tm = TM_ROUTE
    step = pl.program_id(0)

    @pl.when(step == 0)
    def _():
        run_ref[...] = jnp.zeros_like(run_ref)
        r = lax.broadcasted_iota(i32, (tm, tm), 0)
        c = lax.broadcasted_iota(i32, (tm, tm), 1)
        tri_ref[...] = jnp.where(c < r, 1.0, 0.0).astype(bf16)

    g = g_ref[...]
    shift = mod_ref[0, 3:4, :]
    scale = mod_ref[0, 4:5, :]

    def body(r, carry):
        rows = pl.ds(pl.multiple_of(r * ROW_CHUNK, ROW_CHUNK), ROW_CHUNK)
        hb = _norm_mod_rows(x_ref[rows, :], g, shift, scale).astype(bf16)
        hf_ref[rows, :] = hb
        bits = lax.bitcast_convert_type(hb.astype(f32), u32)
        h_ref[rows, :] = (bits[:, :PACKED] >> 16) | (bits[:, PACKED:] & jnp.uint32(0xFFFF0000))
        return carry

    lax.fori_loop(0, tm // ROW_CHUNK, body, 0)

    logits = _dot(hf_ref[...], w_ref[...].astype(bf16)) + b_ref[...]
    lane = lax.broadcasted_iota(i32, (tm, ROUTE_LANES), 1)
    neg = jnp.float32(-jnp.inf)
    lane_f = lane.astype(f32)

    def first_argmax(vals):
        top = jnp.max(vals, axis=-1, keepdims=True)
        idx = jnp.min(jnp.where(vals == top, lane_f, float(ROUTE_LANES)), axis=-1, keepdims=True)
        return top, idx.astype(i32)

    g_logits = jnp.where(lane < MOE_GROUPS, logits, neg)
    g_top, grp = first_argmax(g_logits)
    p_grp = 1.0 / jnp.sum(jnp.exp(g_logits - g_top), axis=-1, keepdims=True)

    e_lane = lane - MOE_GROUPS
    in_grp = (e_lane >= 0) & (e_lane < MOE_EXPERTS) & (lax.shift_right_arithmetic(e_lane, 3) == grp)
    e_logits = jnp.where(in_grp, logits, neg)
    top1, idx1 = first_argmax(e_logits)
    e_logits2 = jnp.where(lane == idx1, neg, e_logits)
    top2, idx2 = first_argmax(e_logits2)
    a = jnp.exp(top2 - top1)
    w1 = (1.0 / (1.0 + a)) * p_grp
    w2 = (a / (1.0 + a)) * p_grp
    e1 = idx1 - MOE_GROUPS
    e2 = idx2 - MOE_GROUPS

    hot1 = lane == e1
    hot2 = lane == e2
    o1 = jnp.where(hot1, 1.0, 0.0)
    o2 = jnp.where(hot2, 1.0, 0.0)
    tri = tri_ref[...]
    before1 = _dot(tri, o1.astype(bf16))
    before2 = _dot(tri, o2.astype(bf16))
    c1 = jnp.sum(o1, axis=0, keepdims=True)
    c2 = jnp.sum(o2, axis=0, keepdims=True)
    run = run_ref[...]
    rank1 = jnp.sum(jnp.where(hot1, before1 + run, 0.0), axis=-1, keepdims=True)
    rank2 = jnp.sum(jnp.where(hot2, before2 + run + c1, 0.0), axis=-1, keepdims=True)
    run = run + c1 + c2
    run_ref[...] = run
    cnt_ref[...] = jnp.broadcast_to(run, cnt_ref.shape)

    info = jnp.where(lane == LANE_E1, e1.astype(f32), 0.0)
    info = jnp.where(lane == LANE_E2, e2.astype(f32), info)
    info = jnp.where(lane == LANE_W1, w1, info)
    info = jnp.where(lane == LANE_W2, w2, info)
    info = jnp.where(lane == LANE_R1, rank1, info)
    info = jnp.where(lane == LANE_R2, rank2, info)
    info_ref[...] = info


def _route(x, mod_l, g2, w_route, b_route, *, n_tiles, mod_sel):
    tm = TM_ROUTE
    return pl.pallas_call(
        _route_kernel,
        grid=(n_tiles,),
        in_specs=[
            pl.BlockSpec((tm, D_MODEL), lambda i: (i, 0)),
            pl.BlockSpec((1, N_MOD, D_MODEL), lambda i: (mod_sel(i), 0, 0)),
            pl.BlockSpec((1, D_MODEL), lambda i: (0, 0)),
            pl.BlockSpec((D_MODEL, ROUTE_LANES), lambda i: (0, 0)),
            pl.BlockSpec((1, ROUTE_LANES), lambda i: (0, 0)),
        ],
        out_specs=[
            pl.BlockSpec((tm, PACKED), lambda i: (i, 0)),
            pl.BlockSpec((tm, ROUTE_LANES), lambda i: (i, 0)),
            pl.BlockSpec((8, ROUTE_LANES), lambda i: (0, 0)),
        ],
        out_shape=[
            jax.ShapeDtypeStruct((n_tiles * tm, PACKED), u32),
            jax.ShapeDtypeStruct((n_tiles * tm, ROUTE_LANES), f32),
            jax.ShapeDtypeStruct((8, ROUTE_LANES), f32),
        ],
        scratch_shapes=[
            pltpu.VMEM((tm, D_MODEL), bf16),
            pltpu.VMEM((tm, tm), bf16),
            pltpu.VMEM((1, ROUTE_LANES), f32),
        ],
        compiler_params=_cparams(("arbitrary",)),
        name="moe_route",
    )(x, mod_l, g2, w_route, b_route)


def _dispatch_kernel(pend_ref, dest_ref, h_ref, xs_ref, zero_ref, sem):
    tm = TM_ROUTE
    step = pl.program_id(0)

    def zero_copy(e):
        start = pl.multiple_of(pend_ref[e + 1] - MOE_BLK, MOE_BLK)
        return pltpu.make_async_copy(zero_ref, xs_ref.at[pl.ds(start, MOE_BLK)], sem)

    def tail_copy(blk):
        start = pl.multiple_of(blk * MOE_BLK, MOE_BLK)
        return pltpu.make_async_copy(zero_ref, xs_ref.at[pl.ds(start, MOE_BLK)], sem)

    @pl.when(step == 0)
    def _():
        zero_ref[...] = jnp.zeros_like(zero_ref)
        first_unused = pend_ref[MOE_EXPERTS] // MOE_BLK
        for phase in range(2):
            def body(e, carry):
                @pl.when(pend_ref[e + 1] > pend_ref[e])
                def _():
                    if phase == 0:
                        zero_copy(e).start()
                    else:
                        zero_copy(e).wait()
                return carry
            lax.fori_loop(0, MOE_EXPERTS, body, 0)

            def tail(blk, carry):
                if phase == 0:
                    tail_copy(blk).start()
                else:
                    tail_copy(blk).wait()
                return carry
            lax.fori_loop(first_unused, MOE_ROWS // MOE_BLK, tail, 0)

    def row_copy(r, k):
        d = dest_ref[0, 0, k * tm + r]
        return pltpu.make_async_copy(h_ref.at[pl.ds(r, 1)], xs_ref.at[pl.ds(d, 1)], sem)

    def start(r, carry):
        row_copy(r, 0).start()
        row_copy(r, 1).start()
        return carry

    def wait(r, carry):
        row_copy(r, 0).wait()
        row_copy(r, 1).wait()
        return carry

    lax.fori_loop(0, tm, start, 0)
    lax.fori_loop(0, tm, wait, 0)


def _dispatch(pend0, dest_tiles, h_packed, *, n_tiles):
    tm = TM_ROUTE
    return pl.pallas_call(
        _dispatch_kernel,
        grid_spec=pltpu.PrefetchScalarGridSpec(
            num_scalar_prefetch=1,
            grid=(n_tiles,),
            in_specs=[
                pl.BlockSpec((1, 1, MOE_TOP_K * tm), lambda i, pend: (i, 0, 0), memory_space=pltpu.SMEM),
                pl.BlockSpec((tm, PACKED), lambda i, pend: (i, 0)),
            ],
            out_specs=pl.BlockSpec(memory_space=pl.ANY),
            scratch_shapes=[pltpu.VMEM((MOE_BLK, PACKED), u32), pltpu.SemaphoreType.DMA(())],
        ),
        out_shape=jax.ShapeDtypeStruct((MOE_ROWS, PACKED), u32),
        compiler_params=_cparams(("arbitrary",)),
        name="moe_dispatch",
    )(pend0, dest_tiles, h_packed)


def _expert_kernel(be_ref, nv_ref, xs_ref, wg_ref, wu_ref, wd_ref, ys_ref):
    @pl.when(pl.program_id(0) < nv_ref[0])
    def _():
        words = xs_ref[...]
        lo = lax.bitcast_convert_type(words << 16, f32)
        hi = lax.bitcast_convert_type(words & jnp.uint32(0xFFFF0000), f32)
        x = jnp.concatenate([lo, hi], axis=1).astype(bf16)
        gate = _dot(x, wg_ref[0].astype(bf16))
        up = _dot(x, wu_ref[0].astype(bf16))
        h = gate * jax.nn.sigmoid(gate) * up
        ys_ref[...] = _dot(h.astype(bf16), wd_ref[0].astype(bf16))

    @pl.when(pl.program_id(0) >= nv_ref[0])
    def _():
        ys_ref[...] = jnp.zeros_like(ys_ref)


def _experts(block_expert, n_valid, xs, w_gate, w_up, w_down):
    n_blocks = MOE_ROWS // MOE_BLK

    def row(i, be, nv):
        return jnp.minimum(i, nv[0] - 1)

    return pl.pallas_call(
        _expert_kernel,
        grid_spec=pltpu.PrefetchScalarGridSpec(
            num_scalar_prefetch=2,
            grid=(n_blocks,),
            in_specs=[
                pl.BlockSpec((MOE_BLK, PACKED), lambda i, be, nv: (row(i, be, nv), 0)),
                pl.BlockSpec((1, D_MODEL, MOE_D_FF), lambda i, be, nv: (be[i], 0, 0)),
                pl.BlockSpec((1, D_MODEL, MOE_D_FF), lambda i, be, nv: (be[i], 0, 0)),
                pl.BlockSpec((1, MOE_D_FF, D_MODEL), lambda i, be, nv: (be[i], 0, 0)),
            ],
            out_specs=pl.BlockSpec((MOE_BLK, D_MODEL), lambda i, be, nv: (i, 0)),
        ),
        out_shape=jax.ShapeDtypeStruct((MOE_ROWS, D_MODEL), f32),
        compiler_params=_cparams(("arbitrary",)),
        name="moe_experts",
    )(block_expert, n_valid, xs, w_gate, w_up, w_down)


def _combine_kernel(dest_ref, x_ref, info_ref, mod_ref, ys_ref, o_ref, buf_ref, sem):
    tm = TM_COMBINE

    def row_copy(r, k):
        d = dest_ref[0, 0, k * tm + r]
        return pltpu.make_async_copy(ys_ref.at[pl.ds(d, 1)], buf_ref.at[k, pl.ds(r, 1)], sem)

    def start(r, carry):
        row_copy(r, 0).start()
        row_copy(r, 1).start()
        return carry

    def wait(r, carry):
        row_copy(r, 0).wait()
        row_copy(r, 1).wait()
        return carry

    lax.fori_loop(0, tm, start, 0)
    lax.fori_loop(0, tm, wait, 0)

    info = info_ref[...]
    w1 = info[:, LANE_W1:LANE_W1 + 1]
    w2 = info[:, LANE_W2:LANE_W2 + 1]
    out = buf_ref[0] * w1 + buf_ref[1] * w2
    o_ref[...] = x_ref[...] + mod_ref[0, 5:6, :] * out


def _combine(dest_tiles, x, info, mod_l, ys, *, n_tiles, mod_sel):
    tm = TM_COMBINE
    return pl.pallas_call(
        _combine_kernel,
        grid=(n_tiles,),
        in_specs=[
            pl.BlockSpec((1, 1, MOE_TOP_K * tm), lambda i: (i, 0, 0), memory_space=pltpu.SMEM),
            pl.BlockSpec((tm, D_MODEL), lambda i: (i, 0)),
            pl.BlockSpec((tm, ROUTE_LANES), lambda i: (i, 0)),
            pl.BlockSpec((1, N_MOD, D_MODEL), lambda i: (mod_sel(i), 0, 0)),
            pl.BlockSpec(memory_space=pl.ANY),
        ],
        out_specs=pl.BlockSpec((tm, D_MODEL), lambda i: (i, 0)),
        out_shape=jax.ShapeDtypeStruct(x.shape, f32),
        scratch_shapes=[pltpu.VMEM((MOE_TOP_K, tm, D_MODEL), f32), pltpu.SemaphoreType.DMA(())],
        input_output_aliases={1: 0},
        compiler_params=_cparams(("arbitrary",)),
        name="moe_combine",
    )(dest_tiles, x, info, mod_l, ys)


def _moe(x, mod_l, g2, w_route, b_route, w_gate, w_up, w_down, *, with_ctx):
    n_tok = N_TOK if with_ctx else N_LAT
    rt = n_tok // TM_ROUTE
    lat_rt = N_LAT // TM_ROUTE
    h_packed, info, cnt = _route(
        x, mod_l, g2, w_route, b_route, n_tiles=rt,
        mod_sel=lambda i: jnp.where(i < lat_rt, i // (SEQ // TM_ROUTE), BATCH))

    counts = cnt[0, :MOE_EXPERTS].astype(i32)
    padded = (counts + MOE_BLK - 1) // MOE_BLK * MOE_BLK
    pend = jnp.cumsum(padded)
    pend0 = jnp.concatenate([jnp.zeros((1,), i32), pend]).astype(i32)
    expert = info[:, LANE_E1:LANE_E2 + 1].astype(i32)
    rank = info[:, LANE_R1:LANE_R2 + 1].astype(i32)
    dest = pend0[expert] + rank
    n_blocks = MOE_ROWS // MOE_BLK
    block_start = jnp.arange(n_blocks, dtype=i32) * MOE_BLK
    block_expert = jnp.minimum(jnp.searchsorted(pend, block_start, side="right"), MOE_EXPERTS - 1).astype(i32)
    n_valid = (pend[-1:] // MOE_BLK).astype(i32)

    def tiles(tm):
        return dest.reshape(n_tok // tm, tm, MOE_TOP_K).transpose(0, 2, 1).reshape(n_tok // tm, 1, MOE_TOP_K * tm)

    xs = _dispatch(pend0, tiles(TM_ROUTE), h_packed, n_tiles=rt)
    ys = _experts(block_expert, n_valid, xs, w_gate, w_up, w_down)
    ct = n_tok // TM_COMBINE
    lat_ct = N_LAT // TM_COMBINE
    return _combine(tiles(TM_COMBINE), x, info, mod_l, ys, n_tiles=ct,
                    mod_sel=lambda i: jnp.where(i < lat_ct, i // (SEQ // TM_COMBINE), BATCH))


def _final_norm_kernel(x_ref, g_ref, o_ref):
    x = x_ref[...]
    ms = jnp.mean(x * x, axis=-1, keepdims=True)
    o_ref[...] = x * lax.rsqrt(ms + NORM_EPS) * g_ref[...]


def _final_norm(x, g):
    tm = ROW_CHUNK
    return pl.pallas_call(
        _final_norm_kernel,
        grid=(N_LAT // tm,),
        in_specs=[pl.BlockSpec((tm, D_MODEL), lambda i: (i, 0)), pl.BlockSpec((1, D_MODEL), lambda i: (0, 0))],
        out_specs=pl.BlockSpec((tm, D_MODEL), lambda i: (i, 0)),
        out_shape=jax.ShapeDtypeStruct((N_LAT, D_MODEL), f32),
        compiler_params=_cparams(("parallel",)),
        name="final_norm",
    )(x, g)


def _rope_tables():
    rows = SEQ // GRID_W
    row = jnp.repeat(jnp.arange(rows, dtype=f32), GRID_W)
    col = jnp.tile(jnp.arange(GRID_W, dtype=f32), rows)
    n_freq = HEAD_DIM // 4
    inv_freq = ROPE_BASE ** (-jnp.arange(n_freq, dtype=f32) / n_freq)
    ang = jnp.concatenate([row[:, None] * inv_freq, col[:, None] * inv_freq], axis=-1)
    cos, sin = jnp.cos(ang), jnp.sin(ang)
    return jnp.concatenate([cos, cos], axis=-1), jnp.concatenate([-sin, sin], axis=-1)


def kernel(x, c, ctx, c_ctx, w_mod, b_mod, norm1_g, norm2_g, w_in, attn_sink, pool_w, pool_scale, ret_decay_fwd, ret_decay_bwd, w_br_attn, w_br_pool, w_br_ret, w_out, w_route_group, b_route_group, w_route_expert, b_route_expert, w_expert_gate, w_expert_up, w_expert_down, final_norm_g):
    assert x.shape == (BATCH, SEQ, D_MODEL) and ctx.shape == (BATCH, CTX_LEN, D_MODEL)
    cos_full, sin_signed = _rope_tables()

    cond8 = jnp.concatenate([c, c_ctx[None, :], jnp.zeros((8 - BATCH - 1, D_MODEL), f32)], axis=0)
    mod = _modulation(cond8, w_mod, b_mod).reshape(DEPTH, 8, N_MOD, D_MODEL)

    lg_fwd = jax.nn.log_sigmoid(ret_decay_fwd.astype(f32))
    lg_bwd = jax.nn.log_sigmoid(ret_decay_bwd.astype(f32))
    pad = ROUTE_LANES - MOE_GROUPS - MOE_EXPERTS
    w_route = jnp.concatenate(
        [w_route_group, w_route_expert, jnp.zeros((DEPTH, D_MODEL, pad), f32)], axis=-1)
    b_route = jnp.concatenate([b_route_group, b_route_expert, jnp.zeros((DEPTH, pad), f32)], axis=-1)

    tok = jnp.concatenate([x.reshape(N_LAT, D_MODEL), ctx.reshape(N_CTX, D_MODEL)], axis=0)
    lat_tiles = N_LAT // TM_LAT
    ctx_block0 = N_LAT // TM_CTX
    sel_lat = lambda i: i // (SEQ // TM_LAT)
    sel_ctx = lambda i: BATCH

    for l in range(DEPTH):
        with_ctx = l < DEPTH - 1
        mod_l = mod[l]
        g1 = norm1_g[l][None, :]
        p_lat = _norm_proj(tok, mod_l, g1, w_in[l], tm=TM_LAT, row_block0=0, n_tiles=lat_tiles, mod_sel=sel_lat)
        p_ctx = _norm_proj(tok, mod_l, g1, w_in[l], tm=TM_CTX, row_block0=ctx_block0, n_tiles=1, mod_sel=sel_ctx)

        ya_lat = _attn_lat(p_lat, p_ctx, attn_sink[l], cos_full, sin_signed)
        yp_lat = _pool(p_lat, pool_w[l], pool_scale[l][None, :], n_seq=BATCH, seq_len=SEQ)
        yr_lat, yr_ctx = _retention(p_lat, p_ctx, lg_fwd[l], lg_bwd[l])

        merged = _merge(ya_lat, yp_lat, yr_lat, p_lat, w_br_attn[l], w_br_pool[l], w_br_ret[l],
                        tm=TM_LAT, n_tiles=lat_tiles)
        tok = _out_proj(merged, w_out[l], tok, mod_l, tm=TM_LAT, row_block0=0, n_tiles=lat_tiles, mod_sel=sel_lat)
        if with_ctx:
            ya_ctx = _attn_ctx(p_ctx, attn_sink[l])
            yp_ctx = _pool(p_ctx, pool_w[l], pool_scale[l][None, :], n_seq=BATCH, seq_len=CTX_LEN)
            merged_c = _merge(ya_ctx, yp_ctx, yr_ctx, p_ctx, w_br_attn[l], w_br_pool[l], w_br_ret[l],
                              tm=TM_CTX, n_tiles=1)
            tok = _out_proj(merged_c, w_out[l], tok, mod_l, tm=TM_CTX, row_block0=ctx_block0, n_tiles=1,
                            mod_sel=sel_ctx)

        tok = _moe(tok, mod_l, norm2_g[l][None, :], w_route[l], b_route[l][None, :],
                   w_expert_gate[l], w_expert_up[l], w_expert_down[l], with_ctx=with_ctx)

    out = _final_norm(tok, final_norm_g[None, :])
    return out.reshape(BATCH, SEQ, D_MODEL)
```

```python
import functools

import jax
import jax.numpy as jnp
from jax import lax
from jax.experimental import pallas as pl
from jax.experimental.pallas import tpu as pltpu

f32 = jnp.float32
bf16 = jnp.bfloat16
i32 = jnp.int32
u32 = jnp.uint32

D_MODEL = 2048
BATCH = 2
SEQ = 4096
DEPTH = 4
GRID_W = 64
CTX_LEN = 256
NORM_EPS = 1e-6
N_MOD = 6
HEAD_DIM = 128
ATTN_HEADS = 8
ATTN_KV_HEADS = 2
ATTN_GROUP = ATTN_HEADS // ATTN_KV_HEADS
ATTN_BLOCK = 128
ROPE_BASE = 10000.0
POOL_GROUPS = 4
POOL_GROUP_DIM = 128
POOL_WIDTH = POOL_GROUPS * POOL_GROUP_DIM
POOL_SPANS = (2, 4, 8, 16)
RET_HEADS = 4
RET_QK_DIM = 64
RET_V_DIM = 128
N_BRANCHES = 3
ATTN_Q_WIDTH = ATTN_HEADS * HEAD_DIM
ATTN_KV_WIDTH = ATTN_KV_HEADS * HEAD_DIM
RET_QK_WIDTH = RET_HEADS * RET_QK_DIM
RET_V_WIDTH = RET_HEADS * RET_V_DIM
IN_SIZES = (ATTN_Q_WIDTH, ATTN_KV_WIDTH, ATTN_KV_WIDTH, POOL_WIDTH,
            RET_QK_WIDTH, RET_QK_WIDTH, RET_V_WIDTH, RET_V_WIDTH, N_BRANCHES * D_MODEL)
IN_WIDTH = sum(IN_SIZES)
MOE_GROUPS = 4
MOE_EXPERTS_PER_GROUP = 8
MOE_EXPERTS = MOE_GROUPS * MOE_EXPERTS_PER_GROUP
MOE_TOP_K = 2
MOE_D_FF = 512

COL_QA = 0
COL_KA = COL_QA + ATTN_Q_WIDTH
COL_VA = COL_KA + ATTN_KV_WIDTH
COL_U = COL_VA + ATTN_KV_WIDTH
COL_QR = COL_U + POOL_WIDTH
COL_KR = COL_QR + RET_QK_WIDTH
COL_VR = COL_KR + RET_QK_WIDTH
COL_GR = COL_VR + RET_V_WIDTH
COL_GATES = COL_GR + RET_V_WIDTH

V7X_LANES = 128
V7X_VMEM_LIMIT_BYTES = 56 * 1024 * 1024

N_LAT = BATCH * SEQ
N_CTX = BATCH * CTX_LEN
N_TOK = N_LAT + N_CTX
TM_LAT = 1024
TM_CTX = N_CTX
TN_PROJ = 512
TN_MERGE = 512
TN_OUT = 512
TN_MOD = 2048
ROW_CHUNK = 128
RET_CHUNK = 256
TM_ROUTE = 512
TM_COMBINE = 256
MOE_BLK = 256
MOE_ROWS = (N_TOK * MOE_TOP_K + MOE_EXPERTS * (MOE_BLK - 1)) // MOE_BLK * MOE_BLK
PACKED = D_MODEL // 2
XS_SUB = PACKED // V7X_LANES
YS_SUB = D_MODEL // V7X_LANES

assert CTX_LEN == RET_CHUNK and SEQ % RET_CHUNK == 0
assert IN_WIDTH % TN_PROJ == 0 and COL_GATES % TN_MERGE == 0


def _cparams(sem, vmem=V7X_VMEM_LIMIT_BYTES):
    return pltpu.CompilerParams(dimension_semantics=sem, vmem_limit_bytes=vmem)


def _dot(a, b):
    return jnp.dot(a, b, preferred_element_type=f32)


def _dot_nt(a, b):
    return lax.dot_general(a, b, (((1,), (1,)), ((), ())), preferred_element_type=f32)


def _dot_tn(a, b):
    return lax.dot_general(a, b, (((0,), (0,)), ((), ())), preferred_element_type=f32)


def _mod_kernel(cond_ref, w_ref, b_ref, o_ref):
    c = cond_ref[...]
    s = (c * jax.nn.sigmoid(c)).astype(bf16)
    o_ref[0] = _dot(s, w_ref[0].astype(bf16)) + b_ref[0]


def _modulation(cond8, w_mod, b_mod):
    n = N_MOD * D_MODEL
    return pl.pallas_call(
        _mod_kernel,
        grid=(DEPTH, n // TN_MOD),
        in_specs=[
            pl.BlockSpec((8, D_MODEL), lambda l, j: (0, 0)),
            pl.BlockSpec((1, D_MODEL, TN_MOD), lambda l, j: (l, 0, j)),
            pl.BlockSpec((1, 1, TN_MOD), lambda l, j: (l, 0, j)),
        ],
        out_specs=pl.BlockSpec((1, 8, TN_MOD), lambda l, j: (l, 0, j)),
        out_shape=jax.ShapeDtypeStruct((DEPTH, 8, n), f32),
        compiler_params=_cparams(("parallel", "parallel")),
        name="modulation",
    )(cond8, w_mod, b_mod.reshape(DEPTH, 1, n))


def _norm_mod_rows(x, g, shift, scale):
    ms = jnp.mean(x * x, axis=-1, keepdims=True)
    y = x * lax.rsqrt(ms + NORM_EPS) * g
    return y * (1.0 + scale) + shift


def _norm_proj_kernel(x_ref, mod_ref, g_ref, w_ref, o_ref, h_ref, *, tm):
    @pl.when(pl.program_id(1) == 0)
    def _():
        g = g_ref[...]
        shift = mod_ref[0, 0:1, :]
        scale = mod_ref[0, 1:2, :]

        def body(r, carry):
            rows = pl.ds(pl.multiple_of(r * ROW_CHUNK, ROW_CHUNK), ROW_CHUNK)
            h_ref[rows, :] = _norm_mod_rows(x_ref[rows, :], g, shift, scale).astype(bf16)
            return carry

        lax.fori_loop(0, tm // ROW_CHUNK, body, 0)

    o_ref[...] = _dot(h_ref[...], w_ref[...].astype(bf16))


def _norm_proj(x, mod_l, g1, w_in, *, layer, tm, row_block0, n_tiles, mod_sel):
    return pl.pallas_call(
        functools.partial(_norm_proj_kernel, tm=tm),
        grid=(n_tiles, IN_WIDTH // TN_PROJ),
        in_specs=[
            pl.BlockSpec((tm, D_MODEL), lambda i, j: (row_block0 + i, 0)),
            pl.BlockSpec((1, N_MOD, D_MODEL), lambda i, j: (mod_sel(i), 0, 0)),
            pl.BlockSpec((1, D_MODEL), lambda i, j: (0, 0)),
            pl.BlockSpec((None, D_MODEL, TN_PROJ), lambda i, j: (layer, 0, j)),
        ],
        out_specs=pl.BlockSpec((tm, TN_PROJ), lambda i, j: (i, j)),
        out_shape=jax.ShapeDtypeStruct((n_tiles * tm, IN_WIDTH), f32),
        scratch_shapes=[pltpu.VMEM((tm, D_MODEL), bf16)],
        compiler_params=_cparams(("parallel", "arbitrary")),
        name="norm_proj",
    )(x, mod_l, g1, w_in)


def _rope(t, cos_full, sin_signed):
    return t * cos_full + pltpu.roll(t, HEAD_DIM // 2, 1) * sin_signed


def _sink_column(sink_ref, g, rows_per_head):
    n = ATTN_GROUP * rows_per_head
    row = lax.broadcasted_iota(i32, (n, 1), 0)
    col = jnp.full((n, 1), sink_ref[g * ATTN_GROUP + ATTN_GROUP - 1], f32)
    for j in range(ATTN_GROUP - 2, -1, -1):
        col = jnp.where(row < (j + 1) * rows_per_head, sink_ref[g * ATTN_GROUP + j], col)
    return col


def _softmax_pv(scores, values, sink_col):
    m = sink_col
    for s in scores:
        m = jnp.maximum(m, jnp.max(s, axis=-1, keepdims=True))
    es = [jnp.exp(s - m) for s in scores]
    denom = jnp.exp(sink_col - m)
    for e in es:
        denom = denom + jnp.sum(e, axis=-1, keepdims=True)
    inv = 1.0 / denom
    out = None
    for e, v in zip(es, values):
        o = _dot((e * inv).astype(bf16), v)
        out = o if out is None else out + o
    return out


def _attn_lat_kernel(sink_ref, q_ref, kp_ref, kc_ref, kn_ref, vp_ref, vc_ref, vn_ref, kx_ref, vx_ref,
                     cq_ref, sq_ref, cp_ref, sp_ref, cn_ref, sn_ref, o_ref, *, n_blocks):
    n = pl.program_id(1)
    blk = ATTN_BLOCK
    scale = HEAD_DIM ** -0.5
    cq, sq = cq_ref[...], sq_ref[...]
    cp, sp = cp_ref[...], sp_ref[...]
    cn, sn = cn_ref[...], sn_ref[...]
    rows = ATTN_GROUP * blk
    qi = lax.broadcasted_iota(i32, (rows, blk), 0) % blk
    kj = lax.broadcasted_iota(i32, (rows, blk), 1)
    neg = jnp.float32(-jnp.inf)
    keep_prev = (kj >= qi) & (n > 0)
    keep_next = (kj <= qi) & (n < n_blocks - 1)

    for g in range(ATTN_KV_HEADS):
        kv = slice(g * HEAD_DIM, (g + 1) * HEAD_DIM)
        qs = jnp.concatenate(
            [(_rope(q_ref[:, (g * ATTN_GROUP + j) * HEAD_DIM:(g * ATTN_GROUP + j + 1) * HEAD_DIM], cq, sq)
              * scale).astype(bf16) for j in range(ATTN_GROUP)],
            axis=0)
        k_prev = _rope(kp_ref[:, kv], cp, sp).astype(bf16)
        k_cur = _rope(kc_ref[:, kv], cq, sq).astype(bf16)
        k_next = _rope(kn_ref[:, kv], cn, sn).astype(bf16)
        k_ctx = kx_ref[:, kv].astype(bf16)
        s_prev = jnp.where(keep_prev, _dot_nt(qs, k_prev), neg)
        s_cur = _dot_nt(qs, k_cur)
        s_next = jnp.where(keep_next, _dot_nt(qs, k_next), neg)
        s_ctx = _dot_nt(qs, k_ctx)
        out = _softmax_pv(
            [s_prev, s_cur, s_next, s_ctx],
            [vp_ref[:, kv].astype(bf16), vc_ref[:, kv].astype(bf16), vn_ref[:, kv].astype(bf16),
             vx_ref[:, kv].astype(bf16)],
            _sink_column(sink_ref, g, blk))
        for j in range(ATTN_GROUP):
            h = g * ATTN_GROUP + j
            o_ref[:, h * HEAD_DIM:(h + 1) * HEAD_DIM] = out[j * blk:(j + 1) * blk, :].astype(o_ref.dtype)


def _attn_lat(p_lat, p_ctx, sink, cos_full, sin_signed):
    nb = SEQ // ATTN_BLOCK
    blk = ATTN_BLOCK
    kcol = COL_KA // ATTN_KV_WIDTH
    vcol = COL_VA // ATTN_KV_WIDTH

    def kv_spec(col, shift):
        return pl.BlockSpec((blk, ATTN_KV_WIDTH), lambda b, n: (b * nb + jnp.clip(n + shift, 0, nb - 1), col))

    def tab_spec(shift):
        return pl.BlockSpec((blk, HEAD_DIM), lambda b, n: (jnp.clip(n + shift, 0, nb - 1), 0))

    return pl.pallas_call(
        functools.partial(_attn_lat_kernel, n_blocks=nb),
        grid=(BATCH, nb),
        in_specs=[
            pl.BlockSpec(memory_space=pltpu.SMEM),
            pl.BlockSpec((blk, ATTN_Q_WIDTH), lambda b, n: (b * nb + n, 0)),
            kv_spec(kcol, -1), kv_spec(kcol, 0), kv_spec(kcol, 1),
            kv_spec(vcol, -1), kv_spec(vcol, 0), kv_spec(vcol, 1),
            pl.BlockSpec((CTX_LEN, ATTN_KV_WIDTH), lambda b, n: (b, kcol)),
            pl.BlockSpec((CTX_LEN, ATTN_KV_WIDTH), lambda b, n: (b, vcol)),
            tab_spec(0), tab_spec(0), tab_spec(-1), tab_spec(-1), tab_spec(1), tab_spec(1),
        ],
        out_specs=pl.BlockSpec((blk, ATTN_Q_WIDTH), lambda b, n: (b * nb + n, 0)),
        out_shape=jax.ShapeDtypeStruct((N_LAT, ATTN_Q_WIDTH), bf16),
        compiler_params=_cparams(("parallel", "arbitrary")),
        name="attn_lat",
    )(sink, p_lat, p_lat, p_lat, p_lat, p_lat, p_lat, p_lat, p_ctx, p_ctx,
      cos_full, sin_signed, cos_full, sin_signed, cos_full, sin_signed)


def _attn_ctx_kernel(sink_ref, q_ref, kx_ref, vx_ref, o_ref):
    g = pl.program_id(1)
    scale = HEAD_DIM ** -0.5
    q = q_ref[...]
    qs = jnp.concatenate(
        [(q[:, j * HEAD_DIM:(j + 1) * HEAD_DIM] * scale).astype(bf16) for j in range(ATTN_GROUP)], axis=0)
    s_ctx = _dot_nt(qs, kx_ref[...].astype(bf16))
    out = _softmax_pv([s_ctx], [vx_ref[...].astype(bf16)], _sink_column(sink_ref, g, CTX_LEN))
    for j in range(ATTN_GROUP):
        o_ref[:, j * HEAD_DIM:(j + 1) * HEAD_DIM] = out[j * CTX_LEN:(j + 1) * CTX_LEN, :].astype(o_ref.dtype)


def _attn_ctx(p_ctx, sink):
    gw = ATTN_GROUP * HEAD_DIM
    kcol = COL_KA // HEAD_DIM
    vcol = COL_VA // HEAD_DIM
    return pl.pallas_call(
        _attn_ctx_kernel,
        grid=(BATCH, ATTN_KV_HEADS),
        in_specs=[
            pl.BlockSpec(memory_space=pltpu.SMEM),
            pl.BlockSpec((CTX_LEN, gw), lambda b, g: (b, g)),
            pl.BlockSpec((CTX_LEN, HEAD_DIM), lambda b, g: (b, kcol + g)),
            pl.BlockSpec((CTX_LEN, HEAD_DIM), lambda b, g: (b, vcol + g)),
        ],
        out_specs=pl.BlockSpec((CTX_LEN, gw), lambda b, g: (b, g)),
        out_shape=jax.ShapeDtypeStruct((N_CTX, ATTN_Q_WIDTH), bf16),
        compiler_params=_cparams(("parallel", "parallel")),
        name="attn_ctx",
    )(sink, p_ctx, p_ctx, p_ctx)


POOL_HALO = 8
POOL_ROWS = 256


def _pool_kernel(u_ref, w_ref, s_ref, o_ref, pad_ref, *, seq_len):
    zeros = jnp.zeros((POOL_HALO, POOL_GROUP_DIM), f32)
    for gi in range(POOL_GROUPS):
        radius = POOL_SPANS[gi] // 2
        cols = slice(gi * POOL_GROUP_DIM, (gi + 1) * POOL_GROUP_DIM)
        pad_ref[0:POOL_HALO, :] = zeros
        pad_ref[POOL_HALO + seq_len:POOL_HALO + seq_len + POOL_HALO, :] = zeros
        pad_ref[POOL_HALO:POOL_HALO + seq_len, :] = u_ref[:, cols]
        w = w_ref[gi].astype(bf16)
        sc = s_ref[:, cols]
        for c in range(seq_len // POOL_ROWS):
            base = c * POOL_ROWS
            acc = pad_ref[POOL_HALO + base:POOL_HALO + base + POOL_ROWS, :]
            center = acc
            for d in range(1, radius + 1):
                acc = acc + pad_ref[POOL_HALO + base - d:POOL_HALO + base - d + POOL_ROWS, :]
                acc = acc + pad_ref[POOL_HALO + base + d:POOL_HALO + base + d + POOL_ROWS, :]
            t = lax.broadcasted_iota(i32, (POOL_ROWS, 1), 0) + base
            lo = jnp.maximum(t - radius, 0)
            hi = jnp.minimum(t + radius + 1, seq_len)
            count = (hi - lo).astype(f32)
            pooled = (acc / count - center).astype(bf16)
            o_ref[base:base + POOL_ROWS, cols] = (_dot(pooled, w) * sc).astype(o_ref.dtype)


def _pool(p, pool_w, pool_scale, *, n_seq, seq_len):
    ucol = COL_U // POOL_WIDTH
    return pl.pallas_call(
        functools.partial(_pool_kernel, seq_len=seq_len),
        grid=(n_seq,),
        in_specs=[
            pl.BlockSpec((seq_len, POOL_WIDTH), lambda s: (s, ucol)),
            pl.BlockSpec((POOL_GROUPS, POOL_GROUP_DIM, POOL_GROUP_DIM), lambda s: (0, 0, 0)),
            pl.BlockSpec((1, POOL_WIDTH), lambda s: (0, 0)),
        ],
        out_specs=pl.BlockSpec((seq_len, POOL_WIDTH), lambda s: (s, 0)),
        out_shape=jax.ShapeDtypeStruct((n_seq * seq_len, POOL_WIDTH), bf16),
        scratch_shapes=[pltpu.VMEM((seq_len + 2 * POOL_HALO, POOL_GROUP_DIM), f32)],
        compiler_params=_cparams(("parallel",)),
        name="pool",
    )(p, pool_w, pool_scale)


def _ret_tables(lg_ref, intra_ref, qdec_ref, kdec_ref, *, reverse):
    c = RET_CHUNK
    i = lax.broadcasted_iota(i32, (c, c), 0).astype(f32)
    j = lax.broadcasted_iota(i32, (c, c), 1).astype(f32)
    pos = lax.broadcasted_iota(i32, (c, V7X_LANES), 0).astype(f32)
    for h in range(RET_HEADS):
        lg = lg_ref[h]
        if reverse:
            diff = j - i
            qd = c - pos
            kd = pos
        else:
            diff = i - j
            qd = pos + 1.0
            kd = c - 1.0 - pos
        intra_ref[h] = jnp.where(diff >= 0, jnp.exp(lg * jnp.maximum(diff, 0.0)), 0.0)
        qdec_ref[h] = jnp.exp(lg * qd)
        kdec_ref[h] = jnp.exp(lg * kd)


def _ret_chunk(q_ref, k_ref, v_ref, lg_ref, intra_ref, qdec_ref, kdec_ref, state_ref):
    outs = []
    k_scale = RET_QK_DIM ** -0.5
    for h in range(RET_HEADS):
        q = q_ref[:, h * RET_QK_DIM:(h + 1) * RET_QK_DIM].astype(bf16)
        k = k_ref[:, h * RET_QK_DIM:(h + 1) * RET_QK_DIM] * k_scale
        v = v_ref[:, h * RET_V_DIM:(h + 1) * RET_V_DIM].astype(bf16)
        scores = _dot_nt(q, k.astype(bf16)) * intra_ref[h]
        state = state_ref[h]
        out = _dot(scores.astype(bf16), v) + _dot(q, state.astype(bf16)) * qdec_ref[h]
        kd = (k * kdec_ref[h][:, :RET_QK_DIM]).astype(bf16)
        chunk_decay = jnp.exp(jnp.full((1, RET_V_DIM), lg_ref[h] * RET_CHUNK, f32))
        state_ref[h] = state * chunk_decay + _dot_tn(kd, v)
        outs.append(out)
    return outs


def _ret_bwd_kernel(lg_ref, ql_ref, kl_ref, vl_ref, qc_ref, kc_ref, vc_ref, ol_ref, oc_ref,
                    intra_ref, qdec_ref, kdec_ref, state_ref):
    t = pl.program_id(1)

    @pl.when((pl.program_id(0) == 0) & (t == 0))
    def _():
        _ret_tables(lg_ref, intra_ref, qdec_ref, kdec_ref, reverse=True)

    @pl.when(t == 0)
    def _():
        state_ref[...] = jnp.zeros_like(state_ref)
        outs = _ret_chunk(qc_ref, kc_ref, vc_ref, lg_ref, intra_ref, qdec_ref, kdec_ref, state_ref)
        for h in range(RET_HEADS):
            oc_ref[:, h * RET_V_DIM:(h + 1) * RET_V_DIM] = outs[h]

    @pl.when(t > 0)
    def _():
        outs = _ret_chunk(ql_ref, kl_ref, vl_ref, lg_ref, intra_ref, qdec_ref, kdec_ref, state_ref)
        for h in range(RET_HEADS):
            ol_ref[:, h * RET_V_DIM:(h + 1) * RET_V_DIM] = outs[h]


def _ret_finish(outs, yb_ref, g_ref, o_ref):
    for h in range(RET_HEADS):
        cols = slice(h * RET_V_DIM, (h + 1) * RET_V_DIM)
        y = outs[h] + yb_ref[:, cols]
        mu = jnp.mean(y, axis=-1, keepdims=True)
        yc = y - mu
        var = jnp.mean(yc * yc, axis=-1, keepdims=True)
        yn = yc * lax.rsqrt(var + NORM_EPS)
        g = g_ref[:, cols]
        o_ref[:, cols] = (g * jax.nn.sigmoid(g) * yn).astype(o_ref.dtype)


def _ret_fwd_kernel(lg_ref, ql_ref, kl_ref, vl_ref, gl_ref, ybl_ref, qc_ref, kc_ref, vc_ref, gc_ref, ybc_ref,
                    ol_ref, oc_ref, intra_ref, qdec_ref, kdec_ref, state_ref):
    t = pl.program_id(1)

    @pl.when((pl.program_id(0) == 0) & (t == 0))
    def _():
        _ret_tables(lg_ref, intra_ref, qdec_ref, kdec_ref, reverse=False)

    @pl.when(t == 0)
    def _():
        state_ref[...] = jnp.zeros_like(state_ref)
        outs = _ret_chunk(qc_ref, kc_ref, vc_ref, lg_ref, intra_ref, qdec_ref, kdec_ref, state_ref)
        _ret_finish(outs, ybc_ref, gc_ref, oc_ref)

    @pl.when(t > 0)
    def _():
        outs = _ret_chunk(ql_ref, kl_ref, vl_ref, lg_ref, intra_ref, qdec_ref, kdec_ref, state_ref)
        _ret_finish(outs, ybl_ref, gl_ref, ol_ref)


def _ret_scratch():
    c = RET_CHUNK
    return [
        pltpu.VMEM((RET_HEADS, c, c), f32),
        pltpu.VMEM((RET_HEADS, c, V7X_LANES), f32),
        pltpu.VMEM((RET_HEADS, c, V7X_LANES), f32),
        pltpu.VMEM((RET_HEADS, RET_QK_DIM, RET_V_DIM), f32),
    ]


def _retention(p_lat, p_ctx, lg_fwd, lg_bwd):
    c = RET_CHUNK
    nc = SEQ // c
    qcol, kcol = COL_QR // RET_QK_WIDTH, COL_KR // RET_QK_WIDTH
    vcol, gcol = COL_VR // RET_V_WIDTH, COL_GR // RET_V_WIDTH

    def lat_row(reverse):
        def f(b, t):
            s = jnp.maximum(t, 1)
            return b * nc + ((nc - s) if reverse else (s - 1))
        return f

    def specs(reverse, with_gate):
        row = lat_row(reverse)
        lat = [pl.BlockSpec((c, RET_QK_WIDTH), lambda b, t: (row(b, t), qcol)),
               pl.BlockSpec((c, RET_QK_WIDTH), lambda b, t: (row(b, t), kcol)),
               pl.BlockSpec((c, RET_V_WIDTH), lambda b, t: (row(b, t), vcol))]
        ctx = [pl.BlockSpec((c, RET_QK_WIDTH), lambda b, t: (b, qcol)),
               pl.BlockSpec((c, RET_QK_WIDTH), lambda b, t: (b, kcol)),
               pl.BlockSpec((c, RET_V_WIDTH), lambda b, t: (b, vcol))]
        if with_gate:
            lat += [pl.BlockSpec((c, RET_V_WIDTH), lambda b, t: (row(b, t), gcol)),
                    pl.BlockSpec((c, RET_V_WIDTH), lambda b, t: (row(b, t), 0))]
            ctx += [pl.BlockSpec((c, RET_V_WIDTH), lambda b, t: (b, gcol)),
                    pl.BlockSpec((c, RET_V_WIDTH), lambda b, t: (b, 0))]
        outs = [pl.BlockSpec((c, RET_V_WIDTH), lambda b, t: (row(b, t), 0)),
                pl.BlockSpec((c, RET_V_WIDTH), lambda b, t: (b, 0))]
        return [pl.BlockSpec(memory_space=pltpu.SMEM)] + lat + ctx, outs

    in_specs, out_specs = specs(True, False)
    yb_lat, yb_ctx = pl.pallas_call(
        _ret_bwd_kernel,
        grid=(BATCH, nc + 1),
        in_specs=in_specs,
        out_specs=out_specs,
        out_shape=[jax.ShapeDtypeStruct((N_LAT, RET_V_WIDTH), f32),
                   jax.ShapeDtypeStruct((N_CTX, RET_V_WIDTH), f32)],
        scratch_shapes=_ret_scratch(),
        compiler_params=_cparams(("arbitrary", "arbitrary")),
        name="retention_bwd",
    )(lg_bwd, p_lat, p_lat, p_lat, p_ctx, p_ctx, p_ctx)

    in_specs, out_specs = specs(False, True)
    return pl.pallas_call(
        _ret_fwd_kernel,
        grid=(BATCH, nc + 1),
        in_specs=in_specs,
        out_specs=out_specs,
        out_shape=[jax.ShapeDtypeStruct((N_LAT, RET_V_WIDTH), bf16),
                   jax.ShapeDtypeStruct((N_CTX, RET_V_WIDTH), bf16)],
        scratch_shapes=_ret_scratch(),
        compiler_params=_cparams(("arbitrary", "arbitrary")),
        name="retention_fwd",
    )(lg_fwd, p_lat, p_lat, p_lat, p_lat, yb_lat, p_ctx, p_ctx, p_ctx, p_ctx, yb_ctx)


def _merge_kernel(ya_ref, yp_ref, yr_ref, ga_ref, gp_ref, gr_ref, wa_ref, wp_ref, wr_ref, o_ref):
    acc = jax.nn.sigmoid(ga_ref[...]) * _dot(ya_ref[...], wa_ref[...].astype(bf16))
    acc = acc + jax.nn.sigmoid(gp_ref[...]) * _dot(yp_ref[...], wp_ref[...].astype(bf16))
    acc = acc + jax.nn.sigmoid(gr_ref[...]) * _dot(yr_ref[...], wr_ref[...].astype(bf16))
    o_ref[...] = acc.astype(o_ref.dtype)


def _merge(ya, yp, yr, p, w_a, w_p, w_r, *, layer, tm, n_tiles):
    gate0 = COL_GATES // TN_MERGE
    per = D_MODEL // TN_MERGE

    def gate_spec(br):
        return pl.BlockSpec((tm, TN_MERGE), lambda i, j: (i, gate0 + br * per + j))

    return pl.pallas_call(
        _merge_kernel,
        grid=(n_tiles, per),
        in_specs=[
            pl.BlockSpec((tm, ATTN_Q_WIDTH), lambda i, j: (i, 0)),
            pl.BlockSpec((tm, POOL_WIDTH), lambda i, j: (i, 0)),
            pl.BlockSpec((tm, RET_V_WIDTH), lambda i, j: (i, 0)),
            gate_spec(0), gate_spec(1), gate_spec(2),
            pl.BlockSpec((None, ATTN_Q_WIDTH, TN_MERGE), lambda i, j: (layer, 0, j)),
            pl.BlockSpec((None, POOL_WIDTH, TN_MERGE), lambda i, j: (layer, 0, j)),
            pl.BlockSpec((None, RET_V_WIDTH, TN_MERGE), lambda i, j: (layer, 0, j)),
        ],
        out_specs=pl.BlockSpec((tm, TN_MERGE), lambda i, j: (i, j)),
        out_shape=jax.ShapeDtypeStruct((n_tiles * tm, D_MODEL), bf16),
        compiler_params=_cparams(("parallel", "parallel")),
        name="merge",
    )(ya, yp, yr, p, p, p, w_a, w_p, w_r)


def _out_proj_kernel(m_ref, w_ref, x_ref, mod_ref, o_ref):
    y = _dot(m_ref[...], w_ref[...].astype(bf16))
    o_ref[...] = x_ref[...] + mod_ref[0, 2:3, :] * y


def _out_proj(merged, w_out, x, mod_l, *, layer, tm, row_block0, n_tiles, mod_sel):
    return pl.pallas_call(
        _out_proj_kernel,
        grid=(n_tiles, D_MODEL // TN_OUT),
        in_specs=[
            pl.BlockSpec((tm, D_MODEL), lambda i, j: (i, 0)),
            pl.BlockSpec((None, D_MODEL, TN_OUT), lambda i, j: (layer, 0, j)),
            pl.BlockSpec((tm, TN_OUT), lambda i, j: (row_block0 + i, j)),
            pl.BlockSpec((1, N_MOD, TN_OUT), lambda i, j: (mod_sel(i), 0, j)),
        ],
        out_specs=pl.BlockSpec((tm, TN_OUT), lambda i, j: (row_block0 + i, j)),
        out_shape=jax.ShapeDtypeStruct(x.shape, f32),
        input_output_aliases={2: 0},
        compiler_params=_cparams(("parallel", "parallel")),
        name="out_proj",
    )(merged, w_out, x, mod_l)


ROUTE_LANES = V7X_LANES
LANE_E1, LANE_E2, LANE_W1, LANE_W2, LANE_R1, LANE_R2 = 0, 1, 2, 3, 4, 5


def _route_kernel(x_ref, mod_ref, g_ref, w_ref, b_ref, h_ref, info_ref, cnt_ref, hf_ref, tri_ref, run_ref):
    tm = TM_ROUTE
    step = pl.program_id(0)

    @pl.when(step == 0)
    def _():
        run_ref[...] = jnp.zeros_like(run_ref)
        r = lax.broadcasted_iota(i32, (tm, tm), 0)
        c = lax.broadcasted_iota(i32, (tm, tm), 1)
        tri_ref[...] = jnp.where(c < r, 1.0, 0.0).astype(bf16)

    g = g_ref[...]
    shift = mod_ref[0, 3:4, :]
    scale = mod_ref[0, 4:5, :]

    def body(r, carry):
        rows = pl.ds(pl.multiple_of(r * ROW_CHUNK, ROW_CHUNK), ROW_CHUNK)
        hf_ref[rows, :] = _norm_mod_rows(x_ref[rows, :], g, shift, scale).astype(bf16)
        return carry

    lax.fori_loop(0, tm // ROW_CHUNK, body, 0)

    for c in range(tm // ROW_CHUNK):
        bits = lax.bitcast_convert_type(hf_ref[c * ROW_CHUNK:(c + 1) * ROW_CHUNK, :].astype(f32), u32)
        words = (bits[:, :PACKED] >> 16) | (bits[:, PACKED:] & jnp.uint32(0xFFFF0000))
        for s in range(XS_SUB):
            h_ref[pl.ds(c * ROW_CHUNK * XS_SUB + s, ROW_CHUNK, stride=XS_SUB), :] = (
                words[:, s * V7X_LANES:(s + 1) * V7X_LANES])

    logits = _dot(hf_ref[...], w_ref[...].astype(bf16)) + b_ref[...]
    lane = lax.broadcasted_iota(i32, (tm, ROUTE_LANES), 1)
    neg = jnp.float32(-jnp.inf)
    lane_f = lane.astype(f32)

    def first_argmax(vals):
        top = jnp.max(vals, axis=-1, keepdims=True)
        idx = jnp.min(jnp.where(vals == top, lane_f, float(ROUTE_LANES)), axis=-1, keepdims=True)
        return top, idx.astype(i32)

    g_logits = jnp.where(lane < MOE_GROUPS, logits, neg)
    g_top, grp = first_argmax(g_logits)
    p_grp = 1.0 / jnp.sum(jnp.exp(g_logits - g_top), axis=-1, keepdims=True)

    e_lane = lane - MOE_GROUPS
    in_grp = (e_lane >= 0) & (e_lane < MOE_EXPERTS) & (lax.shift_right_arithmetic(e_lane, 3) == grp)
    e_logits = jnp.where(in_grp, logits, neg)
    top1, idx1 = first_argmax(e_logits)
    e_logits2 = jnp.where(lane == idx1, neg, e_logits)
    top2, idx2 = first_argmax(e_logits2)
    a = jnp.exp(top2 - top1)
    w1 = (1.0 / (1.0 + a)) * p_grp
    w2 = (a / (1.0 + a)) * p_grp
    e1 = idx1 - MOE_GROUPS
    e2 = idx2 - MOE_GROUPS

    hot1 = lane == e1
    hot2 = lane == e2
    o1 = jnp.where(hot1, 1.0, 0.0)
    o2 = jnp.where(hot2, 1.0, 0.0)
    tri = tri_ref[...]
    before1 = _dot(tri, o1.astype(bf16))
    before2 = _dot(tri, o2.astype(bf16))
    c1 = jnp.sum(o1, axis=0, keepdims=True)
    c2 = jnp.sum(o2, axis=0, keepdims=True)
    run = run_ref[...]
    rank1 = jnp.sum(jnp.where(hot1, before1 + run, 0.0), axis=-1, keepdims=True)
    rank2 = jnp.sum(jnp.where(hot2, before2 + run + c1, 0.0), axis=-1, keepdims=True)
    run = run + c1 + c2
    run_ref[...] = run
    cnt_ref[...] = jnp.broadcast_to(run, cnt_ref.shape)

    info = jnp.where(lane == LANE_E1, e1.astype(f32), 0.0)
    info = jnp.where(lane == LANE_E2, e2.astype(f32), info)
    info = jnp.where(lane == LANE_W1, w1, info)
    info = jnp.where(lane == LANE_W2, w2, info)
    info = jnp.where(lane == LANE_R1, rank1, info)
    info = jnp.where(lane == LANE_R2, rank2, info)
    info_ref[...] = info


def _route(x, mod_l, g2, w_route, b_route, *, n_tiles, mod_sel):
    tm = TM_ROUTE
    return pl.pallas_call(
        _route_kernel,
        grid=(n_tiles,),
        in_specs=[
            pl.BlockSpec((tm, D_MODEL), lambda i: (i, 0)),
            pl.BlockSpec((1, N_MOD, D_MODEL), lambda i: (mod_sel(i), 0, 0)),
            pl.BlockSpec((1, D_MODEL), lambda i: (0, 0)),
            pl.BlockSpec((D_MODEL, ROUTE_LANES), lambda i: (0, 0)),
            pl.BlockSpec((1, ROUTE_LANES), lambda i: (0, 0)),
        ],
        out_specs=[
            pl.BlockSpec((tm * XS_SUB, V7X_LANES), lambda i: (i, 0)),
            pl.BlockSpec((tm, ROUTE_LANES), lambda i: (i, 0)),
            pl.BlockSpec((8, ROUTE_LANES), lambda i: (0, 0)),
        ],
        out_shape=[
            jax.ShapeDtypeStruct((n_tiles * tm * XS_SUB, V7X_LANES), u32),
            jax.ShapeDtypeStruct((n_tiles * tm, ROUTE_LANES), f32),
            jax.ShapeDtypeStruct((8, ROUTE_LANES), f32),
        ],
        scratch_shapes=[
            pltpu.VMEM((tm, D_MODEL), bf16),
            pltpu.VMEM((tm, tm), bf16),
            pltpu.VMEM((1, ROUTE_LANES), f32),
        ],
        compiler_params=_cparams(("arbitrary",)),
        name="moe_route",
    )(x, mod_l, g2, w_route, b_route)


def _dispatch_kernel(pend_ref, dest_ref, h_ref, xs_ref, zero_ref, sem):
    tm = TM_ROUTE
    step = pl.program_id(0)

    blk_rows = MOE_BLK * XS_SUB

    def zero_copy(e):
        start = pl.multiple_of((pend_ref[e + 1] - MOE_BLK) * XS_SUB, blk_rows)
        return pltpu.make_async_copy(zero_ref, xs_ref.at[pl.ds(start, blk_rows)], sem)

    def tail_copy(blk):
        start = pl.multiple_of(blk * blk_rows, blk_rows)
        return pltpu.make_async_copy(zero_ref, xs_ref.at[pl.ds(start, blk_rows)], sem)

    @pl.when(step == 0)
    def _():
        zero_ref[...] = jnp.zeros_like(zero_ref)
        first_unused = pend_ref[MOE_EXPERTS] // MOE_BLK
        for phase in range(2):
            def body(e, carry):
                @pl.when(pend_ref[e + 1] > pend_ref[e])
                def _():
                    if phase == 0:
                        zero_copy(e).start()
                    else:
                        zero_copy(e).wait()
                return carry
            lax.fori_loop(0, MOE_EXPERTS, body, 0)

            def tail(blk, carry):
                if phase == 0:
                    tail_copy(blk).start()
                else:
                    tail_copy(blk).wait()
                return carry
            lax.fori_loop(first_unused, MOE_ROWS // MOE_BLK, tail, 0)

    def row_copy(r, k):
        d = dest_ref[0, 0, k * tm + r]
        src = h_ref.at[pl.ds(pl.multiple_of(r * XS_SUB, XS_SUB), XS_SUB)]
        return pltpu.make_async_copy(src, xs_ref.at[pl.ds(pl.multiple_of(d * XS_SUB, XS_SUB), XS_SUB)], sem)

    def start(r, carry):
        row_copy(r, 0).start()
        row_copy(r, 1).start()
        return carry

    lax.fori_loop(0, tm, start, 0)
    for k in range(MOE_TOP_K):
        pltpu.make_async_copy(h_ref, xs_ref.at[pl.ds(0, tm * XS_SUB)], sem).wait()


def _dispatch(pend0, dest_tiles, h_packed, *, n_tiles):
    tm = TM_ROUTE
    return pl.pallas_call(
        _dispatch_kernel,
        grid_spec=pltpu.PrefetchScalarGridSpec(
            num_scalar_prefetch=1,
            grid=(n_tiles,),
            in_specs=[
                pl.BlockSpec((1, 1, MOE_TOP_K * tm), lambda i, pend: (i, 0, 0), memory_space=pltpu.SMEM),
                pl.BlockSpec((tm * XS_SUB, V7X_LANES), lambda i, pend: (i, 0)),
            ],
            out_specs=pl.BlockSpec(memory_space=pl.ANY),
            scratch_shapes=[pltpu.VMEM((MOE_BLK * XS_SUB, V7X_LANES), u32), pltpu.SemaphoreType.DMA(())],
        ),
        out_shape=jax.ShapeDtypeStruct((MOE_ROWS * XS_SUB, V7X_LANES), u32),
        compiler_params=_cparams(("arbitrary",)),
        name="moe_dispatch",
    )(pend0, dest_tiles, h_packed)


def _expert_kernel(be_ref, nv_ref, first_ref, slot_ref, next_ref, xs_ref, wg_hbm, wu_hbm, wd_hbm, ys_ref,
                   wg_buf, wu_buf, wd_buf, sems, *, layer):
    i = pl.program_id(0)

    def weight_copies(e, s):
        return (pltpu.make_async_copy(wg_hbm.at[layer, e], wg_buf.at[s], sems.at[s, 0]),
                pltpu.make_async_copy(wu_hbm.at[layer, e], wu_buf.at[s], sems.at[s, 1]),
                pltpu.make_async_copy(wd_hbm.at[layer, e], wd_buf.at[s], sems.at[s, 2]))

    @pl.when(i < nv_ref[0])
    def _():
        s = slot_ref[i]

        @pl.when(first_ref[i] == 1)
        def _():
            @pl.when(i == 0)
            def _():
                for cp in weight_copies(be_ref[0], 0):
                    cp.start()
            for cp in weight_copies(be_ref[i], s):
                cp.wait()

            @pl.when(next_ref[i] >= 0)
            def _():
                for cp in weight_copies(next_ref[i], 1 - s):
                    cp.start()

        parts = [xs_ref[pl.ds(t, MOE_BLK, stride=XS_SUB), :] for t in range(XS_SUB)]
        lo = [lax.bitcast_convert_type(p << 16, f32) for p in parts]
        hi = [lax.bitcast_convert_type(p & jnp.uint32(0xFFFF0000), f32) for p in parts]
        x = jnp.concatenate(lo + hi, axis=1).astype(bf16)
        gate = _dot(x, wg_buf[s].astype(bf16))
        up = _dot(x, wu_buf[s].astype(bf16))
        h = gate * jax.nn.sigmoid(gate) * up
        y = _dot(h.astype(bf16), wd_buf[s].astype(bf16))
        for t in range(YS_SUB):
            ys_ref[pl.ds(t, MOE_BLK, stride=YS_SUB), :] = y[:, t * V7X_LANES:(t + 1) * V7X_LANES]

    @pl.when(i >= nv_ref[0])
    def _():
        ys_ref[...] = jnp.zeros_like(ys_ref)


def _experts(block_expert, n_valid, first, slot, nxt, xs, w_gate, w_up, w_down, *, layer):
    n_blocks = MOE_ROWS // MOE_BLK

    def row(i, be, nv, *_):
        return jnp.minimum(i, nv[0] - 1)

    return pl.pallas_call(
        functools.partial(_expert_kernel, layer=layer),
        grid_spec=pltpu.PrefetchScalarGridSpec(
            num_scalar_prefetch=5,
            grid=(n_blocks,),
            in_specs=[
                pl.BlockSpec((MOE_BLK * XS_SUB, V7X_LANES), lambda i, *pf: (row(i, *pf), 0)),
                pl.BlockSpec(memory_space=pl.ANY),
                pl.BlockSpec(memory_space=pl.ANY),
                pl.BlockSpec(memory_space=pl.ANY),
            ],
            out_specs=pl.BlockSpec((MOE_BLK * YS_SUB, V7X_LANES), lambda i, *pf: (i, 0)),
            scratch_shapes=[
                pltpu.VMEM((2, D_MODEL, MOE_D_FF), f32),
                pltpu.VMEM((2, D_MODEL, MOE_D_FF), f32),
                pltpu.VMEM((2, MOE_D_FF, D_MODEL), f32),
                pltpu.SemaphoreType.DMA((2, 3)),
            ],
        ),
        out_shape=jax.ShapeDtypeStruct((MOE_ROWS * YS_SUB, V7X_LANES), f32),
        compiler_params=_cparams(("arbitrary",)),
        name="moe_experts",
    )(block_expert, n_valid, first, slot, nxt, xs, w_gate, w_up, w_down)


def _combine_kernel(dest_ref, x_ref, info_ref, mod_ref, ys_ref, o_ref, buf_ref, sem):
    tm = TM_COMBINE

    def row_copy(r, k):
        d = dest_ref[0, 0, k * tm + r]
        src = ys_ref.at[pl.ds(pl.multiple_of(d * YS_SUB, YS_SUB), YS_SUB)]
        return pltpu.make_async_copy(src, buf_ref.at[k, pl.ds(pl.multiple_of(r * YS_SUB, YS_SUB), YS_SUB)], sem)

    def start(r, carry):
        row_copy(r, 0).start()
        row_copy(r, 1).start()
        return carry

    lax.fori_loop(0, tm, start, 0)
    for k in range(MOE_TOP_K):
        pltpu.make_async_copy(ys_ref.at[pl.ds(0, tm * YS_SUB)], buf_ref.at[k], sem).wait()

    info = info_ref[...]
    w1 = info[:, LANE_W1:LANE_W1 + 1]
    w2 = info[:, LANE_W2:LANE_W2 + 1]
    for t in range(YS_SUB):
        cols = slice(t * V7X_LANES, (t + 1) * V7X_LANES)
        rows = pl.ds(t, tm, stride=YS_SUB)
        out = buf_ref[0, rows, :] * w1 + buf_ref[1, rows, :] * w2
        o_ref[:, cols] = x_ref[:, cols] + mod_ref[0, 5:6, cols] * out


def _combine(dest_tiles, x, info, mod_l, ys, *, n_tiles, mod_sel):
    tm = TM_COMBINE
    return pl.pallas_call(
        _combine_kernel,
        grid=(n_tiles,),
        in_specs=[
            pl.BlockSpec((1, 1, MOE_TOP_K * tm), lambda i: (i, 0, 0), memory_space=pltpu.SMEM),
            pl.BlockSpec((tm, D_MODEL), lambda i: (i, 0)),
            pl.BlockSpec((tm, ROUTE_LANES), lambda i: (i, 0)),
            pl.BlockSpec((1, N_MOD, D_MODEL), lambda i: (mod_sel(i), 0, 0)),
            pl.BlockSpec(memory_space=pl.ANY),
        ],
        out_specs=pl.BlockSpec((tm, D_MODEL), lambda i: (i, 0)),
        out_shape=jax.ShapeDtypeStruct(x.shape, f32),
        scratch_shapes=[pltpu.VMEM((MOE_TOP_K, tm * YS_SUB, V7X_LANES), f32), pltpu.SemaphoreType.DMA(())],
        input_output_aliases={1: 0},
        compiler_params=_cparams(("arbitrary",)),
        name="moe_combine",
    )(dest_tiles, x, info, mod_l, ys)


def _moe(x, mod_l, g2, w_route, b_route, w_gate, w_up, w_down, *, layer, with_ctx):
    n_tok = N_TOK if with_ctx else N_LAT
    rt = n_tok // TM_ROUTE
    lat_rt = N_LAT // TM_ROUTE
    h_packed, info, cnt = _route(
        x, mod_l, g2, w_route, b_route, n_tiles=rt,
        mod_sel=lambda i: jnp.where(i < lat_rt, i // (SEQ // TM_ROUTE), BATCH))

    counts = cnt[0, :MOE_EXPERTS].astype(i32)
    padded = (counts + MOE_BLK - 1) // MOE_BLK * MOE_BLK
    pend = jnp.cumsum(padded)
    pend0 = jnp.concatenate([jnp.zeros((1,), i32), pend]).astype(i32)
    expert = info[:, LANE_E1:LANE_E2 + 1].astype(i32)
    rank = info[:, LANE_R1:LANE_R2 + 1].astype(i32)
    dest = pend0[expert] + rank
    n_blocks = MOE_ROWS // MOE_BLK
    block_start = jnp.arange(n_blocks, dtype=i32) * MOE_BLK
    block_expert = jnp.minimum(jnp.searchsorted(pend, block_start, side="right"), MOE_EXPERTS - 1).astype(i32)
    n_valid = (pend[-1:] // MOE_BLK).astype(i32)
    blk_id = jnp.arange(n_blocks, dtype=i32)
    valid = blk_id < n_valid[0]
    changed = jnp.concatenate([jnp.ones((1,), bool), block_expert[1:] != block_expert[:-1]])
    first = (changed & valid).astype(i32)
    slot = ((jnp.cumsum(first) - 1) % 2).astype(i32)
    first_pos = jnp.where(first == 1, blk_id, n_blocks)
    next_first = jnp.concatenate([lax.cummin(first_pos, axis=0, reverse=True)[1:],
                                  jnp.full((1,), n_blocks, i32)])
    nxt = jnp.where(next_first < n_blocks, block_expert[jnp.minimum(next_first, n_blocks - 1)], -1).astype(i32)

    def tiles(tm):
        return dest.reshape(n_tok // tm, tm, MOE_TOP_K).transpose(0, 2, 1).reshape(n_tok // tm, 1, MOE_TOP_K * tm)

    xs = _dispatch(pend0, tiles(TM_ROUTE), h_packed, n_tiles=rt)
    ys = _experts(block_expert, n_valid, first, slot, nxt, xs, w_gate, w_up, w_down, layer=layer)
    ct = n_tok // TM_COMBINE
    lat_ct = N_LAT // TM_COMBINE
    return _combine(tiles(TM_COMBINE), x, info, mod_l, ys, n_tiles=ct,
                    mod_sel=lambda i: jnp.where(i < lat_ct, i // (SEQ // TM_COMBINE), BATCH))


def _final_norm_kernel(x_ref, g_ref, o_ref):
    x = x_ref[...]
    ms = jnp.mean(x * x, axis=-1, keepdims=True)
    o_ref[...] = x * lax.rsqrt(ms + NORM_EPS) * g_ref[...]


def _final_norm(x, g):
    tm = ROW_CHUNK
    return pl.pallas_call(
        _final_norm_kernel,
        grid=(N_LAT // tm,),
        in_specs=[pl.BlockSpec((tm, D_MODEL), lambda i: (i, 0)), pl.BlockSpec((1, D_MODEL), lambda i: (0, 0))],
        out_specs=pl.BlockSpec((tm, D_MODEL), lambda i: (i, 0)),
        out_shape=jax.ShapeDtypeStruct((N_LAT, D_MODEL), f32),
        compiler_params=_cparams(("parallel",)),
        name="final_norm",
    )(x, g)


def _rope_tables():
    rows = SEQ // GRID_W
    row = jnp.repeat(jnp.arange(rows, dtype=f32), GRID_W)
    col = jnp.tile(jnp.arange(GRID_W, dtype=f32), rows)
    n_freq = HEAD_DIM // 4
    inv_freq = ROPE_BASE ** (-jnp.arange(n_freq, dtype=f32) / n_freq)
    ang = jnp.concatenate([row[:, None] * inv_freq, col[:, None] * inv_freq], axis=-1)
    cos, sin = jnp.cos(ang), jnp.sin(ang)
    return jnp.concatenate([cos, cos], axis=-1), jnp.concatenate([-sin, sin], axis=-1)


def kernel(x, c, ctx, c_ctx, w_mod, b_mod, norm1_g, norm2_g, w_in, attn_sink, pool_w, pool_scale, ret_decay_fwd, ret_decay_bwd, w_br_attn, w_br_pool, w_br_ret, w_out, w_route_group, b_route_group, w_route_expert, b_route_expert, w_expert_gate, w_expert_up, w_expert_down, final_norm_g):
    assert x.shape == (BATCH, SEQ, D_MODEL) and ctx.shape == (BATCH, CTX_LEN, D_MODEL)
    cos_full, sin_signed = _rope_tables()

    cond8 = jnp.concatenate([c, c_ctx[None, :], jnp.zeros((8 - BATCH - 1, D_MODEL), f32)], axis=0)
    mod = _modulation(cond8, w_mod, b_mod).reshape(DEPTH, 8, N_MOD, D_MODEL)

    lg_fwd = jax.nn.log_sigmoid(ret_decay_fwd.astype(f32))
    lg_bwd = jax.nn.log_sigmoid(ret_decay_bwd.astype(f32))
    pad = ROUTE_LANES - MOE_GROUPS - MOE_EXPERTS
    w_route = jnp.concatenate(
        [w_route_group, w_route_expert, jnp.zeros((DEPTH, D_MODEL, pad), f32)], axis=-1)
    b_route = jnp.concatenate([b_route_group, b_route_expert, jnp.zeros((DEPTH, pad), f32)], axis=-1)

    tok = jnp.concatenate([x.reshape(N_LAT, D_MODEL), ctx.reshape(N_CTX, D_MODEL)], axis=0)
    lat_tiles = N_LAT // TM_LAT
    ctx_block0 = N_LAT // TM_CTX
    sel_lat = lambda i: i // (SEQ // TM_LAT)
    sel_ctx = lambda i: BATCH

    for l in range(DEPTH):
        with_ctx = l < DEPTH - 1
        mod_l = mod[l]
        g1 = norm1_g[l][None, :]
        p_lat = _norm_proj(tok, mod_l, g1, w_in, layer=l, tm=TM_LAT, row_block0=0, n_tiles=lat_tiles,
                           mod_sel=sel_lat)
        p_ctx = _norm_proj(tok, mod_l, g1, w_in, layer=l, tm=TM_CTX, row_block0=ctx_block0, n_tiles=1,
                           mod_sel=sel_ctx)

        ya_lat = _attn_lat(p_lat, p_ctx, attn_sink[l], cos_full, sin_signed)
        yp_lat = _pool(p_lat, pool_w[l], pool_scale[l][None, :], n_seq=BATCH, seq_len=SEQ)
        yr_lat, yr_ctx = _retention(p_lat, p_ctx, lg_fwd[l], lg_bwd[l])

        merged = _merge(ya_lat, yp_lat, yr_lat, p_lat, w_br_attn, w_br_pool, w_br_ret,
                        layer=l, tm=TM_LAT, n_tiles=lat_tiles)
        tok = _out_proj(merged, w_out, tok, mod_l, layer=l, tm=TM_LAT, row_block0=0, n_tiles=lat_tiles,
                        mod_sel=sel_lat)
        if with_ctx:
            ya_ctx = _attn_ctx(p_ctx, attn_sink[l])
            yp_ctx = _pool(p_ctx, pool_w[l], pool_scale[l][None, :], n_seq=BATCH, seq_len=CTX_LEN)
            merged_c = _merge(ya_ctx, yp_ctx, yr_ctx, p_ctx, w_br_attn, w_br_pool, w_br_ret,
                              layer=l, tm=TM_CTX, n_tiles=1)
            tok = _out_proj(merged_c, w_out, tok, mod_l, layer=l, tm=TM_CTX, row_block0=ctx_block0, n_tiles=1,
                            mod_sel=sel_ctx)

        tok = _moe(tok, mod_l, norm2_g[l][None, :], w_route[l], b_route[l][None, :],
                   w_expert_gate, w_expert_up, w_expert_down, layer=l, with_ctx=with_ctx)

    out = _final_norm(tok, final_norm_g[None, :])
    return out.reshape(BATCH, SEQ, D_MODEL)
```

```python
import functools

import jax
import jax.numpy as jnp
from jax import lax
from jax.experimental import pallas as pl
from jax.experimental.pallas import tpu as pltpu

f32 = jnp.float32
bf16 = jnp.bfloat16
i32 = jnp.int32
u32 = jnp.uint32

D_MODEL = 2048
BATCH = 2
SEQ = 4096
DEPTH = 4
GRID_W = 64
CTX_LEN = 256
NORM_EPS = 1e-6
N_MOD = 6
HEAD_DIM = 128
ATTN_HEADS = 8
ATTN_KV_HEADS = 2
ATTN_GROUP = ATTN_HEADS // ATTN_KV_HEADS
ATTN_BLOCK = 128
ROPE_BASE = 10000.0
POOL_GROUPS = 4
POOL_GROUP_DIM = 128
POOL_WIDTH = POOL_GROUPS * POOL_GROUP_DIM
POOL_SPANS = (2, 4, 8, 16)
RET_HEADS = 4
RET_QK_DIM = 64
RET_V_DIM = 128
N_BRANCHES = 3
ATTN_Q_WIDTH = ATTN_HEADS * HEAD_DIM
ATTN_KV_WIDTH = ATTN_KV_HEADS * HEAD_DIM
RET_QK_WIDTH = RET_HEADS * RET_QK_DIM
RET_V_WIDTH = RET_HEADS * RET_V_DIM
IN_SIZES = (ATTN_Q_WIDTH, ATTN_KV_WIDTH, ATTN_KV_WIDTH, POOL_WIDTH,
            RET_QK_WIDTH, RET_QK_WIDTH, RET_V_WIDTH, RET_V_WIDTH, N_BRANCHES * D_MODEL)
IN_WIDTH = sum(IN_SIZES)
MOE_GROUPS = 4
MOE_EXPERTS_PER_GROUP = 8
MOE_EXPERTS = MOE_GROUPS * MOE_EXPERTS_PER_GROUP
MOE_TOP_K = 2
MOE_D_FF = 512

COL_QA = 0
COL_KA = COL_QA + ATTN_Q_WIDTH
COL_VA = COL_KA + ATTN_KV_WIDTH
COL_U = COL_VA + ATTN_KV_WIDTH
COL_QR = COL_U + POOL_WIDTH
COL_KR = COL_QR + RET_QK_WIDTH
COL_VR = COL_KR + RET_QK_WIDTH
COL_GR = COL_VR + RET_V_WIDTH
COL_GATES = COL_GR + RET_V_WIDTH

V7X_LANES = 128
V7X_VMEM_LIMIT_BYTES = 56 * 1024 * 1024

N_LAT = BATCH * SEQ
N_CTX = BATCH * CTX_LEN
N_TOK = N_LAT + N_CTX
TM_PROJ = 2048
TM_MERGE = 1024
TM_OUT = 2048
TM_CTX = N_CTX
TM_NORM = 512
TN_PROJ = 512
TN_MERGE = 512
TN_OUT = 512
TN_MOD = 2048
ROW_CHUNK = 128
RET_CHUNK = 256
TM_ROUTE = 512
TM_COMBINE = 256
MOE_BLK = 256
MOE_ROWS = (N_TOK * MOE_TOP_K + MOE_EXPERTS * (MOE_BLK - 1)) // MOE_BLK * MOE_BLK
PACKED = D_MODEL // 2
XS_SUB = PACKED // V7X_LANES
YS_SUB = D_MODEL // V7X_LANES

assert CTX_LEN == RET_CHUNK and SEQ % RET_CHUNK == 0
assert IN_WIDTH % TN_PROJ == 0 and COL_GATES % TN_MERGE == 0


def _cparams(sem, vmem=V7X_VMEM_LIMIT_BYTES):
    return pltpu.CompilerParams(dimension_semantics=sem, vmem_limit_bytes=vmem)


def _dot(a, b):
    return jnp.dot(a, b, preferred_element_type=f32)


def _dot_nt(a, b):
    return lax.dot_general(a, b, (((1,), (1,)), ((), ())), preferred_element_type=f32)


def _dot_tn(a, b):
    return lax.dot_general(a, b, (((0,), (0,)), ((), ())), preferred_element_type=f32)


def _mod_kernel(cond_ref, w_ref, b_ref, o_ref):
    c = cond_ref[...]
    s = (c * jax.nn.sigmoid(c)).astype(bf16)
    o_ref[0] = _dot(s, w_ref[0].astype(bf16)) + b_ref[0]


def _modulation(cond8, w_mod, b_mod):
    n = N_MOD * D_MODEL
    return pl.pallas_call(
        _mod_kernel,
        grid=(DEPTH, n // TN_MOD),
        in_specs=[
            pl.BlockSpec((8, D_MODEL), lambda l, j: (0, 0)),
            pl.BlockSpec((1, D_MODEL, TN_MOD), lambda l, j: (l, 0, j)),
            pl.BlockSpec((1, 1, TN_MOD), lambda l, j: (l, 0, j)),
        ],
        out_specs=pl.BlockSpec((1, 8, TN_MOD), lambda l, j: (l, 0, j)),
        out_shape=jax.ShapeDtypeStruct((DEPTH, 8, n), f32),
        compiler_params=_cparams(("parallel", "parallel")),
        name="modulation",
    )(cond8, w_mod, b_mod.reshape(DEPTH, 1, n))


def _norm_mod_rows(x, g, shift, scale):
    ms = jnp.mean(x * x, axis=-1, keepdims=True)
    y = x * lax.rsqrt(ms + NORM_EPS) * g
    return y * (1.0 + scale) + shift


def _norm_kernel(x_ref, mod_ref, g_ref, o_ref, *, tm):
    g = g_ref[...]
    shift = mod_ref[0, 0:1, :]
    scale = mod_ref[0, 1:2, :]

    def body(r, carry):
        rows = pl.ds(pl.multiple_of(r * ROW_CHUNK, ROW_CHUNK), ROW_CHUNK)
        o_ref[rows, :] = _norm_mod_rows(x_ref[rows, :], g, shift, scale).astype(bf16)
        return carry

    lax.fori_loop(0, tm // ROW_CHUNK, body, 0)


def _norm(x, mod_l, g1, *, n_tiles, mod_sel):
    tm = TM_NORM
    return pl.pallas_call(
        functools.partial(_norm_kernel, tm=tm),
        grid=(n_tiles,),
        in_specs=[
            pl.BlockSpec((tm, D_MODEL), lambda i: (i, 0)),
            pl.BlockSpec((1, N_MOD, D_MODEL), lambda i: (mod_sel(i), 0, 0)),
            pl.BlockSpec((1, D_MODEL), lambda i: (0, 0)),
        ],
        out_specs=pl.BlockSpec((tm, D_MODEL), lambda i: (i, 0)),
        out_shape=jax.ShapeDtypeStruct((n_tiles * tm, D_MODEL), bf16),
        compiler_params=_cparams(("parallel",)),
        name="norm1",
    )(x, mod_l, g1)


def _proj_kernel(h_ref, w_ref, o_ref):
    o_ref[...] = _dot(h_ref[...], w_ref[...].astype(bf16)).astype(o_ref.dtype)


def _proj(h, w_in, *, layer, tm, row_block0, n_tiles):
    return pl.pallas_call(
        _proj_kernel,
        grid=(n_tiles, IN_WIDTH // TN_PROJ),
        in_specs=[
            pl.BlockSpec((tm, D_MODEL), lambda i, j: (row_block0 + i, 0)),
            pl.BlockSpec((None, D_MODEL, TN_PROJ), lambda i, j: (layer, 0, j)),
        ],
        out_specs=pl.BlockSpec((tm, TN_PROJ), lambda i, j: (i, j)),
        out_shape=jax.ShapeDtypeStruct((n_tiles * tm, IN_WIDTH), bf16),
        compiler_params=_cparams(("parallel", "arbitrary")),
        name="in_proj",
    )(h, w_in)


def _rope(t, cos_full, sin_signed):
    return t * cos_full + pltpu.roll(t, HEAD_DIM // 2, 1) * sin_signed


def _sink_column(sink_ref, g, rows_per_head):
    n = ATTN_GROUP * rows_per_head
    row = lax.broadcasted_iota(i32, (n, 1), 0)
    col = jnp.full((n, 1), sink_ref[g * ATTN_GROUP + ATTN_GROUP - 1], f32)
    for j in range(ATTN_GROUP - 2, -1, -1):
        col = jnp.where(row < (j + 1) * rows_per_head, sink_ref[g * ATTN_GROUP + j], col)
    return col


def _softmax_pv(scores, values, sink_col):
    def lane_chunks(arrs):
        return [a[:, c * V7X_LANES:(c + 1) * V7X_LANES] for a in arrs for c in range(a.shape[1] // V7X_LANES)]

    folded = functools.reduce(jnp.maximum, lane_chunks(scores))
    m = jnp.maximum(sink_col, jnp.max(folded, axis=-1, keepdims=True))
    es = [jnp.exp(s - m) for s in scores]
    total = functools.reduce(lambda a, b: a + b, lane_chunks(es))
    denom = jnp.exp(sink_col - m) + jnp.sum(total, axis=-1, keepdims=True)
    inv = 1.0 / denom
    out = None
    for e, v in zip(es, values):
        o = _dot((e * inv).astype(bf16), v)
        out = o if out is None else out + o
    return out


def _attn_lat_kernel(sink_ref, q_ref, kp_ref, kc_ref, kn_ref, vp_ref, vc_ref, vn_ref, kx_ref, vx_ref,
                     cq_ref, sq_ref, cp_ref, sp_ref, cn_ref, sn_ref, o_ref, *, n_blocks):
    n = pl.program_id(1)
    blk = ATTN_BLOCK
    scale = HEAD_DIM ** -0.5
    cq, sq = cq_ref[...], sq_ref[...]
    cp, sp = cp_ref[...], sp_ref[...]
    cn, sn = cn_ref[...], sn_ref[...]
    rows = ATTN_GROUP * blk
    qi = lax.broadcasted_iota(i32, (rows, blk), 0) % blk
    kj = lax.broadcasted_iota(i32, (rows, blk), 1)
    neg = jnp.float32(-jnp.inf)
    keep_prev = (kj >= qi) & (n > 0)
    keep_next = (kj <= qi) & (n < n_blocks - 1)

    for g in range(ATTN_KV_HEADS):
        kv = slice(g * HEAD_DIM, (g + 1) * HEAD_DIM)
        qs = jnp.concatenate(
            [(_rope(q_ref[:, (g * ATTN_GROUP + j) * HEAD_DIM:(g * ATTN_GROUP + j + 1) * HEAD_DIM].astype(f32),
                    cq, sq) * scale).astype(bf16) for j in range(ATTN_GROUP)],
            axis=0)
        k_prev = _rope(kp_ref[:, kv].astype(f32), cp, sp).astype(bf16)
        k_cur = _rope(kc_ref[:, kv].astype(f32), cq, sq).astype(bf16)
        k_next = _rope(kn_ref[:, kv].astype(f32), cn, sn).astype(bf16)
        k_ctx = kx_ref[:, kv]
        s_prev = jnp.where(keep_prev, _dot_nt(qs, k_prev), neg)
        s_cur = _dot_nt(qs, k_cur)
        s_next = jnp.where(keep_next, _dot_nt(qs, k_next), neg)
        s_ctx = _dot_nt(qs, k_ctx)
        out = _softmax_pv(
            [s_prev, s_cur, s_next, s_ctx],
            [vp_ref[:, kv], vc_ref[:, kv], vn_ref[:, kv], vx_ref[:, kv]],
            _sink_column(sink_ref, g, blk))
        for j in range(ATTN_GROUP):
            h = g * ATTN_GROUP + j
            o_ref[:, h * HEAD_DIM:(h + 1) * HEAD_DIM] = out[j * blk:(j + 1) * blk, :].astype(o_ref.dtype)


def _attn_lat(p_lat, p_ctx, sink, cos_full, sin_signed):
    nb = SEQ // ATTN_BLOCK
    blk = ATTN_BLOCK
    kcol = COL_KA // ATTN_KV_WIDTH
    vcol = COL_VA // ATTN_KV_WIDTH

    def kv_spec(col, shift):
        return pl.BlockSpec((blk, ATTN_KV_WIDTH), lambda b, n: (b * nb + jnp.clip(n + shift, 0, nb - 1), col))

    def tab_spec(shift):
        return pl.BlockSpec((blk, HEAD_DIM), lambda b, n: (jnp.clip(n + shift, 0, nb - 1), 0))

    return pl.pallas_call(
        functools.partial(_attn_lat_kernel, n_blocks=nb),
        grid=(BATCH, nb),
        in_specs=[
            pl.BlockSpec(memory_space=pltpu.SMEM),
            pl.BlockSpec((blk, ATTN_Q_WIDTH), lambda b, n: (b * nb + n, 0)),
            kv_spec(kcol, -1), kv_spec(kcol, 0), kv_spec(kcol, 1),
            kv_spec(vcol, -1), kv_spec(vcol, 0), kv_spec(vcol, 1),
            pl.BlockSpec((CTX_LEN, ATTN_KV_WIDTH), lambda b, n: (b, kcol)),
            pl.BlockSpec((CTX_LEN, ATTN_KV_WIDTH), lambda b, n: (b, vcol)),
            tab_spec(0), tab_spec(0), tab_spec(-1), tab_spec(-1), tab_spec(1), tab_spec(1),
        ],
        out_specs=pl.BlockSpec((blk, ATTN_Q_WIDTH), lambda b, n: (b * nb + n, 0)),
        out_shape=jax.ShapeDtypeStruct((N_LAT, ATTN_Q_WIDTH), bf16),
        compiler_params=_cparams(("parallel", "arbitrary")),
        name="attn_lat",
    )(sink, p_lat, p_lat, p_lat, p_lat, p_lat, p_lat, p_lat, p_ctx, p_ctx,
      cos_full, sin_signed, cos_full, sin_signed, cos_full, sin_signed)


def _attn_ctx_kernel(sink_ref, q_ref, kx_ref, vx_ref, o_ref):
    g = pl.program_id(1)
    scale = HEAD_DIM ** -0.5
    q = q_ref[...].astype(f32)
    qs = jnp.concatenate(
        [(q[:, j * HEAD_DIM:(j + 1) * HEAD_DIM] * scale).astype(bf16) for j in range(ATTN_GROUP)], axis=0)
    s_ctx = _dot_nt(qs, kx_ref[...])
    out = _softmax_pv([s_ctx], [vx_ref[...]], _sink_column(sink_ref, g, CTX_LEN))
    for j in range(ATTN_GROUP):
        o_ref[:, j * HEAD_DIM:(j + 1) * HEAD_DIM] = out[j * CTX_LEN:(j + 1) * CTX_LEN, :].astype(o_ref.dtype)


def _attn_ctx(p_ctx, sink):
    gw = ATTN_GROUP * HEAD_DIM
    kcol = COL_KA // HEAD_DIM
    vcol = COL_VA // HEAD_DIM
    return pl.pallas_call(
        _attn_ctx_kernel,
        grid=(BATCH, ATTN_KV_HEADS),
        in_specs=[
            pl.BlockSpec(memory_space=pltpu.SMEM),
            pl.BlockSpec((CTX_LEN, gw), lambda b, g: (b, g)),
            pl.BlockSpec((CTX_LEN, HEAD_DIM), lambda b, g: (b, kcol + g)),
            pl.BlockSpec((CTX_LEN, HEAD_DIM), lambda b, g: (b, vcol + g)),
        ],
        out_specs=pl.BlockSpec((CTX_LEN, gw), lambda b, g: (b, g)),
        out_shape=jax.ShapeDtypeStruct((N_CTX, ATTN_Q_WIDTH), bf16),
        compiler_params=_cparams(("parallel", "parallel")),
        name="attn_ctx",
    )(sink, p_ctx, p_ctx, p_ctx)


POOL_HALO = 8
POOL_ROWS = 256


def _pool_kernel(u_ref, w_ref, s_ref, o_ref, pad_ref, *, seq_len):
    zeros = jnp.zeros((POOL_HALO, POOL_GROUP_DIM), f32)
    for gi in range(POOL_GROUPS):
        radius = POOL_SPANS[gi] // 2
        cols = slice(gi * POOL_GROUP_DIM, (gi + 1) * POOL_GROUP_DIM)
        pad_ref[0:POOL_HALO, :] = zeros
        pad_ref[POOL_HALO + seq_len:POOL_HALO + seq_len + POOL_HALO, :] = zeros
        pad_ref[POOL_HALO:POOL_HALO + seq_len, :] = u_ref[:, cols].astype(f32)
        w = w_ref[gi].astype(bf16)
        sc = s_ref[:, cols]
        for c in range(seq_len // POOL_ROWS):
            base = c * POOL_ROWS
            acc = pad_ref[POOL_HALO + base:POOL_HALO + base + POOL_ROWS, :]
            center = acc
            for d in range(1, radius + 1):
                acc = acc + pad_ref[POOL_HALO + base - d:POOL_HALO + base - d + POOL_ROWS, :]
                acc = acc + pad_ref[POOL_HALO + base + d:POOL_HALO + base + d + POOL_ROWS, :]
            t = lax.broadcasted_iota(i32, (POOL_ROWS, 1), 0) + base
            lo = jnp.maximum(t - radius, 0)
            hi = jnp.minimum(t + radius + 1, seq_len)
            count = (hi - lo).astype(f32)
            pooled = (acc / count - center).astype(bf16)
            o_ref[base:base + POOL_ROWS, cols] = (_dot(pooled, w) * sc).astype(o_ref.dtype)


def _pool(p, pool_w, pool_scale, *, n_seq, seq_len):
    ucol = COL_U // POOL_WIDTH
    return pl.pallas_call(
        functools.partial(_pool_kernel, seq_len=seq_len),
        grid=(n_seq,),
        in_specs=[
            pl.BlockSpec((seq_len, POOL_WIDTH), lambda s: (s, ucol)),
            pl.BlockSpec((POOL_GROUPS, POOL_GROUP_DIM, POOL_GROUP_DIM), lambda s: (0, 0, 0)),
            pl.BlockSpec((1, POOL_WIDTH), lambda s: (0, 0)),
        ],
        out_specs=pl.BlockSpec((seq_len, POOL_WIDTH), lambda s: (s, 0)),
        out_shape=jax.ShapeDtypeStruct((n_seq * seq_len, POOL_WIDTH), bf16),
        scratch_shapes=[pltpu.VMEM((seq_len + 2 * POOL_HALO, POOL_GROUP_DIM), f32)],
        compiler_params=_cparams(("parallel",)),
        name="pool",
    )(p, pool_w, pool_scale)


def _ret_tables(lg_ref, intra_ref, qdec_ref, kdec_ref, *, reverse):
    c = RET_CHUNK
    i = lax.broadcasted_iota(i32, (c, c), 0).astype(f32)
    j = lax.broadcasted_iota(i32, (c, c), 1).astype(f32)
    pos = lax.broadcasted_iota(i32, (c, V7X_LANES), 0).astype(f32)
    for h in range(RET_HEADS):
        lg = lg_ref[h]
        if reverse:
            diff = j - i
            qd = c - pos
            kd = pos
        else:
            diff = i - j
            qd = pos + 1.0
            kd = c - 1.0 - pos
        intra_ref[h] = jnp.where(diff >= 0, jnp.exp(lg * jnp.maximum(diff, 0.0)), 0.0)
        qdec_ref[h] = jnp.exp(lg * qd)
        kdec_ref[h] = jnp.exp(lg * kd)


def _ret_chunk(q_ref, k_ref, v_ref, lg_ref, intra_ref, qdec_ref, kdec_ref, state_ref):
    outs = []
    k_scale = RET_QK_DIM ** -0.5
    for h in range(RET_HEADS):
        q = q_ref[:, h * RET_QK_DIM:(h + 1) * RET_QK_DIM].astype(bf16)
        k = k_ref[:, h * RET_QK_DIM:(h + 1) * RET_QK_DIM].astype(f32) * k_scale
        v = v_ref[:, h * RET_V_DIM:(h + 1) * RET_V_DIM].astype(bf16)
        scores = _dot_nt(q, k.astype(bf16)) * intra_ref[h]
        state = state_ref[h]
        out = _dot(scores.astype(bf16), v) + _dot(q, state.astype(bf16)) * qdec_ref[h]
        kd = (k * kdec_ref[h][:, :RET_QK_DIM]).astype(bf16)
        chunk_decay = jnp.exp(jnp.full((1, RET_V_DIM), lg_ref[h] * RET_CHUNK, f32))
        state_ref[h] = state * chunk_decay + _dot_tn(kd, v)
        outs.append(out)
    return outs


def _ret_bwd_kernel(lg_ref, ql_ref, kl_ref, vl_ref, qc_ref, kc_ref, vc_ref, ol_ref, oc_ref,
                    intra_ref, qdec_ref, kdec_ref, state_ref):
    t = pl.program_id(1)

    @pl.when((pl.program_id(0) == 0) & (t == 0))
    def _():
        _ret_tables(lg_ref, intra_ref, qdec_ref, kdec_ref, reverse=True)

    @pl.when(t == 0)
    def _():
        state_ref[...] = jnp.zeros_like(state_ref)
        outs = _ret_chunk(qc_ref, kc_ref, vc_ref, lg_ref, intra_ref, qdec_ref, kdec_ref, state_ref)
        for h in range(RET_HEADS):
            oc_ref[:, h * RET_V_DIM:(h + 1) * RET_V_DIM] = outs[h]

    @pl.when(t > 0)
    def _():
        outs = _ret_chunk(ql_ref, kl_ref, vl_ref, lg_ref, intra_ref, qdec_ref, kdec_ref, state_ref)
        for h in range(RET_HEADS):
            ol_ref[:, h * RET_V_DIM:(h + 1) * RET_V_DIM] = outs[h]


def _ret_finish(outs, yb_ref, g_ref, o_ref):
    for h in range(RET_HEADS):
        cols = slice(h * RET_V_DIM, (h + 1) * RET_V_DIM)
        y = outs[h] + yb_ref[:, cols]
        mu = jnp.mean(y, axis=-1, keepdims=True)
        yc = y - mu
        var = jnp.mean(yc * yc, axis=-1, keepdims=True)
        yn = yc * lax.rsqrt(var + NORM_EPS)
        g = g_ref[:, cols].astype(f32)
        o_ref[:, cols] = (g * jax.nn.sigmoid(g) * yn).astype(o_ref.dtype)


def _ret_fwd_kernel(lg_ref, ql_ref, kl_ref, vl_ref, gl_ref, ybl_ref, qc_ref, kc_ref, vc_ref, gc_ref, ybc_ref,
                    ol_ref, oc_ref, intra_ref, qdec_ref, kdec_ref, state_ref):
    t = pl.program_id(1)

    @pl.when((pl.program_id(0) == 0) & (t == 0))
    def _():
        _ret_tables(lg_ref, intra_ref, qdec_ref, kdec_ref, reverse=False)

    @pl.when(t == 0)
    def _():
        state_ref[...] = jnp.zeros_like(state_ref)
        outs = _ret_chunk(qc_ref, kc_ref, vc_ref, lg_ref, intra_ref, qdec_ref, kdec_ref, state_ref)
        _ret_finish(outs, ybc_ref, gc_ref, oc_ref)

    @pl.when(t > 0)
    def _():
        outs = _ret_chunk(ql_ref, kl_ref, vl_ref, lg_ref, intra_ref, qdec_ref, kdec_ref, state_ref)
        _ret_finish(outs, ybl_ref, gl_ref, ol_ref)


def _ret_scratch():
    c = RET_CHUNK
    return [
        pltpu.VMEM((RET_HEADS, c, c), f32),
        pltpu.VMEM((RET_HEADS, c, V7X_LANES), f32),
        pltpu.VMEM((RET_HEADS, c, V7X_LANES), f32),
        pltpu.VMEM((RET_HEADS, RET_QK_DIM, RET_V_DIM), f32),
    ]


def _retention(p_lat, p_ctx, lg_fwd, lg_bwd):
    c = RET_CHUNK
    nc = SEQ // c
    qcol, kcol = COL_QR // RET_QK_WIDTH, COL_KR // RET_QK_WIDTH
    vcol, gcol = COL_VR // RET_V_WIDTH, COL_GR // RET_V_WIDTH

    def lat_row(reverse):
        def f(b, t):
            s = jnp.maximum(t, 1)
            return b * nc + ((nc - s) if reverse else (s - 1))
        return f

    def specs(reverse, with_gate):
        row = lat_row(reverse)
        lat = [pl.BlockSpec((c, RET_QK_WIDTH), lambda b, t: (row(b, t), qcol)),
               pl.BlockSpec((c, RET_QK_WIDTH), lambda b, t: (row(b, t), kcol)),
               pl.BlockSpec((c, RET_V_WIDTH), lambda b, t: (row(b, t), vcol))]
        ctx = [pl.BlockSpec((c, RET_QK_WIDTH), lambda b, t: (b, qcol)),
               pl.BlockSpec((c, RET_QK_WIDTH), lambda b, t: (b, kcol)),
               pl.BlockSpec((c, RET_V_WIDTH), lambda b, t: (b, vcol))]
        if with_gate:
            lat += [pl.BlockSpec((c, RET_V_WIDTH), lambda b, t: (row(b, t), gcol)),
                    pl.BlockSpec((c, RET_V_WIDTH), lambda b, t: (row(b, t), 0))]
            ctx += [pl.BlockSpec((c, RET_V_WIDTH), lambda b, t: (b, gcol)),
                    pl.BlockSpec((c, RET_V_WIDTH), lambda b, t: (b, 0))]
        outs = [pl.BlockSpec((c, RET_V_WIDTH), lambda b, t: (row(b, t), 0)),
                pl.BlockSpec((c, RET_V_WIDTH), lambda b, t: (b, 0))]
        return [pl.BlockSpec(memory_space=pltpu.SMEM)] + lat + ctx, outs

    in_specs, out_specs = specs(True, False)
    yb_lat, yb_ctx = pl.pallas_call(
        _ret_bwd_kernel,
        grid=(BATCH, nc + 1),
        in_specs=in_specs,
        out_specs=out_specs,
        out_shape=[jax.ShapeDtypeStruct((N_LAT, RET_V_WIDTH), f32),
                   jax.ShapeDtypeStruct((N_CTX, RET_V_WIDTH), f32)],
        scratch_shapes=_ret_scratch(),
        compiler_params=_cparams(("arbitrary", "arbitrary")),
        name="retention_bwd",
    )(lg_bwd, p_lat, p_lat, p_lat, p_ctx, p_ctx, p_ctx)

    in_specs, out_specs = specs(False, True)
    return pl.pallas_call(
        _ret_fwd_kernel,
        grid=(BATCH, nc + 1),
        in_specs=in_specs,
        out_specs=out_specs,
        out_shape=[jax.ShapeDtypeStruct((N_LAT, RET_V_WIDTH), bf16),
                   jax.ShapeDtypeStruct((N_CTX, RET_V_WIDTH), bf16)],
        scratch_shapes=_ret_scratch(),
        compiler_params=_cparams(("arbitrary", "arbitrary")),
        name="retention_fwd",
    )(lg_fwd, p_lat, p_lat, p_lat, p_lat, yb_lat, p_ctx, p_ctx, p_ctx, p_ctx, yb_ctx)


def _merge_kernel(ya_ref, yp_ref, yr_ref, ga_ref, gp_ref, gr_ref, wa_ref, wp_ref, wr_ref, o_ref):
    acc = jax.nn.sigmoid(ga_ref[...].astype(f32)) * _dot(ya_ref[...], wa_ref[...].astype(bf16))
    acc = acc + jax.nn.sigmoid(gp_ref[...].astype(f32)) * _dot(yp_ref[...], wp_ref[...].astype(bf16))
    acc = acc + jax.nn.sigmoid(gr_ref[...].astype(f32)) * _dot(yr_ref[...], wr_ref[...].astype(bf16))
    o_ref[...] = acc.astype(o_ref.dtype)


def _merge(ya, yp, yr, p, w_a, w_p, w_r, *, layer, tm, n_tiles):
    gate0 = COL_GATES // TN_MERGE
    per = D_MODEL // TN_MERGE

    def gate_spec(br):
        return pl.BlockSpec((tm, TN_MERGE), lambda i, j: (i, gate0 + br * per + j))

    return pl.pallas_call(
        _merge_kernel,
        grid=(n_tiles, per),
        in_specs=[
            pl.BlockSpec((tm, ATTN_Q_WIDTH), lambda i, j: (i, 0)),
            pl.BlockSpec((tm, POOL_WIDTH), lambda i, j: (i, 0)),
            pl.BlockSpec((tm, RET_V_WIDTH), lambda i, j: (i, 0)),
            gate_spec(0), gate_spec(1), gate_spec(2),
            pl.BlockSpec((None, ATTN_Q_WIDTH, TN_MERGE), lambda i, j: (layer, 0, j)),
            pl.BlockSpec((None, POOL_WIDTH, TN_MERGE), lambda i, j: (layer, 0, j)),
            pl.BlockSpec((None, RET_V_WIDTH, TN_MERGE), lambda i, j: (layer, 0, j)),
        ],
        out_specs=pl.BlockSpec((tm, TN_MERGE), lambda i, j: (i, j)),
        out_shape=jax.ShapeDtypeStruct((n_tiles * tm, D_MODEL), bf16),
        compiler_params=_cparams(("parallel", "parallel")),
        name="merge",
    )(ya, yp, yr, p, p, p, w_a, w_p, w_r)


def _out_proj_kernel(m_ref, w_ref, x_ref, mod_ref, o_ref):
    y = _dot(m_ref[...], w_ref[...].astype(bf16))
    o_ref[...] = x_ref[...] + mod_ref[0, 2:3, :] * y


def _out_proj(merged, w_out, x, mod_l, *, layer, tm, row_block0, n_tiles, mod_sel):
    return pl.pallas_call(
        _out_proj_kernel,
        grid=(n_tiles, D_MODEL // TN_OUT),
        in_specs=[
            pl.BlockSpec((tm, D_MODEL), lambda i, j: (i, 0)),
            pl.BlockSpec((None, D_MODEL, TN_OUT), lambda i, j: (layer, 0, j)),
            pl.BlockSpec((tm, TN_OUT), lambda i, j: (row_block0 + i, j)),
            pl.BlockSpec((1, N_MOD, TN_OUT), lambda i, j: (mod_sel(i), 0, j)),
        ],
        out_specs=pl.BlockSpec((tm, TN_OUT), lambda i, j: (row_block0 + i, j)),
        out_shape=jax.ShapeDtypeStruct(x.shape, f32),
        input_output_aliases={2: 0},
        compiler_params=_cparams(("parallel", "parallel")),
        name="out_proj",
    )(merged, w_out, x, mod_l)


ROUTE_LANES = V7X_LANES
LANE_E1, LANE_E2, LANE_W1, LANE_W2, LANE_R1, LANE_R2 = 0, 1, 2, 3, 4, 5
HI16 = 0xFFFF0000


def _pack_bf16_pairs(v):
    n = v.shape[1] // 2
    bits = lax.bitcast_convert_type(v.astype(bf16).astype(f32), u32)
    return (bits[:, :n] >> 16) | (bits[:, n:] & jnp.uint32(HI16))


def _unpack_bf16_pairs(words):
    return (lax.bitcast_convert_type(words << 16, f32),
            lax.bitcast_convert_type(words & jnp.uint32(HI16), f32))


def _route_kernel(x_ref, mod_ref, g_ref, w_ref, b_ref, h_ref, info_ref, cnt_ref, hf_ref, tri_ref, run_ref):
    tm = TM_ROUTE
    step = pl.program_id(0)

    @pl.when(step == 0)
    def _():
        run_ref[...] = jnp.zeros_like(run_ref)
        r = lax.broadcasted_iota(i32, (tm, tm), 0)
        c = lax.broadcasted_iota(i32, (tm, tm), 1)
        tri_ref[...] = jnp.where(c < r, 1.0, 0.0).astype(bf16)

    g = g_ref[...]
    shift = mod_ref[0, 3:4, :]
    scale = mod_ref[0, 4:5, :]

    def body(r, carry):
        rows = pl.ds(pl.multiple_of(r * ROW_CHUNK, ROW_CHUNK), ROW_CHUNK)
        hf_ref[rows, :] = _norm_mod_rows(x_ref[rows, :], g, shift, scale).astype(bf16)
        return carry

    lax.fori_loop(0, tm // ROW_CHUNK, body, 0)

    for c in range(tm // ROW_CHUNK):
        words = _pack_bf16_pairs(hf_ref[c * ROW_CHUNK:(c + 1) * ROW_CHUNK, :])
        for s in range(XS_SUB):
            h_ref[pl.ds(c * ROW_CHUNK * XS_SUB + s, ROW_CHUNK, stride=XS_SUB), :] = (
                words[:, s * V7X_LANES:(s + 1) * V7X_LANES])

    logits = _dot(hf_ref[...], w_ref[...].astype(bf16)) + b_ref[...]
    lane = lax.broadcasted_iota(i32, (tm, ROUTE_LANES), 1)
    neg = jnp.float32(-jnp.inf)
    lane_f = lane.astype(f32)

    def first_argmax(vals):
        top = jnp.max(vals, axis=-1, keepdims=True)
        idx = jnp.min(jnp.where(vals == top, lane_f, float(ROUTE_LANES)), axis=-1, keepdims=True)
        return top, idx.astype(i32)

    g_logits = jnp.where(lane < MOE_GROUPS, logits, neg)
    g_top, grp = first_argmax(g_logits)
    p_grp = 1.0 / jnp.sum(jnp.exp(g_logits - g_top), axis=-1, keepdims=True)

    e_lane = lane - MOE_GROUPS
    in_grp = (e_lane >= 0) & (e_lane < MOE_EXPERTS) & (lax.shift_right_arithmetic(e_lane, 3) == grp)
    e_logits = jnp.where(in_grp, logits, neg)
    top1, idx1 = first_argmax(e_logits)
    e_logits2 = jnp.where(lane == idx1, neg, e_logits)
    top2, idx2 = first_argmax(e_logits2)
    a = jnp.exp(top2 - top1)
    w1 = (1.0 / (1.0 + a)) * p_grp
    w2 = (a / (1.0 + a)) * p_grp
    e1 = idx1 - MOE_GROUPS
    e2 = idx2 - MOE_GROUPS

    hot1 = lane == e1
    hot2 = lane == e2
    o1 = jnp.where(hot1, 1.0, 0.0)
    o2 = jnp.where(hot2, 1.0, 0.0)
    tri = tri_ref[...]
    before1 = _dot(tri, o1.astype(bf16))
    before2 = _dot(tri, o2.astype(bf16))
    c1 = jnp.sum(o1, axis=0, keepdims=True)
    c2 = jnp.sum(o2, axis=0, keepdims=True)
    run = run_ref[...]
    rank1 = jnp.sum(jnp.where(hot1, before1 + run, 0.0), axis=-1, keepdims=True)
    rank2 = jnp.sum(jnp.where(hot2, before2 + run + c1, 0.0), axis=-1, keepdims=True)
    run = run + c1 + c2
    run_ref[...] = run
    cnt_ref[...] = jnp.broadcast_to(run, cnt_ref.shape)

    info = jnp.where(lane == LANE_E1, e1.astype(f32), 0.0)
    info = jnp.where(lane == LANE_E2, e2.astype(f32), info)
    info = jnp.where(lane == LANE_W1, w1, info)
    info = jnp.where(lane == LANE_W2, w2, info)
    info = jnp.where(lane == LANE_R1, rank1, info)
    info = jnp.where(lane == LANE_R2, rank2, info)
    info_ref[...] = info


def _route(x, mod_l, g2, w_route, b_route, *, n_tiles, mod_sel):
    tm = TM_ROUTE
    return pl.pallas_call(
        _route_kernel,
        grid=(n_tiles,),
        in_specs=[
            pl.BlockSpec((tm, D_MODEL), lambda i: (i, 0)),
            pl.BlockSpec((1, N_MOD, D_MODEL), lambda i: (mod_sel(i), 0, 0)),
            pl.BlockSpec((1, D_MODEL), lambda i: (0, 0)),
            pl.BlockSpec((D_MODEL, ROUTE_LANES), lambda i: (0, 0)),
            pl.BlockSpec((1, ROUTE_LANES), lambda i: (0, 0)),
        ],
        out_specs=[
            pl.BlockSpec((tm * XS_SUB, V7X_LANES), lambda i: (i, 0)),
            pl.BlockSpec((tm, ROUTE_LANES), lambda i: (i, 0)),
            pl.BlockSpec((8, ROUTE_LANES), lambda i: (0, 0)),
        ],
        out_shape=[
            jax.ShapeDtypeStruct((n_tiles * tm * XS_SUB, V7X_LANES), u32),
            jax.ShapeDtypeStruct((n_tiles * tm, ROUTE_LANES), f32),
            jax.ShapeDtypeStruct((8, ROUTE_LANES), f32),
        ],
        scratch_shapes=[
            pltpu.VMEM((tm, D_MODEL), bf16),
            pltpu.VMEM((tm, tm), bf16),
            pltpu.VMEM((1, ROUTE_LANES), f32),
        ],
        compiler_params=_cparams(("arbitrary",)),
        name="moe_route",
    )(x, mod_l, g2, w_route, b_route)


def _dispatch_kernel(pend_ref, dest_ref, h_ref, xs_ref, zero_ref, sem):
    tm = TM_ROUTE
    step = pl.program_id(0)

    blk_rows = MOE_BLK * XS_SUB

    def zero_copy(e):
        start = pl.multiple_of((pend_ref[e + 1] - MOE_BLK) * XS_SUB, blk_rows)
        return pltpu.make_async_copy(zero_ref, xs_ref.at[pl.ds(start, blk_rows)], sem)

    def tail_copy(blk):
        start = pl.multiple_of(blk * blk_rows, blk_rows)
        return pltpu.make_async_copy(zero_ref, xs_ref.at[pl.ds(start, blk_rows)], sem)

    @pl.when(step == 0)
    def _():
        zero_ref[...] = jnp.zeros_like(zero_ref)
        first_unused = pend_ref[MOE_EXPERTS] // MOE_BLK
        for phase in range(2):
            def body(e, carry):
                @pl.when(pend_ref[e + 1] > pend_ref[e])
                def _():
                    if phase == 0:
                        zero_copy(e).start()
                    else:
                        zero_copy(e).wait()
                return carry
            lax.fori_loop(0, MOE_EXPERTS, body, 0)

            def tail(blk, carry):
                if phase == 0:
                    tail_copy(blk).start()
                else:
                    tail_copy(blk).wait()
                return carry
            lax.fori_loop(first_unused, MOE_ROWS // MOE_BLK, tail, 0)

    def row_copy(r, k):
        d = dest_ref[0, 0, k * tm + r]
        src = h_ref.at[pl.ds(pl.multiple_of(r * XS_SUB, XS_SUB), XS_SUB)]
        return pltpu.make_async_copy(src, xs_ref.at[pl.ds(pl.multiple_of(d * XS_SUB, XS_SUB), XS_SUB)], sem)

    def start(r, carry):
        row_copy(r, 0).start()
        row_copy(r, 1).start()
        return carry

    lax.fori_loop(0, tm, start, 0)
    for k in range(MOE_TOP_K):
        pltpu.make_async_copy(h_ref, xs_ref.at[pl.ds(0, tm * XS_SUB)], sem).wait()


def _dispatch(pend0, dest_tiles, h_packed, *, n_tiles):
    tm = TM_ROUTE
    return pl.pallas_call(
        _dispatch_kernel,
        grid_spec=pltpu.PrefetchScalarGridSpec(
            num_scalar_prefetch=1,
            grid=(n_tiles,),
            in_specs=[
                pl.BlockSpec((1, 1, MOE_TOP_K * tm), lambda i, pend: (i, 0, 0), memory_space=pltpu.SMEM),
                pl.BlockSpec((tm * XS_SUB, V7X_LANES), lambda i, pend: (i, 0)),
            ],
            out_specs=pl.BlockSpec(memory_space=pl.ANY),
            scratch_shapes=[pltpu.VMEM((MOE_BLK * XS_SUB, V7X_LANES), u32), pltpu.SemaphoreType.DMA(())],
        ),
        out_shape=jax.ShapeDtypeStruct((MOE_ROWS * XS_SUB, V7X_LANES), u32),
        compiler_params=_cparams(("arbitrary",)),
        name="moe_dispatch",
    )(pend0, dest_tiles, h_packed)


def _expert_kernel(be_ref, nv_ref, first_ref, slot_ref, next_ref, xs_ref, wg_hbm, wu_hbm, wd_hbm, ys_ref,
                   wg_buf, wu_buf, wd_buf, sems, *, layer):
    i = pl.program_id(0)

    def weight_copies(e, s):
        return (pltpu.make_async_copy(wg_hbm.at[layer, e], wg_buf.at[s], sems.at[s, 0]),
                pltpu.make_async_copy(wu_hbm.at[layer, e], wu_buf.at[s], sems.at[s, 1]),
                pltpu.make_async_copy(wd_hbm.at[layer, e], wd_buf.at[s], sems.at[s, 2]))

    @pl.when(i < nv_ref[0])
    def _():
        s = slot_ref[i]

        @pl.when(first_ref[i] == 1)
        def _():
            @pl.when(i == 0)
            def _():
                for cp in weight_copies(be_ref[0], 0):
                    cp.start()
            for cp in weight_copies(be_ref[i], s):
                cp.wait()

            @pl.when(next_ref[i] >= 0)
            def _():
                for cp in weight_copies(next_ref[i], 1 - s):
                    cp.start()

        parts = [_unpack_bf16_pairs(xs_ref[pl.ds(t, MOE_BLK, stride=XS_SUB), :]) for t in range(XS_SUB)]
        x = jnp.concatenate([p[0] for p in parts] + [p[1] for p in parts], axis=1).astype(bf16)
        gate = _dot(x, wg_buf[s].astype(bf16))
        up = _dot(x, wu_buf[s].astype(bf16))
        h = gate * jax.nn.sigmoid(gate) * up
        y = _dot(h.astype(bf16), wd_buf[s].astype(bf16))
        words = _pack_bf16_pairs(y)
        for t in range(XS_SUB):
            ys_ref[pl.ds(t, MOE_BLK, stride=XS_SUB), :] = words[:, t * V7X_LANES:(t + 1) * V7X_LANES]

    @pl.when(i >= nv_ref[0])
    def _():
        ys_ref[...] = jnp.zeros_like(ys_ref)


def _experts(block_expert, n_valid, first, slot, nxt, xs, w_gate, w_up, w_down, *, layer):
    n_blocks = MOE_ROWS // MOE_BLK

    def row(i, be, nv, *_):
        return jnp.minimum(i, nv[0] - 1)

    return pl.pallas_call(
        functools.partial(_expert_kernel, layer=layer),
        grid_spec=pltpu.PrefetchScalarGridSpec(
            num_scalar_prefetch=5,
            grid=(n_blocks,),
            in_specs=[
                pl.BlockSpec((MOE_BLK * XS_SUB, V7X_LANES), lambda i, *pf: (row(i, *pf), 0)),
                pl.BlockSpec(memory_space=pl.ANY),
                pl.BlockSpec(memory_space=pl.ANY),
                pl.BlockSpec(memory_space=pl.ANY),
            ],
            out_specs=pl.BlockSpec((MOE_BLK * XS_SUB, V7X_LANES), lambda i, *pf: (i, 0)),
            scratch_shapes=[
                pltpu.VMEM((2, D_MODEL, MOE_D_FF), f32),
                pltpu.VMEM((2, D_MODEL, MOE_D_FF), f32),
                pltpu.VMEM((2, MOE_D_FF, D_MODEL), f32),
                pltpu.SemaphoreType.DMA((2, 3)),
            ],
        ),
        out_shape=jax.ShapeDtypeStruct((MOE_ROWS * XS_SUB, V7X_LANES), u32),
        compiler_params=_cparams(("arbitrary",)),
        name="moe_experts",
    )(block_expert, n_valid, first, slot, nxt, xs, w_gate, w_up, w_down)


def _combine_kernel(dest0_ref, destn_ref, x_ref, info_ref, mod_ref, ys_ref, o_ref, buf_ref, sems, *, n_tiles):
    tm = TM_COMBINE
    i = pl.program_id(0)

    def start_tile(dest_ref, slot):
        def row_copy(r, k):
            d = dest_ref[0, 0, k * tm + r]
            src = ys_ref.at[pl.ds(pl.multiple_of(d * XS_SUB, XS_SUB), XS_SUB)]
            dst = buf_ref.at[slot, k, pl.ds(pl.multiple_of(r * XS_SUB, XS_SUB), XS_SUB)]
            return pltpu.make_async_copy(src, dst, sems.at[slot])

        def body(r, carry):
            row_copy(r, 0).start()
            row_copy(r, 1).start()
            return carry

        lax.fori_loop(0, tm, body, 0)

    @pl.when(i == 0)
    def _():
        start_tile(dest0_ref, 0)

    slot = i % 2

    @pl.when(i + 1 < n_tiles)
    def _():
        start_tile(destn_ref, 1 - slot)

    for k in range(MOE_TOP_K):
        pltpu.make_async_copy(ys_ref.at[pl.ds(0, tm * XS_SUB)], buf_ref.at[slot, k], sems.at[slot]).wait()

    info = info_ref[...]
    w1 = info[:, LANE_W1:LANE_W1 + 1]
    w2 = info[:, LANE_W2:LANE_W2 + 1]
    for t in range(XS_SUB):
        rows = pl.ds(t, tm, stride=XS_SUB)
        lo1, hi1 = _unpack_bf16_pairs(buf_ref[slot, 0, rows, :])
        lo2, hi2 = _unpack_bf16_pairs(buf_ref[slot, 1, rows, :])
        for half, (y1, y2) in enumerate(((lo1, lo2), (hi1, hi2))):
            cols = slice(half * PACKED + t * V7X_LANES, half * PACKED + (t + 1) * V7X_LANES)
            o_ref[:, cols] = x_ref[:, cols] + mod_ref[0, 5:6, cols] * (y1 * w1 + y2 * w2)


def _combine(dest_tiles, x, info, mod_l, ys, *, n_tiles, mod_sel):
    tm = TM_COMBINE
    dest_spec = lambda imap: pl.BlockSpec((1, 1, MOE_TOP_K * tm), imap, memory_space=pltpu.SMEM)
    return pl.pallas_call(
        functools.partial(_combine_kernel, n_tiles=n_tiles),
        grid=(n_tiles,),
        in_specs=[
            dest_spec(lambda i: (0, 0, 0)),
            dest_spec(lambda i: (jnp.minimum(i + 1, n_tiles - 1), 0, 0)),
            pl.BlockSpec((tm, D_MODEL), lambda i: (i, 0)),
            pl.BlockSpec((tm, ROUTE_LANES), lambda i: (i, 0)),
            pl.BlockSpec((1, N_MOD, D_MODEL), lambda i: (mod_sel(i), 0, 0)),
            pl.BlockSpec(memory_space=pl.ANY),
        ],
        out_specs=pl.BlockSpec((tm, D_MODEL), lambda i: (i, 0)),
        out_shape=jax.ShapeDtypeStruct(x.shape, f32),
        scratch_shapes=[pltpu.VMEM((2, MOE_TOP_K, tm * XS_SUB, V7X_LANES), u32), pltpu.SemaphoreType.DMA((2,))],
        input_output_aliases={2: 0},
        compiler_params=_cparams(("arbitrary",)),
        name="moe_combine",
    )(dest_tiles, dest_tiles, x, info, mod_l, ys)


def _moe(x, mod_l, g2, w_route, b_route, w_gate, w_up, w_down, *, layer, with_ctx):
    n_tok = N_TOK if with_ctx else N_LAT
    rt = n_tok // TM_ROUTE
    lat_rt = N_LAT // TM_ROUTE
    h_packed, info, cnt = _route(
        x, mod_l, g2, w_route, b_route, n_tiles=rt,
        mod_sel=lambda i: jnp.where(i < lat_rt, i // (SEQ // TM_ROUTE), BATCH))

    counts = cnt[0, :MOE_EXPERTS].astype(i32)
    padded = (counts + MOE_BLK - 1) // MOE_BLK * MOE_BLK
    pend = jnp.cumsum(padded)
    pend0 = jnp.concatenate([jnp.zeros((1,), i32), pend]).astype(i32)
    dest = [jnp.take(pend0, info[:, e_lane].astype(i32)) + info[:, r_lane].astype(i32)
            for e_lane, r_lane in ((LANE_E1, LANE_R1), (LANE_E2, LANE_R2))]
    n_blocks = MOE_ROWS // MOE_BLK
    block_start = jnp.arange(n_blocks, dtype=i32) * MOE_BLK
    block_expert = jnp.minimum(jnp.searchsorted(pend, block_start, side="right"), MOE_EXPERTS - 1).astype(i32)
    n_valid = (pend[-1:] // MOE_BLK).astype(i32)
    blk_id = jnp.arange(n_blocks, dtype=i32)
    valid = blk_id < n_valid[0]
    changed = jnp.concatenate([jnp.ones((1,), bool), block_expert[1:] != block_expert[:-1]])
    first = (changed & valid).astype(i32)
    slot = ((jnp.cumsum(first) - 1) % 2).astype(i32)
    first_pos = jnp.where(first == 1, blk_id, n_blocks)
    next_first = jnp.concatenate([lax.cummin(first_pos, axis=0, reverse=True)[1:],
                                  jnp.full((1,), n_blocks, i32)])
    nxt = jnp.where(next_first < n_blocks, block_expert[jnp.minimum(next_first, n_blocks - 1)], -1).astype(i32)

    def tiles(tm):
        return jnp.concatenate([d.reshape(n_tok // tm, 1, tm) for d in dest], axis=2)

    xs = _dispatch(pend0, tiles(TM_ROUTE), h_packed, n_tiles=rt)
    ys = _experts(block_expert, n_valid, first, slot, nxt, xs, w_gate, w_up, w_down, layer=layer)
    ct = n_tok // TM_COMBINE
    lat_ct = N_LAT // TM_COMBINE
    return _combine(tiles(TM_COMBINE), x, info, mod_l, ys, n_tiles=ct,
                    mod_sel=lambda i: jnp.where(i < lat_ct, i // (SEQ // TM_COMBINE), BATCH))


def _final_norm_kernel(x_ref, g_ref, o_ref):
    x = x_ref[...]
    ms = jnp.mean(x * x, axis=-1, keepdims=True)
    o_ref[...] = x * lax.rsqrt(ms + NORM_EPS) * g_ref[...]


def _final_norm(x, g):
    tm = ROW_CHUNK
    return pl.pallas_call(
        _final_norm_kernel,
        grid=(N_LAT // tm,),
        in_specs=[pl.BlockSpec((tm, D_MODEL), lambda i: (i, 0)), pl.BlockSpec((1, D_MODEL), lambda i: (0, 0))],
        out_specs=pl.BlockSpec((tm, D_MODEL), lambda i: (i, 0)),
        out_shape=jax.ShapeDtypeStruct((N_LAT, D_MODEL), f32),
        compiler_params=_cparams(("parallel",)),
        name="final_norm",
    )(x, g)


def _rope_tables():
    rows = SEQ // GRID_W
    row = jnp.repeat(jnp.arange(rows, dtype=f32), GRID_W)
    col = jnp.tile(jnp.arange(GRID_W, dtype=f32), rows)
    n_freq = HEAD_DIM // 4
    inv_freq = ROPE_BASE ** (-jnp.arange(n_freq, dtype=f32) / n_freq)
    ang = jnp.concatenate([row[:, None] * inv_freq, col[:, None] * inv_freq], axis=-1)
    cos, sin = jnp.cos(ang), jnp.sin(ang)
    return jnp.concatenate([cos, cos], axis=-1), jnp.concatenate([-sin, sin], axis=-1)


def kernel(x, c, ctx, c_ctx, w_mod, b_mod, norm1_g, norm2_g, w_in, attn_sink, pool_w, pool_scale, ret_decay_fwd, ret_decay_bwd, w_br_attn, w_br_pool, w_br_ret, w_out, w_route_group, b_route_group, w_route_expert, b_route_expert, w_expert_gate, w_expert_up, w_expert_down, final_norm_g):
    assert x.shape == (BATCH, SEQ, D_MODEL) and ctx.shape == (BATCH, CTX_LEN, D_MODEL)
    cos_full, sin_signed = _rope_tables()

    cond8 = jnp.concatenate([c, c_ctx[None, :], jnp.zeros((8 - BATCH - 1, D_MODEL), f32)], axis=0)
    mod = _modulation(cond8, w_mod, b_mod).reshape(DEPTH, 8, N_MOD, D_MODEL)

    lg_fwd = jax.nn.log_sigmoid(ret_decay_fwd.astype(f32))
    lg_bwd = jax.nn.log_sigmoid(ret_decay_bwd.astype(f32))
    pad = ROUTE_LANES - MOE_GROUPS - MOE_EXPERTS
    w_route = jnp.concatenate(
        [w_route_group, w_route_expert, jnp.zeros((DEPTH, D_MODEL, pad), f32)], axis=-1)
    b_route = jnp.concatenate([b_route_group, b_route_expert, jnp.zeros((DEPTH, pad), f32)], axis=-1)

    tok = jnp.concatenate([x.reshape(N_LAT, D_MODEL), ctx.reshape(N_CTX, D_MODEL)], axis=0)
    ctx_block0 = N_LAT // TM_CTX
    sel_ctx = lambda i: BATCH

    def sel_rows(tm):
        return lambda i: jnp.where(i < N_LAT // tm, i // (SEQ // tm), BATCH)

    for l in range(DEPTH):
        with_ctx = l < DEPTH - 1
        mod_l = mod[l]
        g1 = norm1_g[l][None, :]
        h1 = _norm(tok, mod_l, g1, n_tiles=N_TOK // TM_NORM, mod_sel=sel_rows(TM_NORM))
        p_lat = _proj(h1, w_in, layer=l, tm=TM_PROJ, row_block0=0, n_tiles=N_LAT // TM_PROJ)
        p_ctx = _proj(h1, w_in, layer=l, tm=TM_CTX, row_block0=ctx_block0, n_tiles=1)

        ya_lat = _attn_lat(p_lat, p_ctx, attn_sink[l], cos_full, sin_signed)
        yp_lat = _pool(p_lat, pool_w[l], pool_scale[l][None, :], n_seq=BATCH, seq_len=SEQ)
        yr_lat, yr_ctx = _retention(p_lat, p_ctx, lg_fwd[l], lg_bwd[l])

        merged = _merge(ya_lat, yp_lat, yr_lat, p_lat, w_br_attn, w_br_pool, w_br_ret,
                        layer=l, tm=TM_MERGE, n_tiles=N_LAT // TM_MERGE)
        tok = _out_proj(merged, w_out, tok, mod_l, layer=l, tm=TM_OUT, row_block0=0, n_tiles=N_LAT // TM_OUT,
                        mod_sel=sel_rows(TM_OUT))
        if with_ctx:
            ya_ctx = _attn_ctx(p_ctx, attn_sink[l])
            yp_ctx = _pool(p_ctx, pool_w[l], pool_scale[l][None, :], n_seq=BATCH, seq_len=CTX_LEN)
            merged_c = _merge(ya_ctx, yp_ctx, yr_ctx, p_ctx, w_br_attn, w_br_pool, w_br_ret,
                              layer=l, tm=TM_CTX, n_tiles=1)
            tok = _out_proj(merged_c, w_out, tok, mod_l, layer=l, tm=TM_CTX, row_block0=ctx_block0, n_tiles=1,
                            mod_sel=sel_ctx)

        tok = _moe(tok, mod_l, norm2_g[l][None, :], w_route[l], b_route[l][None, :],
                   w_expert_gate, w_expert_up, w_expert_down, layer=l, with_ctx=with_ctx)

    out = _final_norm(tok, final_norm_g[None, :])
    return out.reshape(BATCH, SEQ, D_MODEL)
```

```python
import functools

import jax
import jax.numpy as jnp
from jax import lax
from jax.experimental import pallas as pl
from jax.experimental.pallas import tpu as pltpu

f32 = jnp.float32
bf16 = jnp.bfloat16
i32 = jnp.int32
u32 = jnp.uint32

D_MODEL = 2048
BATCH = 2
SEQ = 4096
DEPTH = 4
GRID_W = 64
CTX_LEN = 256
NORM_EPS = 1e-6
N_MOD = 6
HEAD_DIM = 128
ATTN_HEADS = 8
ATTN_KV_HEADS = 2
ATTN_GROUP = ATTN_HEADS // ATTN_KV_HEADS
ATTN_BLOCK = 128
ROPE_BASE = 10000.0
POOL_GROUPS = 4
POOL_GROUP_DIM = 128
POOL_WIDTH = POOL_GROUPS * POOL_GROUP_DIM
POOL_SPANS = (2, 4, 8, 16)
RET_HEADS = 4
RET_QK_DIM = 64
RET_V_DIM = 128
N_BRANCHES = 3
ATTN_Q_WIDTH = ATTN_HEADS * HEAD_DIM
ATTN_KV_WIDTH = ATTN_KV_HEADS * HEAD_DIM
RET_QK_WIDTH = RET_HEADS * RET_QK_DIM
RET_V_WIDTH = RET_HEADS * RET_V_DIM
IN_SIZES = (ATTN_Q_WIDTH, ATTN_KV_WIDTH, ATTN_KV_WIDTH, POOL_WIDTH,
            RET_QK_WIDTH, RET_QK_WIDTH, RET_V_WIDTH, RET_V_WIDTH, N_BRANCHES * D_MODEL)
IN_WIDTH = sum(IN_SIZES)
MOE_GROUPS = 4
MOE_EXPERTS_PER_GROUP = 8
MOE_EXPERTS = MOE_GROUPS * MOE_EXPERTS_PER_GROUP
MOE_TOP_K = 2
MOE_D_FF = 512

COL_QA = 0
COL_KA = COL_QA + ATTN_Q_WIDTH
COL_VA = COL_KA + ATTN_KV_WIDTH
COL_U = COL_VA + ATTN_KV_WIDTH
COL_QR = COL_U + POOL_WIDTH
COL_KR = COL_QR + RET_QK_WIDTH
COL_VR = COL_KR + RET_QK_WIDTH
COL_GR = COL_VR + RET_V_WIDTH
COL_GATES = COL_GR + RET_V_WIDTH

V7X_LANES = 128
V7X_VMEM_LIMIT_BYTES = 56 * 1024 * 1024

N_LAT = BATCH * SEQ
N_CTX = BATCH * CTX_LEN
N_TOK = N_LAT + N_CTX
TM_PROJ = 2048
TM_MERGE = 1024
TM_OUT = 2048
TM_CTX = N_CTX
TM_NORM = 512
TN_PROJ = 512
TN_MERGE = 512
TN_OUT = 512
TN_MOD = 2048
ROW_CHUNK = 128
RET_CHUNK = 256
TM_ROUTE = 512
TM_COMBINE = 256
MOE_BLK = 256
MOE_ROWS = (N_TOK * MOE_TOP_K + MOE_EXPERTS * (MOE_BLK - 1)) // MOE_BLK * MOE_BLK
PACKED = D_MODEL // 2
XS_SUB = PACKED // V7X_LANES
YS_SUB = D_MODEL // V7X_LANES

assert CTX_LEN == RET_CHUNK and SEQ % RET_CHUNK == 0
assert IN_WIDTH % TN_PROJ == 0 and COL_GATES % TN_MERGE == 0


def _cparams(sem, vmem=V7X_VMEM_LIMIT_BYTES):
    return pltpu.CompilerParams(dimension_semantics=sem, vmem_limit_bytes=vmem)


def _dot(a, b):
    return jnp.dot(a, b, preferred_element_type=f32)


def _dot_nt(a, b):
    return lax.dot_general(a, b, (((1,), (1,)), ((), ())), preferred_element_type=f32)


def _dot_tn(a, b):
    return lax.dot_general(a, b, (((0,), (0,)), ((), ())), preferred_element_type=f32)


def _mod_kernel(cond_ref, w_ref, b_ref, o_ref):
    c = cond_ref[...]
    s = (c * jax.nn.sigmoid(c)).astype(bf16)
    o_ref[0] = _dot(s, w_ref[0].astype(bf16)) + b_ref[0]


def _modulation(cond8, w_mod, b_mod):
    n = N_MOD * D_MODEL
    return pl.pallas_call(
        _mod_kernel,
        grid=(DEPTH, n // TN_MOD),
        in_specs=[
            pl.BlockSpec((8, D_MODEL), lambda l, j: (0, 0)),
            pl.BlockSpec((1, D_MODEL, TN_MOD), lambda l, j: (l, 0, j)),
            pl.BlockSpec((1, 1, TN_MOD), lambda l, j: (l, 0, j)),
        ],
        out_specs=pl.BlockSpec((1, 8, TN_MOD), lambda l, j: (l, 0, j)),
        out_shape=jax.ShapeDtypeStruct((DEPTH, 8, n), f32),
        compiler_params=_cparams(("parallel", "parallel")),
        name="modulation",
    )(cond8, w_mod, b_mod.reshape(DEPTH, 1, n))


def _norm_mod_rows(x, g, shift, scale):
    ms = jnp.mean(x * x, axis=-1, keepdims=True)
    y = x * lax.rsqrt(ms + NORM_EPS) * g
    return y * (1.0 + scale) + shift


def _norm_kernel(x_ref, mod_ref, g_ref, o_ref, *, tm):
    g = g_ref[...]
    shift = mod_ref[0, 0:1, :]
    scale = mod_ref[0, 1:2, :]

    def body(r, carry):
        rows = pl.ds(pl.multiple_of(r * ROW_CHUNK, ROW_CHUNK), ROW_CHUNK)
        o_ref[rows, :] = _norm_mod_rows(x_ref[rows, :], g, shift, scale).astype(bf16)
        return carry

    lax.fori_loop(0, tm // ROW_CHUNK, body, 0)


def _norm(x, mod_l, g1, *, n_tiles, mod_sel):
    tm = TM_NORM
    return pl.pallas_call(
        functools.partial(_norm_kernel, tm=tm),
        grid=(n_tiles,),
        in_specs=[
            pl.BlockSpec((tm, D_MODEL), lambda i: (i, 0)),
            pl.BlockSpec((1, N_MOD, D_MODEL), lambda i: (mod_sel(i), 0, 0)),
            pl.BlockSpec((1, D_MODEL), lambda i: (0, 0)),
        ],
        out_specs=pl.BlockSpec((tm, D_MODEL), lambda i: (i, 0)),
        out_shape=jax.ShapeDtypeStruct((n_tiles * tm, D_MODEL), bf16),
        compiler_params=_cparams(("parallel",)),
        name="norm1",
    )(x, mod_l, g1)


def _proj_kernel(h_ref, w_ref, o_ref):
    acc = _dot(h_ref[...], w_ref[...].astype(bf16))
    is_gate = pl.program_id(1) >= COL_GATES // TN_PROJ
    o_ref[...] = jnp.where(is_gate, jax.nn.sigmoid(acc), acc).astype(o_ref.dtype)


def _proj(h, w_in, *, layer, tm, row_block0, n_tiles):
    return pl.pallas_call(
        _proj_kernel,
        grid=(n_tiles, IN_WIDTH // TN_PROJ),
        in_specs=[
            pl.BlockSpec((tm, D_MODEL), lambda i, j: (row_block0 + i, 0)),
            pl.BlockSpec((None, D_MODEL, TN_PROJ), lambda i, j: (layer, 0, j)),
        ],
        out_specs=pl.BlockSpec((tm, TN_PROJ), lambda i, j: (i, j)),
        out_shape=jax.ShapeDtypeStruct((n_tiles * tm, IN_WIDTH), bf16),
        compiler_params=_cparams(("parallel", "arbitrary")),
        name="in_proj",
    )(h, w_in)


def _rope(t, cos_full, sin_signed):
    return t * cos_full + pltpu.roll(t, HEAD_DIM // 2, 1) * sin_signed


def _sink_column(sink_ref, g, rows_per_head):
    n = ATTN_GROUP * rows_per_head
    row = lax.broadcasted_iota(i32, (n, 1), 0)
    col = jnp.full((n, 1), sink_ref[g * ATTN_GROUP + ATTN_GROUP - 1], f32)
    for j in range(ATTN_GROUP - 2, -1, -1):
        col = jnp.where(row < (j + 1) * rows_per_head, sink_ref[g * ATTN_GROUP + j], col)
    return col


def _softmax_pv(scores, values, sink_col):
    def lane_chunks(arrs):
        return [a[:, c * V7X_LANES:(c + 1) * V7X_LANES] for a in arrs for c in range(a.shape[1] // V7X_LANES)]

    folded = functools.reduce(jnp.maximum, lane_chunks(scores))
    m = jnp.maximum(sink_col, jnp.max(folded, axis=-1, keepdims=True))
    es = [jnp.exp(s - m) for s in scores]
    total = functools.reduce(lambda a, b: a + b, lane_chunks(es))
    denom = jnp.exp(sink_col - m) + jnp.sum(total, axis=-1, keepdims=True)
    inv = 1.0 / denom
    out = None
    for e, v in zip(es, values):
        o = _dot((e * inv).astype(bf16), v)
        out = o if out is None else out + o
    return out


def _attn_lat_kernel(sink_ref, q_ref, kp_ref, kc_ref, kn_ref, vp_ref, vc_ref, vn_ref, kx_ref, vx_ref,
                     cq_ref, sq_ref, cp_ref, sp_ref, cn_ref, sn_ref, o_ref, *, n_blocks):
    n = pl.program_id(1)
    blk = ATTN_BLOCK
    scale = HEAD_DIM ** -0.5
    cq, sq = cq_ref[...], sq_ref[...]
    cp, sp = cp_ref[...], sp_ref[...]
    cn, sn = cn_ref[...], sn_ref[...]
    rows = ATTN_GROUP * blk
    qi = lax.broadcasted_iota(i32, (rows, blk), 0) % blk
    kj = lax.broadcasted_iota(i32, (rows, blk), 1)
    neg = jnp.float32(-jnp.inf)
    keep_prev = (kj >= qi) & (n > 0)
    keep_next = (kj <= qi) & (n < n_blocks - 1)

    for g in range(ATTN_KV_HEADS):
        kv = slice(g * HEAD_DIM, (g + 1) * HEAD_DIM)
        qs = jnp.concatenate(
            [(_rope(q_ref[:, (g * ATTN_GROUP + j) * HEAD_DIM:(g * ATTN_GROUP + j + 1) * HEAD_DIM].astype(f32),
                    cq, sq) * scale).astype(bf16) for j in range(ATTN_GROUP)],
            axis=0)
        k_prev = _rope(kp_ref[:, kv].astype(f32), cp, sp).astype(bf16)
        k_cur = _rope(kc_ref[:, kv].astype(f32), cq, sq).astype(bf16)
        k_next = _rope(kn_ref[:, kv].astype(f32), cn, sn).astype(bf16)
        k_ctx = kx_ref[:, kv]
        s_prev = jnp.where(keep_prev, _dot_nt(qs, k_prev), neg)
        s_cur = _dot_nt(qs, k_cur)
        s_next = jnp.where(keep_next, _dot_nt(qs, k_next), neg)
        s_ctx = _dot_nt(qs, k_ctx)
        out = _softmax_pv(
            [s_prev, s_cur, s_next, s_ctx],
            [vp_ref[:, kv], vc_ref[:, kv], vn_ref[:, kv], vx_ref[:, kv]],
            _sink_column(sink_ref, g, blk))
        for j in range(ATTN_GROUP):
            h = g * ATTN_GROUP + j
            o_ref[:, h * HEAD_DIM:(h + 1) * HEAD_DIM] = out[j * blk:(j + 1) * blk, :].astype(o_ref.dtype)


def _attn_lat(p_lat, p_ctx, sink, cos_full, sin_signed):
    nb = SEQ // ATTN_BLOCK
    blk = ATTN_BLOCK
    kcol = COL_KA // ATTN_KV_WIDTH
    vcol = COL_VA // ATTN_KV_WIDTH

    def kv_spec(col, shift):
        return pl.BlockSpec((blk, ATTN_KV_WIDTH), lambda b, n: (b * nb + jnp.clip(n + shift, 0, nb - 1), col))

    def tab_spec(shift):
        return pl.BlockSpec((blk, HEAD_DIM), lambda b, n: (jnp.clip(n + shift, 0, nb - 1), 0))

    return pl.pallas_call(
        functools.partial(_attn_lat_kernel, n_blocks=nb),
        grid=(BATCH, nb),
        in_specs=[
            pl.BlockSpec(memory_space=pltpu.SMEM),
            pl.BlockSpec((blk, ATTN_Q_WIDTH), lambda b, n: (b * nb + n, 0)),
            kv_spec(kcol, -1), kv_spec(kcol, 0), kv_spec(kcol, 1),
            kv_spec(vcol, -1), kv_spec(vcol, 0), kv_spec(vcol, 1),
            pl.BlockSpec((CTX_LEN, ATTN_KV_WIDTH), lambda b, n: (b, kcol)),
            pl.BlockSpec((CTX_LEN, ATTN_KV_WIDTH), lambda b, n: (b, vcol)),
            tab_spec(0), tab_spec(0), tab_spec(-1), tab_spec(-1), tab_spec(1), tab_spec(1),
        ],
        out_specs=pl.BlockSpec((blk, ATTN_Q_WIDTH), lambda b, n: (b * nb + n, 0)),
        out_shape=jax.ShapeDtypeStruct((N_LAT, ATTN_Q_WIDTH), bf16),
        compiler_params=_cparams(("parallel", "arbitrary")),
        name="attn_lat",
    )(sink, p_lat, p_lat, p_lat, p_lat, p_lat, p_lat, p_lat, p_ctx, p_ctx,
      cos_full, sin_signed, cos_full, sin_signed, cos_full, sin_signed)


def _attn_ctx_kernel(sink_ref, q_ref, kx_ref, vx_ref, o_ref):
    g = pl.program_id(1)
    scale = HEAD_DIM ** -0.5
    q = q_ref[...].astype(f32)
    qs = jnp.concatenate(
        [(q[:, j * HEAD_DIM:(j + 1) * HEAD_DIM] * scale).astype(bf16) for j in range(ATTN_GROUP)], axis=0)
    s_ctx = _dot_nt(qs, kx_ref[...])
    out = _softmax_pv([s_ctx], [vx_ref[...]], _sink_column(sink_ref, g, CTX_LEN))
    for j in range(ATTN_GROUP):
        o_ref[:, j * HEAD_DIM:(j + 1) * HEAD_DIM] = out[j * CTX_LEN:(j + 1) * CTX_LEN, :].astype(o_ref.dtype)


def _attn_ctx(p_ctx, sink):
    gw = ATTN_GROUP * HEAD_DIM
    kcol = COL_KA // HEAD_DIM
    vcol = COL_VA // HEAD_DIM
    return pl.pallas_call(
        _attn_ctx_kernel,
        grid=(BATCH, ATTN_KV_HEADS),
        in_specs=[
            pl.BlockSpec(memory_space=pltpu.SMEM),
            pl.BlockSpec((CTX_LEN, gw), lambda b, g: (b, g)),
            pl.BlockSpec((CTX_LEN, HEAD_DIM), lambda b, g: (b, kcol + g)),
            pl.BlockSpec((CTX_LEN, HEAD_DIM), lambda b, g: (b, vcol + g)),
        ],
        out_specs=pl.BlockSpec((CTX_LEN, gw), lambda b, g: (b, g)),
        out_shape=jax.ShapeDtypeStruct((N_CTX, ATTN_Q_WIDTH), bf16),
        compiler_params=_cparams(("parallel", "parallel")),
        name="attn_ctx",
    )(sink, p_ctx, p_ctx, p_ctx)


POOL_HALO = 8
POOL_ROWS = 256


def _pool_kernel(u_ref, w_ref, s_ref, o_ref, pad_ref, *, seq_len):
    zeros = jnp.zeros((POOL_HALO, POOL_GROUP_DIM), f32)
    for gi in range(POOL_GROUPS):
        radius = POOL_SPANS[gi] // 2
        cols = slice(gi * POOL_GROUP_DIM, (gi + 1) * POOL_GROUP_DIM)
        pad_ref[0:POOL_HALO, :] = zeros
        pad_ref[POOL_HALO + seq_len:POOL_HALO + seq_len + POOL_HALO, :] = zeros
        pad_ref[POOL_HALO:POOL_HALO + seq_len, :] = u_ref[:, cols].astype(f32)
        w = w_ref[gi].astype(bf16)
        sc = s_ref[:, cols]
        for c in range(seq_len // POOL_ROWS):
            base = c * POOL_ROWS
            acc = pad_ref[POOL_HALO + base:POOL_HALO + base + POOL_ROWS, :]
            center = acc
            for d in range(1, radius + 1):
                acc = acc + pad_ref[POOL_HALO + base - d:POOL_HALO + base - d + POOL_ROWS, :]
                acc = acc + pad_ref[POOL_HALO + base + d:POOL_HALO + base + d + POOL_ROWS, :]
            t = lax.broadcasted_iota(i32, (POOL_ROWS, 1), 0) + base
            lo = jnp.maximum(t - radius, 0)
            hi = jnp.minimum(t + radius + 1, seq_len)
            count = (hi - lo).astype(f32)
            pooled = (acc / count - center).astype(bf16)
            o_ref[base:base + POOL_ROWS, cols] = (_dot(pooled, w) * sc).astype(o_ref.dtype)


def _pool(p, pool_w, pool_scale, *, n_seq, seq_len):
    ucol = COL_U // POOL_WIDTH
    return pl.pallas_call(
        functools.partial(_pool_kernel, seq_len=seq_len),
        grid=(n_seq,),
        in_specs=[
            pl.BlockSpec((seq_len, POOL_WIDTH), lambda s: (s, ucol)),
            pl.BlockSpec((POOL_GROUPS, POOL_GROUP_DIM, POOL_GROUP_DIM), lambda s: (0, 0, 0)),
            pl.BlockSpec((1, POOL_WIDTH), lambda s: (0, 0)),
        ],
        out_specs=pl.BlockSpec((seq_len, POOL_WIDTH), lambda s: (s, 0)),
        out_shape=jax.ShapeDtypeStruct((n_seq * seq_len, POOL_WIDTH), bf16),
        scratch_shapes=[pltpu.VMEM((seq_len + 2 * POOL_HALO, POOL_GROUP_DIM), f32)],
        compiler_params=_cparams(("parallel",)),
        name="pool",
    )(p, pool_w, pool_scale)


def _ret_tables(lg_ref, intra_ref, qdec_ref, kdec_ref, *, reverse):
    c = RET_CHUNK
    i = lax.broadcasted_iota(i32, (c, c), 0).astype(f32)
    j = lax.broadcasted_iota(i32, (c, c), 1).astype(f32)
    pos = lax.broadcasted_iota(i32, (c, V7X_LANES), 0).astype(f32)
    for h in range(RET_HEADS):
        lg = lg_ref[h]
        if reverse:
            diff = j - i
            qd = c - pos
            kd = pos
        else:
            diff = i - j
            qd = pos + 1.0
            kd = c - 1.0 - pos
        intra_ref[h] = jnp.where(diff >= 0, jnp.exp(lg * jnp.maximum(diff, 0.0)), 0.0)
        qdec_ref[h] = jnp.exp(lg * qd)
        kdec_ref[h] = jnp.exp(lg * kd)


def _ret_chunk(q_ref, k_ref, v_ref, lg_ref, intra_ref, qdec_ref, kdec_ref, state_ref):
    outs = []
    k_scale = RET_QK_DIM ** -0.5
    for h in range(RET_HEADS):
        q = q_ref[:, h * RET_QK_DIM:(h + 1) * RET_QK_DIM].astype(bf16)
        k = k_ref[:, h * RET_QK_DIM:(h + 1) * RET_QK_DIM].astype(f32) * k_scale
        v = v_ref[:, h * RET_V_DIM:(h + 1) * RET_V_DIM].astype(bf16)
        scores = _dot_nt(q, k.astype(bf16)) * intra_ref[h]
        state = state_ref[h]
        out = _dot(scores.astype(bf16), v) + _dot(q, state.astype(bf16)) * qdec_ref[h]
        kd = (k * kdec_ref[h][:, :RET_QK_DIM]).astype(bf16)
        chunk_decay = jnp.exp(jnp.full((1, RET_V_DIM), lg_ref[h] * RET_CHUNK, f32))
        state_ref[h] = state * chunk_decay + _dot_tn(kd, v)
        outs.append(out)
    return outs


def _ret_bwd_kernel(lg_ref, ql_ref, kl_ref, vl_ref, qc_ref, kc_ref, vc_ref, ol_ref, oc_ref,
                    intra_ref, qdec_ref, kdec_ref, state_ref):
    t = pl.program_id(1)

    @pl.when((pl.program_id(0) == 0) & (t == 0))
    def _():
        _ret_tables(lg_ref, intra_ref, qdec_ref, kdec_ref, reverse=True)

    @pl.when(t == 0)
    def _():
        state_ref[...] = jnp.zeros_like(state_ref)
        outs = _ret_chunk(qc_ref, kc_ref, vc_ref, lg_ref, intra_ref, qdec_ref, kdec_ref, state_ref)
        for h in range(RET_HEADS):
            oc_ref[:, h * RET_V_DIM:(h + 1) * RET_V_DIM] = outs[h]

    @pl.when(t > 0)
    def _():
        outs = _ret_chunk(ql_ref, kl_ref, vl_ref, lg_ref, intra_ref, qdec_ref, kdec_ref, state_ref)
        for h in range(RET_HEADS):
            ol_ref[:, h * RET_V_DIM:(h + 1) * RET_V_DIM] = outs[h]


def _ret_finish(outs, yb_ref, g_ref, o_ref):
    for h in range(RET_HEADS):
        cols = slice(h * RET_V_DIM, (h + 1) * RET_V_DIM)
        y = outs[h] + yb_ref[:, cols]
        mu = jnp.mean(y, axis=-1, keepdims=True)
        yc = y - mu
        var = jnp.mean(yc * yc, axis=-1, keepdims=True)
        yn = yc * lax.rsqrt(var + NORM_EPS)
        g = g_ref[:, cols].astype(f32)
        o_ref[:, cols] = (g * jax.nn.sigmoid(g) * yn).astype(o_ref.dtype)


def _ret_fwd_kernel(lg_ref, ql_ref, kl_ref, vl_ref, gl_ref, ybl_ref, qc_ref, kc_ref, vc_ref, gc_ref, ybc_ref,
                    ol_ref, oc_ref, intra_ref, qdec_ref, kdec_ref, state_ref):
    t = pl.program_id(1)

    @pl.when((pl.program_id(0) == 0) & (t == 0))
    def _():
        _ret_tables(lg_ref, intra_ref, qdec_ref, kdec_ref, reverse=False)

    @pl.when(t == 0)
    def _():
        state_ref[...] = jnp.zeros_like(state_ref)
        outs = _ret_chunk(qc_ref, kc_ref, vc_ref, lg_ref, intra_ref, qdec_ref, kdec_ref, state_ref)
        _ret_finish(outs, ybc_ref, gc_ref, oc_ref)

    @pl.when(t > 0)
    def _():
        outs = _ret_chunk(ql_ref, kl_ref, vl_ref, lg_ref, intra_ref, qdec_ref, kdec_ref, state_ref)
        _ret_finish(outs, ybl_ref, gl_ref, ol_ref)


def _ret_scratch():
    c = RET_CHUNK
    return [
        pltpu.VMEM((RET_HEADS, c, c), f32),
        pltpu.VMEM((RET_HEADS, c, V7X_LANES), f32),
        pltpu.VMEM((RET_HEADS, c, V7X_LANES), f32),
        pltpu.VMEM((RET_HEADS, RET_QK_DIM, RET_V_DIM), f32),
    ]


def _retention(p_lat, p_ctx, lg_fwd, lg_bwd):
    c = RET_CHUNK
    nc = SEQ // c
    qcol, kcol = COL_QR // RET_QK_WIDTH, COL_KR // RET_QK_WIDTH
    vcol, gcol = COL_VR // RET_V_WIDTH, COL_GR // RET_V_WIDTH

    def lat_row(reverse):
        def f(b, t):
            s = jnp.maximum(t, 1)
            return b * nc + ((nc - s) if reverse else (s - 1))
        return f

    def specs(reverse, with_gate):
        row = lat_row(reverse)
        lat = [pl.BlockSpec((c, RET_QK_WIDTH), lambda b, t: (row(b, t), qcol)),
               pl.BlockSpec((c, RET_QK_WIDTH), lambda b, t: (row(b, t), kcol)),
               pl.BlockSpec((c, RET_V_WIDTH), lambda b, t: (row(b, t), vcol))]
        ctx = [pl.BlockSpec((c, RET_QK_WIDTH), lambda b, t: (b, qcol)),
               pl.BlockSpec((c, RET_QK_WIDTH), lambda b, t: (b, kcol)),
               pl.BlockSpec((c, RET_V_WIDTH), lambda b, t: (b, vcol))]
        if with_gate:
            lat += [pl.BlockSpec((c, RET_V_WIDTH), lambda b, t: (row(b, t), gcol)),
                    pl.BlockSpec((c, RET_V_WIDTH), lambda b, t: (row(b, t), 0))]
            ctx += [pl.BlockSpec((c, RET_V_WIDTH), lambda b, t: (b, gcol)),
                    pl.BlockSpec((c, RET_V_WIDTH), lambda b, t: (b, 0))]
        outs = [pl.BlockSpec((c, RET_V_WIDTH), lambda b, t: (row(b, t), 0)),
                pl.BlockSpec((c, RET_V_WIDTH), lambda b, t: (b, 0))]
        return [pl.BlockSpec(memory_space=pltpu.SMEM)] + lat + ctx, outs

    in_specs, out_specs = specs(True, False)
    yb_lat, yb_ctx = pl.pallas_call(
        _ret_bwd_kernel,
        grid=(BATCH, nc + 1),
        in_specs=in_specs,
        out_specs=out_specs,
        out_shape=[jax.ShapeDtypeStruct((N_LAT, RET_V_WIDTH), f32),
                   jax.ShapeDtypeStruct((N_CTX, RET_V_WIDTH), f32)],
        scratch_shapes=_ret_scratch(),
        compiler_params=_cparams(("arbitrary", "arbitrary")),
        name="retention_bwd",
    )(lg_bwd, p_lat, p_lat, p_lat, p_ctx, p_ctx, p_ctx)

    in_specs, out_specs = specs(False, True)
    return pl.pallas_call(
        _ret_fwd_kernel,
        grid=(BATCH, nc + 1),
        in_specs=in_specs,
        out_specs=out_specs,
        out_shape=[jax.ShapeDtypeStruct((N_LAT, RET_V_WIDTH), bf16),
                   jax.ShapeDtypeStruct((N_CTX, RET_V_WIDTH), bf16)],
        scratch_shapes=_ret_scratch(),
        compiler_params=_cparams(("arbitrary", "arbitrary")),
        name="retention_fwd",
    )(lg_fwd, p_lat, p_lat, p_lat, p_lat, yb_lat, p_ctx, p_ctx, p_ctx, p_ctx, yb_ctx)


def _merge_kernel(ya_ref, yp_ref, yr_ref, ga_ref, gp_ref, gr_ref, wa_ref, wp_ref, wr_ref, o_ref):
    acc = ga_ref[...].astype(f32) * _dot(ya_ref[...], wa_ref[...].astype(bf16))
    acc = acc + gp_ref[...].astype(f32) * _dot(yp_ref[...], wp_ref[...].astype(bf16))
    acc = acc + gr_ref[...].astype(f32) * _dot(yr_ref[...], wr_ref[...].astype(bf16))
    o_ref[...] = acc.astype(o_ref.dtype)


def _merge(ya, yp, yr, p, w_a, w_p, w_r, *, layer, tm, n_tiles):
    gate0 = COL_GATES // TN_MERGE
    per = D_MODEL // TN_MERGE

    def gate_spec(br):
        return pl.BlockSpec((tm, TN_MERGE), lambda i, j: (i, gate0 + br * per + j))

    return pl.pallas_call(
        _merge_kernel,
        grid=(n_tiles, per),
        in_specs=[
            pl.BlockSpec((tm, ATTN_Q_WIDTH), lambda i, j: (i, 0)),
            pl.BlockSpec((tm, POOL_WIDTH), lambda i, j: (i, 0)),
            pl.BlockSpec((tm, RET_V_WIDTH), lambda i, j: (i, 0)),
            gate_spec(0), gate_spec(1), gate_spec(2),
            pl.BlockSpec((None, ATTN_Q_WIDTH, TN_MERGE), lambda i, j: (layer, 0, j)),
            pl.BlockSpec((None, POOL_WIDTH, TN_MERGE), lambda i, j: (layer, 0, j)),
            pl.BlockSpec((None, RET_V_WIDTH, TN_MERGE), lambda i, j: (layer, 0, j)),
        ],
        out_specs=pl.BlockSpec((tm, TN_MERGE), lambda i, j: (i, j)),
        out_shape=jax.ShapeDtypeStruct((n_tiles * tm, D_MODEL), bf16),
        compiler_params=_cparams(("parallel", "parallel")),
        name="merge",
    )(ya, yp, yr, p, p, p, w_a, w_p, w_r)


def _out_proj_kernel(m_ref, w_ref, x_ref, mod_ref, o_ref):
    y = _dot(m_ref[...], w_ref[...].astype(bf16))
    o_ref[...] = x_ref[...] + mod_ref[0, 2:3, :] * y


def _out_proj(merged, w_out, x, mod_l, *, layer, tm, row_block0, n_tiles, mod_sel):
    return pl.pallas_call(
        _out_proj_kernel,
        grid=(n_tiles, D_MODEL // TN_OUT),
        in_specs=[
            pl.BlockSpec((tm, D_MODEL), lambda i, j: (i, 0)),
            pl.BlockSpec((None, D_MODEL, TN_OUT), lambda i, j: (layer, 0, j)),
            pl.BlockSpec((tm, TN_OUT), lambda i, j: (row_block0 + i, j)),
            pl.BlockSpec((1, N_MOD, TN_OUT), lambda i, j: (mod_sel(i), 0, j)),
        ],
        out_specs=pl.BlockSpec((tm, TN_OUT), lambda i, j: (row_block0 + i, j)),
        out_shape=jax.ShapeDtypeStruct(x.shape, f32),
        input_output_aliases={2: 0},
        compiler_params=_cparams(("parallel", "parallel")),
        name="out_proj",
    )(merged, w_out, x, mod_l)


ROUTE_LANES = V7X_LANES
LANE_E1, LANE_E2, LANE_R1, LANE_R2, LANE_W1, LANE_W2 = 0, 1, 2, 3, 4, 5
PLAN_ROWS = 8
HI16 = 0xFFFF0000


def _pack_bf16_pairs(v):
    n = v.shape[1] // 2
    bits = lax.bitcast_convert_type(v.astype(bf16).astype(f32), u32)
    return (bits[:, :n] >> 16) | (bits[:, n:] & jnp.uint32(HI16))


def _unpack_bf16_pairs(words):
    return (lax.bitcast_convert_type(words << 16, f32),
            lax.bitcast_convert_type(words & jnp.uint32(HI16), f32))


def _route_kernel(x_ref, mod_ref, g_ref, w_ref, b_ref, h_ref, info_ref, plan_ref, cnt_ref,
                  hf_ref, tri_ref, run_ref):
    tm = TM_ROUTE
    step = pl.program_id(0)

    @pl.when(step == 0)
    def _():
        run_ref[...] = jnp.zeros_like(run_ref)
        r = lax.broadcasted_iota(i32, (tm, tm), 0)
        c = lax.broadcasted_iota(i32, (tm, tm), 1)
        tri_ref[...] = jnp.where(c < r, 1.0, 0.0).astype(bf16)

    g = g_ref[...]
    shift = mod_ref[0, 3:4, :]
    scale = mod_ref[0, 4:5, :]

    def body(r, carry):
        rows = pl.ds(pl.multiple_of(r * ROW_CHUNK, ROW_CHUNK), ROW_CHUNK)
        hf_ref[rows, :] = _norm_mod_rows(x_ref[rows, :], g, shift, scale).astype(bf16)
        return carry

    lax.fori_loop(0, tm // ROW_CHUNK, body, 0)

    for c in range(tm // ROW_CHUNK):
        words = _pack_bf16_pairs(hf_ref[c * ROW_CHUNK:(c + 1) * ROW_CHUNK, :])
        for s in range(XS_SUB):
            h_ref[pl.ds(c * ROW_CHUNK * XS_SUB + s, ROW_CHUNK, stride=XS_SUB), :] = (
                words[:, s * V7X_LANES:(s + 1) * V7X_LANES])

    logits = _dot(hf_ref[...], w_ref[...].astype(bf16)) + b_ref[...]
    lane = lax.broadcasted_iota(i32, (tm, ROUTE_LANES), 1)
    neg = jnp.float32(-jnp.inf)
    lane_f = lane.astype(f32)

    def first_argmax(vals):
        top = jnp.max(vals, axis=-1, keepdims=True)
        idx = jnp.min(jnp.where(vals == top, lane_f, float(ROUTE_LANES)), axis=-1, keepdims=True)
        return top, idx.astype(i32)

    g_logits = jnp.where(lane < MOE_GROUPS, logits, neg)
    g_top, grp = first_argmax(g_logits)
    p_grp = 1.0 / jnp.sum(jnp.exp(g_logits - g_top), axis=-1, keepdims=True)

    e_lane = lane - MOE_GROUPS
    in_grp = (e_lane >= 0) & (e_lane < MOE_EXPERTS) & (lax.shift_right_arithmetic(e_lane, 3) == grp)
    e_logits = jnp.where(in_grp, logits, neg)
    top1, idx1 = first_argmax(e_logits)
    e_logits2 = jnp.where(lane == idx1, neg, e_logits)
    top2, idx2 = first_argmax(e_logits2)
    a = jnp.exp(top2 - top1)
    w1 = (1.0 / (1.0 + a)) * p_grp
    w2 = (a / (1.0 + a)) * p_grp
    e1 = idx1 - MOE_GROUPS
    e2 = idx2 - MOE_GROUPS

    hot1 = lane == e1
    hot2 = lane == e2
    o1 = jnp.where(hot1, 1.0, 0.0)
    o2 = jnp.where(hot2, 1.0, 0.0)
    tri = tri_ref[...]
    before1 = _dot(tri, o1.astype(bf16))
    before2 = _dot(tri, o2.astype(bf16))
    c1 = jnp.sum(o1, axis=0, keepdims=True)
    c2 = jnp.sum(o2, axis=0, keepdims=True)
    run = run_ref[...]
    rank1 = jnp.sum(jnp.where(hot1, before1 + run, 0.0), axis=-1, keepdims=True)
    rank2 = jnp.sum(jnp.where(hot2, before2 + run + c1, 0.0), axis=-1, keepdims=True)
    run = run + c1 + c2
    run_ref[...] = run
    cnt_ref[...] = jnp.broadcast_to(run, cnt_ref.shape)

    info = jnp.where(lane == LANE_E1, e1.astype(f32), 0.0)
    info = jnp.where(lane == LANE_E2, e2.astype(f32), info)
    info = jnp.where(lane == LANE_W1, w1, info)
    info = jnp.where(lane == LANE_W2, w2, info)
    info = jnp.where(lane == LANE_R1, rank1, info)
    info = jnp.where(lane == LANE_R2, rank2, info)
    info_ref[...] = info
    plan_ref[0] = jnp.transpose(info)[0:PLAN_ROWS, :].astype(i32)


def _route(x, mod_l, g2, w_route, b_route, *, n_tiles, mod_sel):
    tm = TM_ROUTE
    return pl.pallas_call(
        _route_kernel,
        grid=(n_tiles,),
        in_specs=[
            pl.BlockSpec((tm, D_MODEL), lambda i: (i, 0)),
            pl.BlockSpec((1, N_MOD, D_MODEL), lambda i: (mod_sel(i), 0, 0)),
            pl.BlockSpec((1, D_MODEL), lambda i: (0, 0)),
            pl.BlockSpec((D_MODEL, ROUTE_LANES), lambda i: (0, 0)),
            pl.BlockSpec((1, ROUTE_LANES), lambda i: (0, 0)),
        ],
        out_specs=[
            pl.BlockSpec((tm * XS_SUB, V7X_LANES), lambda i: (i, 0)),
            pl.BlockSpec((tm, ROUTE_LANES), lambda i: (i, 0)),
            pl.BlockSpec((1, PLAN_ROWS, tm), lambda i: (i, 0, 0)),
            pl.BlockSpec((8, ROUTE_LANES), lambda i: (0, 0)),
        ],
        out_shape=[
            jax.ShapeDtypeStruct((n_tiles * tm * XS_SUB, V7X_LANES), u32),
            jax.ShapeDtypeStruct((n_tiles * tm, ROUTE_LANES), f32),
            jax.ShapeDtypeStruct((n_tiles, PLAN_ROWS, tm), i32),
            jax.ShapeDtypeStruct((8, ROUTE_LANES), f32),
        ],
        scratch_shapes=[
            pltpu.VMEM((tm, D_MODEL), bf16),
            pltpu.VMEM((tm, tm), bf16),
            pltpu.VMEM((1, ROUTE_LANES), f32),
        ],
        compiler_params=_cparams(("arbitrary",)),
        name="moe_route",
    )(x, mod_l, g2, w_route, b_route)


def _dest_row(pend_ref, plan_ref, r, k):
    return pend_ref[plan_ref[0, k, r]] + plan_ref[0, MOE_TOP_K + k, r]


def _dispatch_kernel(pend_ref, plan_ref, h_ref, xs_ref, zero_ref, sem):
    tm = TM_ROUTE
    step = pl.program_id(0)

    blk_rows = MOE_BLK * XS_SUB

    def zero_copy(e):
        start = pl.multiple_of((pend_ref[e + 1] - MOE_BLK) * XS_SUB, blk_rows)
        return pltpu.make_async_copy(zero_ref, xs_ref.at[pl.ds(start, blk_rows)], sem)

    def tail_copy(blk):
        start = pl.multiple_of(blk * blk_rows, blk_rows)
        return pltpu.make_async_copy(zero_ref, xs_ref.at[pl.ds(start, blk_rows)], sem)

    @pl.when(step == 0)
    def _():
        zero_ref[...] = jnp.zeros_like(zero_ref)
        first_unused = pend_ref[MOE_EXPERTS] // MOE_BLK
        for phase in range(2):
            def body(e, carry):
                @pl.when(pend_ref[e + 1] > pend_ref[e])
                def _():
                    if phase == 0:
                        zero_copy(e).start()
                    else:
                        zero_copy(e).wait()
                return carry
            lax.fori_loop(0, MOE_EXPERTS, body, 0)

            def tail(blk, carry):
                if phase == 0:
                    tail_copy(blk).start()
                else:
                    tail_copy(blk).wait()
                return carry
            lax.fori_loop(first_unused, MOE_ROWS // MOE_BLK, tail, 0)

    def row_copy(r, k):
        d = _dest_row(pend_ref, plan_ref, r, k)
        src = h_ref.at[pl.ds(pl.multiple_of(r * XS_SUB, XS_SUB), XS_SUB)]
        return pltpu.make_async_copy(src, xs_ref.at[pl.ds(pl.multiple_of(d * XS_SUB, XS_SUB), XS_SUB)], sem)

    def start(r, carry):
        row_copy(r, 0).start()
        row_copy(r, 1).start()
        return carry

    lax.fori_loop(0, tm, start, 0)
    for k in range(MOE_TOP_K):
        pltpu.make_async_copy(h_ref, xs_ref.at[pl.ds(0, tm * XS_SUB)], sem).wait()


def _dispatch(pend0, plan, h_packed, *, n_tiles):
    tm = TM_ROUTE
    return pl.pallas_call(
        _dispatch_kernel,
        grid_spec=pltpu.PrefetchScalarGridSpec(
            num_scalar_prefetch=1,
            grid=(n_tiles,),
            in_specs=[
                pl.BlockSpec((1, PLAN_ROWS, tm), lambda i, pend: (i, 0, 0), memory_space=pltpu.SMEM),
                pl.BlockSpec((tm * XS_SUB, V7X_LANES), lambda i, pend: (i, 0)),
            ],
            out_specs=pl.BlockSpec(memory_space=pl.ANY),
            scratch_shapes=[pltpu.VMEM((MOE_BLK * XS_SUB, V7X_LANES), u32), pltpu.SemaphoreType.DMA(())],
        ),
        out_shape=jax.ShapeDtypeStruct((MOE_ROWS * XS_SUB, V7X_LANES), u32),
        compiler_params=_cparams(("arbitrary",)),
        name="moe_dispatch",
    )(pend0, plan, h_packed)


def _expert_kernel(be_ref, nv_ref, first_ref, slot_ref, next_ref, xs_ref, wg_hbm, wu_hbm, wd_hbm, ys_ref,
                   wg_buf, wu_buf, wd_buf, sems, *, layer):
    i = pl.program_id(0)

    def weight_copies(e, s):
        return (pltpu.make_async_copy(wg_hbm.at[layer, e], wg_buf.at[s], sems.at[s, 0]),
                pltpu.make_async_copy(wu_hbm.at[layer, e], wu_buf.at[s], sems.at[s, 1]),
                pltpu.make_async_copy(wd_hbm.at[layer, e], wd_buf.at[s], sems.at[s, 2]))

    @pl.when(i < nv_ref[0])
    def _():
        s = slot_ref[i]

        @pl.when(first_ref[i] == 1)
        def _():
            @pl.when(i == 0)
            def _():
                for cp in weight_copies(be_ref[0], 0):
                    cp.start()
            for cp in weight_copies(be_ref[i], s):
                cp.wait()

            @pl.when(next_ref[i] >= 0)
            def _():
                for cp in weight_copies(next_ref[i], 1 - s):
                    cp.start()

        parts = [_unpack_bf16_pairs(xs_ref[pl.ds(t, MOE_BLK, stride=XS_SUB), :]) for t in range(XS_SUB)]
        x = jnp.concatenate([p[0] for p in parts] + [p[1] for p in parts], axis=1).astype(bf16)
        gate = _dot(x, wg_buf[s].astype(bf16))
        up = _dot(x, wu_buf[s].astype(bf16))
        h = gate * jax.nn.sigmoid(gate) * up
        y = _dot(h.astype(bf16), wd_buf[s].astype(bf16))
        words = _pack_bf16_pairs(y)
        for t in range(XS_SUB):
            ys_ref[pl.ds(t, MOE_BLK, stride=XS_SUB), :] = words[:, t * V7X_LANES:(t + 1) * V7X_LANES]

    @pl.when(i >= nv_ref[0])
    def _():
        ys_ref[...] = jnp.zeros_like(ys_ref)


def _experts(block_expert, n_valid, first, slot, nxt, xs, w_gate, w_up, w_down, *, layer):
    n_blocks = MOE_ROWS // MOE_BLK

    def row(i, be, nv, *_):
        return jnp.minimum(i, nv[0] - 1)

    return pl.pallas_call(
        functools.partial(_expert_kernel, layer=layer),
        grid_spec=pltpu.PrefetchScalarGridSpec(
            num_scalar_prefetch=5,
            grid=(n_blocks,),
            in_specs=[
                pl.BlockSpec((MOE_BLK * XS_SUB, V7X_LANES), lambda i, *pf: (row(i, *pf), 0)),
                pl.BlockSpec(memory_space=pl.ANY),
                pl.BlockSpec(memory_space=pl.ANY),
                pl.BlockSpec(memory_space=pl.ANY),
            ],
            out_specs=pl.BlockSpec((MOE_BLK * XS_SUB, V7X_LANES), lambda i, *pf: (i, 0)),
            scratch_shapes=[
                pltpu.VMEM((2, D_MODEL, MOE_D_FF), f32),
                pltpu.VMEM((2, D_MODEL, MOE_D_FF), f32),
                pltpu.VMEM((2, MOE_D_FF, D_MODEL), f32),
                pltpu.SemaphoreType.DMA((2, 3)),
            ],
        ),
        out_shape=jax.ShapeDtypeStruct((MOE_ROWS * XS_SUB, V7X_LANES), u32),
        compiler_params=_cparams(("arbitrary",)),
        name="moe_experts",
    )(block_expert, n_valid, first, slot, nxt, xs, w_gate, w_up, w_down)


def _combine_kernel(pend_ref, plan0_ref, plann_ref, x_ref, info_ref, mod_ref, modn_ref, gn_ref, ys_ref,
                    o_ref, hn_ref, buf_ref, sems, *, n_tiles, modulated):
    tm = TM_COMBINE
    i = pl.program_id(0)

    def start_tile(plan_ref, slot):
        def row_copy(r, k):
            d = _dest_row(pend_ref, plan_ref, r, k)
            src = ys_ref.at[pl.ds(pl.multiple_of(d * XS_SUB, XS_SUB), XS_SUB)]
            dst = buf_ref.at[slot, k, pl.ds(pl.multiple_of(r * XS_SUB, XS_SUB), XS_SUB)]
            return pltpu.make_async_copy(src, dst, sems.at[slot])

        def body(r, carry):
            row_copy(r, 0).start()
            row_copy(r, 1).start()
            return carry

        lax.fori_loop(0, tm, body, 0)

    @pl.when(i == 0)
    def _():
        start_tile(plan0_ref, 0)

    slot = i % 2

    @pl.when(i + 1 < n_tiles)
    def _():
        start_tile(plann_ref, 1 - slot)

    for k in range(MOE_TOP_K):
        pltpu.make_async_copy(ys_ref.at[pl.ds(0, tm * XS_SUB)], buf_ref.at[slot, k], sems.at[slot]).wait()

    info = info_ref[...]
    w1 = info[:, LANE_W1:LANE_W1 + 1]
    w2 = info[:, LANE_W2:LANE_W2 + 1]
    for t in range(XS_SUB):
        rows = pl.ds(t, tm, stride=XS_SUB)
        lo1, hi1 = _unpack_bf16_pairs(buf_ref[slot, 0, rows, :])
        lo2, hi2 = _unpack_bf16_pairs(buf_ref[slot, 1, rows, :])
        for half, (y1, y2) in enumerate(((lo1, lo2), (hi1, hi2))):
            cols = slice(half * PACKED + t * V7X_LANES, half * PACKED + (t + 1) * V7X_LANES)
            o_ref[:, cols] = x_ref[:, cols] + mod_ref[0, 5:6, cols] * (y1 * w1 + y2 * w2)

    g = gn_ref[...]
    for c in range(tm // ROW_CHUNK):
        rows = slice(c * ROW_CHUNK, (c + 1) * ROW_CHUNK)
        if modulated:
            y = _norm_mod_rows(o_ref[rows, :], g, modn_ref[0, 0:1, :], modn_ref[0, 1:2, :])
        else:
            xr = o_ref[rows, :]
            y = xr * lax.rsqrt(jnp.mean(xr * xr, axis=-1, keepdims=True) + NORM_EPS) * g
        hn_ref[rows, :] = y.astype(hn_ref.dtype)


def _combine(pend0, plan, x, info, mod_l, mod_next, g_next, ys, *, n_tiles, mod_sel):
    tm = TM_COMBINE
    per = TM_ROUTE // tm
    modulated = mod_next is not None
    if not modulated:
        mod_next = mod_l

    def plan_spec(tile):
        return pl.BlockSpec((1, PLAN_ROWS, tm), lambda i, pend: (tile(i) // per, 0, tile(i) % per),
                            memory_space=pltpu.SMEM)

    return pl.pallas_call(
        functools.partial(_combine_kernel, n_tiles=n_tiles, modulated=modulated),
        grid_spec=pltpu.PrefetchScalarGridSpec(
            num_scalar_prefetch=1,
            grid=(n_tiles,),
            in_specs=[
                plan_spec(lambda i: 0),
                plan_spec(lambda i: jnp.minimum(i + 1, n_tiles - 1)),
                pl.BlockSpec((tm, D_MODEL), lambda i, pend: (i, 0)),
                pl.BlockSpec((tm, ROUTE_LANES), lambda i, pend: (i, 0)),
                pl.BlockSpec((1, N_MOD, D_MODEL), lambda i, pend: (mod_sel(i), 0, 0)),
                pl.BlockSpec((1, N_MOD, D_MODEL), lambda i, pend: (mod_sel(i), 0, 0)),
                pl.BlockSpec((1, D_MODEL), lambda i, pend: (0, 0)),
                pl.BlockSpec(memory_space=pl.ANY),
            ],
            out_specs=[pl.BlockSpec((tm, D_MODEL), lambda i, pend: (i, 0)),
                       pl.BlockSpec((tm, D_MODEL), lambda i, pend: (i, 0))],
            scratch_shapes=[pltpu.VMEM((2, MOE_TOP_K, tm * XS_SUB, V7X_LANES), u32),
                            pltpu.SemaphoreType.DMA((2,))],
        ),
        out_shape=[jax.ShapeDtypeStruct(x.shape, f32),
                   jax.ShapeDtypeStruct((n_tiles * tm, D_MODEL), bf16 if modulated else f32)],
        input_output_aliases={3: 0},
        compiler_params=_cparams(("arbitrary",)),
        name="moe_combine",
    )(pend0, plan, plan, x, info, mod_l, mod_next, g_next, ys)


def _moe(x, mod_l, g2, w_route, b_route, w_gate, w_up, w_down, mod_next, g_next, *, layer, with_ctx):
    n_tok = N_TOK if with_ctx else N_LAT
    rt = n_tok // TM_ROUTE
    lat_rt = N_LAT // TM_ROUTE
    h_packed, info, plan, cnt = _route(
        x, mod_l, g2, w_route, b_route, n_tiles=rt,
        mod_sel=lambda i: jnp.where(i < lat_rt, i // (SEQ // TM_ROUTE), BATCH))

    counts = cnt[0, :MOE_EXPERTS].astype(i32)
    padded = (counts + MOE_BLK - 1) // MOE_BLK * MOE_BLK
    pend = jnp.cumsum(padded)
    pend0 = jnp.concatenate([jnp.zeros((1,), i32), pend]).astype(i32)
    n_blocks = MOE_ROWS // MOE_BLK
    blk_id = jnp.arange(n_blocks, dtype=i32)

    def expert_at(row):
        return jnp.minimum(jnp.sum((pend[None, :] <= row[:, None]).astype(i32), axis=1), MOE_EXPERTS - 1)

    block_expert = expert_at(blk_id * MOE_BLK)
    n_valid = (pend[-1:] // MOE_BLK).astype(i32)
    valid = blk_id < n_valid[0]
    changed = jnp.concatenate([jnp.ones((1,), bool), block_expert[1:] != block_expert[:-1]])
    first = (changed & valid).astype(i32)
    slot = ((jnp.cumsum(first) - 1) % 2).astype(i32)
    own_end = jnp.sum(jnp.where(jnp.arange(MOE_EXPERTS, dtype=i32)[None, :] == block_expert[:, None],
                                pend[None, :], 0), axis=1)
    nxt = jnp.where(own_end < pend[-1], expert_at(own_end), -1).astype(i32)

    xs = _dispatch(pend0, plan, h_packed, n_tiles=rt)
    ys = _experts(block_expert, n_valid, first, slot, nxt, xs, w_gate, w_up, w_down, layer=layer)
    ct = n_tok // TM_COMBINE
    lat_ct = N_LAT // TM_COMBINE
    return _combine(pend0, plan, x, info, mod_l, mod_next, g_next, ys, n_tiles=ct,
                    mod_sel=lambda i: jnp.where(i < lat_ct, i // (SEQ // TM_COMBINE), BATCH))


def _rope_tables():
    rows = SEQ // GRID_W
    row = jnp.repeat(jnp.arange(rows, dtype=f32), GRID_W)
    col = jnp.tile(jnp.arange(GRID_W, dtype=f32), rows)
    n_freq = HEAD_DIM // 4
    inv_freq = ROPE_BASE ** (-jnp.arange(n_freq, dtype=f32) / n_freq)
    ang = jnp.concatenate([row[:, None] * inv_freq, col[:, None] * inv_freq], axis=-1)
    cos, sin = jnp.cos(ang), jnp.sin(ang)
    return jnp.concatenate([cos, cos], axis=-1), jnp.concatenate([-sin, sin], axis=-1)


def kernel(x, c, ctx, c_ctx, w_mod, b_mod, norm1_g, norm2_g, w_in, attn_sink, pool_w, pool_scale, ret_decay_fwd, ret_decay_bwd, w_br_attn, w_br_pool, w_br_ret, w_out, w_route_group, b_route_group, w_route_expert, b_route_expert, w_expert_gate, w_expert_up, w_expert_down, final_norm_g):
    assert x.shape == (BATCH, SEQ, D_MODEL) and ctx.shape == (BATCH, CTX_LEN, D_MODEL)
    cos_full, sin_signed = _rope_tables()

    cond8 = jnp.concatenate([c, c_ctx[None, :], jnp.zeros((8 - BATCH - 1, D_MODEL), f32)], axis=0)
    mod = _modulation(cond8, w_mod, b_mod).reshape(DEPTH, 8, N_MOD, D_MODEL)

    lg_fwd = jax.nn.log_sigmoid(ret_decay_fwd.astype(f32))
    lg_bwd = jax.nn.log_sigmoid(ret_decay_bwd.astype(f32))
    pad = ROUTE_LANES - MOE_GROUPS - MOE_EXPERTS
    w_route = jnp.concatenate(
        [w_route_group, w_route_expert, jnp.zeros((DEPTH, D_MODEL, pad), f32)], axis=-1)
    b_route = jnp.concatenate([b_route_group, b_route_expert, jnp.zeros((DEPTH, pad), f32)], axis=-1)

    tok = jnp.concatenate([x.reshape(N_LAT, D_MODEL), ctx.reshape(N_CTX, D_MODEL)], axis=0)
    ctx_block0 = N_LAT // TM_CTX
    sel_ctx = lambda i: BATCH

    def sel_rows(tm):
        return lambda i: jnp.where(i < N_LAT // tm, i // (SEQ // tm), BATCH)

    h1 = _norm(tok, mod[0], norm1_g[0][None, :], n_tiles=N_TOK // TM_NORM, mod_sel=sel_rows(TM_NORM))
    for l in range(DEPTH):
        with_ctx = l < DEPTH - 1
        mod_l = mod[l]
        p_lat = _proj(h1, w_in, layer=l, tm=TM_PROJ, row_block0=0, n_tiles=N_LAT // TM_PROJ)
        p_ctx = _proj(h1, w_in, layer=l, tm=TM_CTX, row_block0=ctx_block0, n_tiles=1)

        ya_lat = _attn_lat(p_lat, p_ctx, attn_sink[l], cos_full, sin_signed)
        yp_lat = _pool(p_lat, pool_w[l], pool_scale[l][None, :], n_seq=BATCH, seq_len=SEQ)
        yr_lat, yr_ctx = _retention(p_lat, p_ctx, lg_fwd[l], lg_bwd[l])

        merged = _merge(ya_lat, yp_lat, yr_lat, p_lat, w_br_attn, w_br_pool, w_br_ret,
                        layer=l, tm=TM_MERGE, n_tiles=N_LAT // TM_MERGE)
        tok = _out_proj(merged, w_out, tok, mod_l, layer=l, tm=TM_OUT, row_block0=0, n_tiles=N_LAT // TM_OUT,
                        mod_sel=sel_rows(TM_OUT))
        if with_ctx:
            ya_ctx = _attn_ctx(p_ctx, attn_sink[l])
            yp_ctx = _pool(p_ctx, pool_w[l], pool_scale[l][None, :], n_seq=BATCH, seq_len=CTX_LEN)
            merged_c = _merge(ya_ctx, yp_ctx, yr_ctx, p_ctx, w_br_attn, w_br_pool, w_br_ret,
                              layer=l, tm=TM_CTX, n_tiles=1)
            tok = _out_proj(merged_c, w_out, tok, mod_l, layer=l, tm=TM_CTX, row_block0=ctx_block0, n_tiles=1,
                            mod_sel=sel_ctx)

        mod_next = mod[l + 1] if with_ctx else None
        g_next = norm1_g[l + 1][None, :] if with_ctx else final_norm_g[None, :]
        tok, h1 = _moe(tok, mod_l, norm2_g[l][None, :], w_route[l], b_route[l][None, :],
                       w_expert_gate, w_expert_up, w_expert_down, mod_next, g_next, layer=l, with_ctx=with_ctx)

    return h1.reshape(BATCH, SEQ, D_MODEL)
```

```python
import functools

import jax
import jax.numpy as jnp
from jax import lax
from jax.experimental import pallas as pl
from jax.experimental.pallas import tpu as pltpu

f32 = jnp.float32
bf16 = jnp.bfloat16
i32 = jnp.int32
u32 = jnp.uint32

D_MODEL = 2048
BATCH = 2
SEQ = 4096
DEPTH = 4
GRID_W = 64
CTX_LEN = 256
NORM_EPS = 1e-6
N_MOD = 6
HEAD_DIM = 128
ATTN_HEADS = 8
ATTN_KV_HEADS = 2
ATTN_GROUP = ATTN_HEADS // ATTN_KV_HEADS
ATTN_BLOCK = 128
ROPE_BASE = 10000.0
LOG2_E = 1.4426950408889634
POOL_GROUPS = 4
POOL_GROUP_DIM = 128
POOL_WIDTH = POOL_GROUPS * POOL_GROUP_DIM
POOL_SPANS = (2, 4, 8, 16)
RET_HEADS = 4
RET_QK_DIM = 64
RET_V_DIM = 128
N_BRANCHES = 3
ATTN_Q_WIDTH = ATTN_HEADS * HEAD_DIM
ATTN_KV_WIDTH = ATTN_KV_HEADS * HEAD_DIM
RET_QK_WIDTH = RET_HEADS * RET_QK_DIM
RET_V_WIDTH = RET_HEADS * RET_V_DIM
IN_SIZES = (ATTN_Q_WIDTH, ATTN_KV_WIDTH, ATTN_KV_WIDTH, POOL_WIDTH,
            RET_QK_WIDTH, RET_QK_WIDTH, RET_V_WIDTH, RET_V_WIDTH, N_BRANCHES * D_MODEL)
IN_WIDTH = sum(IN_SIZES)
MOE_GROUPS = 4
MOE_EXPERTS_PER_GROUP = 8
MOE_EXPERTS = MOE_GROUPS * MOE_EXPERTS_PER_GROUP
MOE_TOP_K = 2
MOE_D_FF = 512

COL_QA = 0
COL_KA = COL_QA + ATTN_Q_WIDTH
COL_VA = COL_KA + ATTN_KV_WIDTH
COL_U = COL_VA + ATTN_KV_WIDTH
COL_QR = COL_U + POOL_WIDTH
COL_KR = COL_QR + RET_QK_WIDTH
COL_VR = COL_KR + RET_QK_WIDTH
COL_GR = COL_VR + RET_V_WIDTH
COL_GATES = COL_GR + RET_V_WIDTH

V7X_LANES = 128
V7X_VMEM_LIMIT_BYTES = 56 * 1024 * 1024

N_LAT = BATCH * SEQ
N_CTX = BATCH * CTX_LEN
N_TOK = N_LAT + N_CTX
TM_PROJ = 2048
TM_MERGE = 1024
TM_OUT = 2048
TM_CTX = N_CTX
TM_NORM = 512
TN_PROJ = 512
TN_MERGE = 512
TN_OUT = 512
TN_MOD = 2048
ROW_CHUNK = 128
RET_CHUNK = 256
TM_ROUTE = 512
TM_COMBINE = 256
MOE_BLK = 256
MOE_ROWS = (N_TOK * MOE_TOP_K + MOE_EXPERTS * (MOE_BLK - 1)) // MOE_BLK * MOE_BLK
PACKED = D_MODEL // 2
XS_SUB = PACKED // V7X_LANES
YS_SUB = D_MODEL // V7X_LANES

assert CTX_LEN == RET_CHUNK and SEQ % RET_CHUNK == 0
assert IN_WIDTH % TN_PROJ == 0 and COL_GATES % TN_MERGE == 0


def _cparams(sem, vmem=V7X_VMEM_LIMIT_BYTES):
    return pltpu.CompilerParams(dimension_semantics=sem, vmem_limit_bytes=vmem)


def _dot(a, b):
    return jnp.dot(a, b, preferred_element_type=f32)


def _dot_nt(a, b):
    return lax.dot_general(a, b, (((1,), (1,)), ((), ())), preferred_element_type=f32)


def _dot_tn(a, b):
    return lax.dot_general(a, b, (((0,), (0,)), ((), ())), preferred_element_type=f32)


def _mod_kernel(cond_ref, w_ref, b_ref, o_ref):
    c = cond_ref[...]
    s = (c * jax.nn.sigmoid(c)).astype(bf16)
    o_ref[0] = _dot(s, w_ref[0].astype(bf16)) + b_ref[0]


def _modulation(cond8, w_mod, b_mod):
    n = N_MOD * D_MODEL
    return pl.pallas_call(
        _mod_kernel,
        grid=(DEPTH, n // TN_MOD),
        in_specs=[
            pl.BlockSpec((8, D_MODEL), lambda l, j: (0, 0)),
            pl.BlockSpec((1, D_MODEL, TN_MOD), lambda l, j: (l, 0, j)),
            pl.BlockSpec((1, 1, TN_MOD), lambda l, j: (l, 0, j)),
        ],
        out_specs=pl.BlockSpec((1, 8, TN_MOD), lambda l, j: (l, 0, j)),
        out_shape=jax.ShapeDtypeStruct((DEPTH, 8, n), f32),
        compiler_params=_cparams(("parallel", "parallel")),
        name="modulation",
    )(cond8, w_mod, b_mod.reshape(DEPTH, 1, n))


def _norm_mod_rows(x, g, shift, scale):
    ms = jnp.mean(x * x, axis=-1, keepdims=True)
    y = x * lax.rsqrt(ms + NORM_EPS) * g
    return y * (1.0 + scale) + shift


def _norm_kernel(x_ref, mod_ref, g_ref, o_ref, *, tm):
    g = g_ref[...]
    shift = mod_ref[0, 0:1, :]
    scale = mod_ref[0, 1:2, :]

    def body(r, carry):
        rows = pl.ds(pl.multiple_of(r * ROW_CHUNK, ROW_CHUNK), ROW_CHUNK)
        o_ref[rows, :] = _norm_mod_rows(x_ref[rows, :], g, shift, scale).astype(bf16)
        return carry

    lax.fori_loop(0, tm // ROW_CHUNK, body, 0)


def _norm(x, mod_l, g1, *, n_tiles, mod_sel):
    tm = TM_NORM
    return pl.pallas_call(
        functools.partial(_norm_kernel, tm=tm),
        grid=(n_tiles,),
        in_specs=[
            pl.BlockSpec((tm, D_MODEL), lambda i: (i, 0)),
            pl.BlockSpec((1, N_MOD, D_MODEL), lambda i: (mod_sel(i), 0, 0)),
            pl.BlockSpec((1, D_MODEL), lambda i: (0, 0)),
        ],
        out_specs=pl.BlockSpec((tm, D_MODEL), lambda i: (i, 0)),
        out_shape=jax.ShapeDtypeStruct((n_tiles * tm, D_MODEL), bf16),
        compiler_params=_cparams(("parallel",)),
        name="norm1",
    )(x, mod_l, g1)


def _proj_kernel(h_ref, w_ref, o_ref):
    o_ref[...] = _dot(h_ref[...], w_ref[...].astype(bf16)).astype(o_ref.dtype)


def _proj(h, w_in, *, layer, tm, row_block0, n_tiles):
    return pl.pallas_call(
        _proj_kernel,
        grid=(n_tiles, IN_WIDTH // TN_PROJ),
        in_specs=[
            pl.BlockSpec((tm, D_MODEL), lambda i, j: (row_block0 + i, 0)),
            pl.BlockSpec((None, D_MODEL, TN_PROJ), lambda i, j: (layer, 0, j)),
        ],
        out_specs=pl.BlockSpec((tm, TN_PROJ), lambda i, j: (i, j)),
        out_shape=jax.ShapeDtypeStruct((n_tiles * tm, IN_WIDTH), bf16),
        compiler_params=_cparams(("parallel", "arbitrary")),
        name="in_proj",
    )(h, w_in)


def _rope(t, cos_full, sin_signed):
    return t * cos_full + pltpu.roll(t, HEAD_DIM // 2, 1) * sin_signed


def _sink_column(sink_ref, g, rows_per_head):
    n = ATTN_GROUP * rows_per_head
    row = lax.broadcasted_iota(i32, (n, 1), 0)
    col = jnp.full((n, 1), sink_ref[g * ATTN_GROUP + ATTN_GROUP - 1], f32)
    for j in range(ATTN_GROUP - 2, -1, -1):
        col = jnp.where(row < (j + 1) * rows_per_head, sink_ref[g * ATTN_GROUP + j], col)
    return col


def _softmax_pv(scores, values, sink_col):
    def lane_chunks(arrs):
        return [a[:, c * V7X_LANES:(c + 1) * V7X_LANES] for a in arrs for c in range(a.shape[1] // V7X_LANES)]

    folded = functools.reduce(jnp.maximum, lane_chunks(scores))
    m = jnp.maximum(sink_col, jnp.max(folded, axis=-1, keepdims=True))
    es = [jnp.exp2(s - m) for s in scores]
    total = functools.reduce(lambda a, b: a + b, lane_chunks(es))
    denom = jnp.exp2(sink_col - m) + jnp.sum(total, axis=-1, keepdims=True)
    out = None
    for e, v in zip(es, values):
        o = _dot(e.astype(bf16), v)
        out = o if out is None else out + o
    return out * (1.0 / denom)


def _attn_lat_kernel(sink_ref, q_ref, kp_ref, kc_ref, kn_ref, vp_ref, vc_ref, vn_ref, kx_ref, vx_ref,
                     cq_ref, sq_ref, cp_ref, sp_ref, cn_ref, sn_ref, o_ref, *, n_blocks):
    n = pl.program_id(1)
    blk = ATTN_BLOCK
    scale = HEAD_DIM ** -0.5 * LOG2_E
    cq, sq = cq_ref[...], sq_ref[...]
    cp, sp = cp_ref[...], sp_ref[...]
    cn, sn = cn_ref[...], sn_ref[...]
    rows = ATTN_GROUP * blk
    qi = lax.broadcasted_iota(i32, (rows, blk), 0) % blk
    kj = lax.broadcasted_iota(i32, (rows, blk), 1)
    neg = jnp.float32(-jnp.inf)
    keep_prev = (kj >= qi) & (n > 0)
    keep_next = (kj <= qi) & (n < n_blocks - 1)

    for g in range(ATTN_KV_HEADS):
        kv = slice(g * HEAD_DIM, (g + 1) * HEAD_DIM)
        qs = jnp.concatenate(
            [(_rope(q_ref[:, (g * ATTN_GROUP + j) * HEAD_DIM:(g * ATTN_GROUP + j + 1) * HEAD_DIM].astype(f32),
                    cq, sq) * scale).astype(bf16) for j in range(ATTN_GROUP)],
            axis=0)
        k_prev = _rope(kp_ref[:, kv].astype(f32), cp, sp).astype(bf16)
        k_cur = _rope(kc_ref[:, kv].astype(f32), cq, sq).astype(bf16)
        k_next = _rope(kn_ref[:, kv].astype(f32), cn, sn).astype(bf16)
        k_ctx = kx_ref[:, kv]
        s_prev = jnp.where(keep_prev, _dot_nt(qs, k_prev), neg)
        s_cur = _dot_nt(qs, k_cur)
        s_next = jnp.where(keep_next, _dot_nt(qs, k_next), neg)
        s_ctx = _dot_nt(qs, k_ctx)
        out = _softmax_pv(
            [s_prev, s_cur, s_next, s_ctx],
            [vp_ref[:, kv], vc_ref[:, kv], vn_ref[:, kv], vx_ref[:, kv]],
            _sink_column(sink_ref, g, blk) * LOG2_E)
        for j in range(ATTN_GROUP):
            h = g * ATTN_GROUP + j
            o_ref[:, h * HEAD_DIM:(h + 1) * HEAD_DIM] = out[j * blk:(j + 1) * blk, :].astype(o_ref.dtype)


def _attn_lat(p_lat, p_ctx, sink, cos_full, sin_signed):
    nb = SEQ // ATTN_BLOCK
    blk = ATTN_BLOCK
    kcol = COL_KA // ATTN_KV_WIDTH
    vcol = COL_VA // ATTN_KV_WIDTH

    def kv_spec(col, shift):
        return pl.BlockSpec((blk, ATTN_KV_WIDTH), lambda b, n: (b * nb + jnp.clip(n + shift, 0, nb - 1), col))

    def tab_spec(shift):
        return pl.BlockSpec((blk, HEAD_DIM), lambda b, n: (jnp.clip(n + shift, 0, nb - 1), 0))

    return pl.pallas_call(
        functools.partial(_attn_lat_kernel, n_blocks=nb),
        grid=(BATCH, nb),
        in_specs=[
            pl.BlockSpec(memory_space=pltpu.SMEM),
            pl.BlockSpec((blk, ATTN_Q_WIDTH), lambda b, n: (b * nb + n, 0)),
            kv_spec(kcol, -1), kv_spec(kcol, 0), kv_spec(kcol, 1),
            kv_spec(vcol, -1), kv_spec(vcol, 0), kv_spec(vcol, 1),
            pl.BlockSpec((CTX_LEN, ATTN_KV_WIDTH), lambda b, n: (b, kcol)),
            pl.BlockSpec((CTX_LEN, ATTN_KV_WIDTH), lambda b, n: (b, vcol)),
            tab_spec(0), tab_spec(0), tab_spec(-1), tab_spec(-1), tab_spec(1), tab_spec(1),
        ],
        out_specs=pl.BlockSpec((blk, ATTN_Q_WIDTH), lambda b, n: (b * nb + n, 0)),
        out_shape=jax.ShapeDtypeStruct((N_LAT, ATTN_Q_WIDTH), bf16),
        compiler_params=_cparams(("parallel", "arbitrary")),
        name="attn_lat",
    )(sink, p_lat, p_lat, p_lat, p_lat, p_lat, p_lat, p_lat, p_ctx, p_ctx,
      cos_full, sin_signed, cos_full, sin_signed, cos_full, sin_signed)


def _attn_ctx_kernel(sink_ref, q_ref, kx_ref, vx_ref, o_ref):
    g = pl.program_id(1)
    scale = HEAD_DIM ** -0.5 * LOG2_E
    q = q_ref[...].astype(f32)
    qs = jnp.concatenate(
        [(q[:, j * HEAD_DIM:(j + 1) * HEAD_DIM] * scale).astype(bf16) for j in range(ATTN_GROUP)], axis=0)
    s_ctx = _dot_nt(qs, kx_ref[...])
    out = _softmax_pv([s_ctx], [vx_ref[...]], _sink_column(sink_ref, g, CTX_LEN) * LOG2_E)
    for j in range(ATTN_GROUP):
        o_ref[:, j * HEAD_DIM:(j + 1) * HEAD_DIM] = out[j * CTX_LEN:(j + 1) * CTX_LEN, :].astype(o_ref.dtype)


def _attn_ctx(p_ctx, sink):
    gw = ATTN_GROUP * HEAD_DIM
    kcol = COL_KA // HEAD_DIM
    vcol = COL_VA // HEAD_DIM
    return pl.pallas_call(
        _attn_ctx_kernel,
        grid=(BATCH, ATTN_KV_HEADS),
        in_specs=[
            pl.BlockSpec(memory_space=pltpu.SMEM),
            pl.BlockSpec((CTX_LEN, gw), lambda b, g: (b, g)),
            pl.BlockSpec((CTX_LEN, HEAD_DIM), lambda b, g: (b, kcol + g)),
            pl.BlockSpec((CTX_LEN, HEAD_DIM), lambda b, g: (b, vcol + g)),
        ],
        out_specs=pl.BlockSpec((CTX_LEN, gw), lambda b, g: (b, g)),
        out_shape=jax.ShapeDtypeStruct((N_CTX, ATTN_Q_WIDTH), bf16),
        compiler_params=_cparams(("parallel", "parallel")),
        name="attn_ctx",
    )(sink, p_ctx, p_ctx, p_ctx)


POOL_HALO = 8
POOL_ROWS = 256


def _pool_kernel(u_ref, w_ref, s_ref, o_ref, pad_ref, *, seq_len):
    zeros = jnp.zeros((POOL_HALO, POOL_GROUP_DIM), f32)
    for gi in range(POOL_GROUPS):
        radius = POOL_SPANS[gi] // 2
        cols = slice(gi * POOL_GROUP_DIM, (gi + 1) * POOL_GROUP_DIM)
        pad_ref[0:POOL_HALO, :] = zeros
        pad_ref[POOL_HALO + seq_len:POOL_HALO + seq_len + POOL_HALO, :] = zeros
        pad_ref[POOL_HALO:POOL_HALO + seq_len, :] = u_ref[:, cols].astype(f32)
        w = w_ref[gi].astype(bf16)
        sc = s_ref[:, cols]
        for c in range(seq_len // POOL_ROWS):
            base = c * POOL_ROWS
            acc = pad_ref[POOL_HALO + base:POOL_HALO + base + POOL_ROWS, :]
            center = acc
            for d in range(1, radius + 1):
                acc = acc + pad_ref[POOL_HALO + base - d:POOL_HALO + base - d + POOL_ROWS, :]
                acc = acc + pad_ref[POOL_HALO + base + d:POOL_HALO + base + d + POOL_ROWS, :]
            t = lax.broadcasted_iota(i32, (POOL_ROWS, 1), 0) + base
            lo = jnp.maximum(t - radius, 0)
            hi = jnp.minimum(t + radius + 1, seq_len)
            count = (hi - lo).astype(f32)
            pooled = (acc / count - center).astype(bf16)
            o_ref[base:base + POOL_ROWS, cols] = (_dot(pooled, w) * sc).astype(o_ref.dtype)


def _pool(p, pool_w, pool_scale, *, n_seq, seq_len):
    ucol = COL_U // POOL_WIDTH
    return pl.pallas_call(
        functools.partial(_pool_kernel, seq_len=seq_len),
        grid=(n_seq,),
        in_specs=[
            pl.BlockSpec((seq_len, POOL_WIDTH), lambda s: (s, ucol)),
            pl.BlockSpec((POOL_GROUPS, POOL_GROUP_DIM, POOL_GROUP_DIM), lambda s: (0, 0, 0)),
            pl.BlockSpec((1, POOL_WIDTH), lambda s: (0, 0)),
        ],
        out_specs=pl.BlockSpec((seq_len, POOL_WIDTH), lambda s: (s, 0)),
        out_shape=jax.ShapeDtypeStruct((n_seq * seq_len, POOL_WIDTH), bf16),
        scratch_shapes=[pltpu.VMEM((seq_len + 2 * POOL_HALO, POOL_GROUP_DIM), f32)],
        compiler_params=_cparams(("parallel",)),
        name="pool",
    )(p, pool_w, pool_scale)


def _ret_tables(lg_ref, intra_ref, qdec_ref, kdec_ref, *, reverse):
    c = RET_CHUNK
    i = lax.broadcasted_iota(i32, (c, c), 0).astype(f32)
    j = lax.broadcasted_iota(i32, (c, c), 1).astype(f32)
    pos = lax.broadcasted_iota(i32, (c, V7X_LANES), 0).astype(f32)
    for h in range(RET_HEADS):
        lg = lg_ref[h]
        if reverse:
            diff = j - i
            qd = c - pos
            kd = pos
        else:
            diff = i - j
            qd = pos + 1.0
            kd = c - 1.0 - pos
        intra_ref[h] = jnp.where(diff >= 0, jnp.exp(lg * jnp.maximum(diff, 0.0)), 0.0)
        qdec_ref[h] = jnp.exp(lg * qd)
        kdec_ref[h] = jnp.exp(lg * kd)


def _ret_chunk(q_ref, k_ref, v_ref, lg_ref, intra_ref, qdec_ref, kdec_ref, state_ref):
    outs = []
    k_scale = RET_QK_DIM ** -0.5
    for h in range(RET_HEADS):
        q = q_ref[:, h * RET_QK_DIM:(h + 1) * RET_QK_DIM].astype(bf16)
        k = k_ref[:, h * RET_QK_DIM:(h + 1) * RET_QK_DIM].astype(f32) * k_scale
        v = v_ref[:, h * RET_V_DIM:(h + 1) * RET_V_DIM].astype(bf16)
        scores = _dot_nt(q, k.astype(bf16)) * intra_ref[h]
        state = state_ref[h]
        out = _dot(scores.astype(bf16), v) + _dot(q, state.astype(bf16)) * qdec_ref[h]
        kd = (k * kdec_ref[h][:, :RET_QK_DIM]).astype(bf16)
        chunk_decay = jnp.exp(jnp.full((1, RET_V_DIM), lg_ref[h] * RET_CHUNK, f32))
        state_ref[h] = state * chunk_decay + _dot_tn(kd, v)
        outs.append(out)
    return outs


def _ret_bwd_kernel(lg_ref, ql_ref, kl_ref, vl_ref, qc_ref, kc_ref, vc_ref, ol_ref, oc_ref,
                    intra_ref, qdec_ref, kdec_ref, state_ref):
    t = pl.program_id(1)

    @pl.when((pl.program_id(0) == 0) & (t == 0))
    def _():
        _ret_tables(lg_ref, intra_ref, qdec_ref, kdec_ref, reverse=True)

    @pl.when(t == 0)
    def _():
        state_ref[...] = jnp.zeros_like(state_ref)
        outs = _ret_chunk(qc_ref, kc_ref, vc_ref, lg_ref, intra_ref, qdec_ref, kdec_ref, state_ref)
        for h in range(RET_HEADS):
            oc_ref[:, h * RET_V_DIM:(h + 1) * RET_V_DIM] = outs[h]

    @pl.when(t > 0)
    def _():
        outs = _ret_chunk(ql_ref, kl_ref, vl_ref, lg_ref, intra_ref, qdec_ref, kdec_ref, state_ref)
        for h in range(RET_HEADS):
            ol_ref[:, h * RET_V_DIM:(h + 1) * RET_V_DIM] = outs[h]


def _ret_finish(outs, yb_ref, g_ref, o_ref):
    for h in range(RET_HEADS):
        cols = slice(h * RET_V_DIM, (h + 1) * RET_V_DIM)
        y = outs[h] + yb_ref[:, cols]
        mu = jnp.mean(y, axis=-1, keepdims=True)
        yc = y - mu
        var = jnp.mean(yc * yc, axis=-1, keepdims=True)
        yn = yc * lax.rsqrt(var + NORM_EPS)
        g = g_ref[:, cols].astype(f32)
        o_ref[:, cols] = (g * jax.nn.sigmoid(g) * yn).astype(o_ref.dtype)


def _ret_fwd_kernel(lg_ref, ql_ref, kl_ref, vl_ref, gl_ref, ybl_ref, qc_ref, kc_ref, vc_ref, gc_ref, ybc_ref,
                    ol_ref, oc_ref, intra_ref, qdec_ref, kdec_ref, state_ref):
    t = pl.program_id(1)

    @pl.when((pl.program_id(0) == 0) & (t == 0))
    def _():
        _ret_tables(lg_ref, intra_ref, qdec_ref, kdec_ref, reverse=False)

    @pl.when(t == 0)
    def _():
        state_ref[...] = jnp.zeros_like(state_ref)
        outs = _ret_chunk(qc_ref, kc_ref, vc_ref, lg_ref, intra_ref, qdec_ref, kdec_ref, state_ref)
        _ret_finish(outs, ybc_ref, gc_ref, oc_ref)

    @pl.when(t > 0)
    def _():
        outs = _ret_chunk(ql_ref, kl_ref, vl_ref, lg_ref, intra_ref, qdec_ref, kdec_ref, state_ref)
        _ret_finish(outs, ybl_ref, gl_ref, ol_ref)


def _ret_scratch():
    c = RET_CHUNK
    return [
        pltpu.VMEM((RET_HEADS, c, c), f32),
        pltpu.VMEM((RET_HEADS, c, V7X_LANES), f32),
        pltpu.VMEM((RET_HEADS, c, V7X_LANES), f32),
        pltpu.VMEM((RET_HEADS, RET_QK_DIM, RET_V_DIM), f32),
    ]


def _retention(p_lat, p_ctx, lg_fwd, lg_bwd):
    c = RET_CHUNK
    nc = SEQ // c
    qcol, kcol = COL_QR // RET_QK_WIDTH, COL_KR // RET_QK_WIDTH
    vcol, gcol = COL_VR // RET_V_WIDTH, COL_GR // RET_V_WIDTH

    def lat_row(reverse):
        def f(b, t):
            s = jnp.maximum(t, 1)
            return b * nc + ((nc - s) if reverse else (s - 1))
        return f

    def specs(reverse, with_gate):
        row = lat_row(reverse)
        lat = [pl.BlockSpec((c, RET_QK_WIDTH), lambda b, t: (row(b, t), qcol)),
               pl.BlockSpec((c, RET_QK_WIDTH), lambda b, t: (row(b, t), kcol)),
               pl.BlockSpec((c, RET_V_WIDTH), lambda b, t: (row(b, t), vcol))]
        ctx = [pl.BlockSpec((c, RET_QK_WIDTH), lambda b, t: (b, qcol)),
               pl.BlockSpec((c, RET_QK_WIDTH), lambda b, t: (b, kcol)),
               pl.BlockSpec((c, RET_V_WIDTH), lambda b, t: (b, vcol))]
        if with_gate:
            lat += [pl.BlockSpec((c, RET_V_WIDTH), lambda b, t: (row(b, t), gcol)),
                    pl.BlockSpec((c, RET_V_WIDTH), lambda b, t: (row(b, t), 0))]
            ctx += [pl.BlockSpec((c, RET_V_WIDTH), lambda b, t: (b, gcol)),
                    pl.BlockSpec((c, RET_V_WIDTH), lambda b, t: (b, 0))]
        outs = [pl.BlockSpec((c, RET_V_WIDTH), lambda b, t: (row(b, t), 0)),
                pl.BlockSpec((c, RET_V_WIDTH), lambda b, t: (b, 0))]
        return [pl.BlockSpec(memory_space=pltpu.SMEM)] + lat + ctx, outs

    in_specs, out_specs = specs(True, False)
    yb_lat, yb_ctx = pl.pallas_call(
        _ret_bwd_kernel,
        grid=(BATCH, nc + 1),
        in_specs=in_specs,
        out_specs=out_specs,
        out_shape=[jax.ShapeDtypeStruct((N_LAT, RET_V_WIDTH), f32),
                   jax.ShapeDtypeStruct((N_CTX, RET_V_WIDTH), f32)],
        scratch_shapes=_ret_scratch(),
        compiler_params=_cparams(("arbitrary", "arbitrary")),
        name="retention_bwd",
    )(lg_bwd, p_lat, p_lat, p_lat, p_ctx, p_ctx, p_ctx)

    in_specs, out_specs = specs(False, True)
    return pl.pallas_call(
        _ret_fwd_kernel,
        grid=(BATCH, nc + 1),
        in_specs=in_specs,
        out_specs=out_specs,
        out_shape=[jax.ShapeDtypeStruct((N_LAT, RET_V_WIDTH), bf16),
                   jax.ShapeDtypeStruct((N_CTX, RET_V_WIDTH), bf16)],
        scratch_shapes=_ret_scratch(),
        compiler_params=_cparams(("arbitrary", "arbitrary")),
        name="retention_fwd",
    )(lg_fwd, p_lat, p_lat, p_lat, p_lat, yb_lat, p_ctx, p_ctx, p_ctx, p_ctx, yb_ctx)


def _merge_kernel(ya_ref, yp_ref, yr_ref, ga_ref, gp_ref, gr_ref, wa_ref, wp_ref, wr_ref, o_ref):
    acc = jax.nn.sigmoid(ga_ref[...].astype(f32)) * _dot(ya_ref[...], wa_ref[...].astype(bf16))
    acc = acc + jax.nn.sigmoid(gp_ref[...].astype(f32)) * _dot(yp_ref[...], wp_ref[...].astype(bf16))
    acc = acc + jax.nn.sigmoid(gr_ref[...].astype(f32)) * _dot(yr_ref[...], wr_ref[...].astype(bf16))
    o_ref[...] = acc.astype(o_ref.dtype)


def _merge(ya, yp, yr, p, w_a, w_p, w_r, *, layer, tm, n_tiles):
    gate0 = COL_GATES // TN_MERGE
    per = D_MODEL // TN_MERGE

    def gate_spec(br):
        return pl.BlockSpec((tm, TN_MERGE), lambda i, j: (i, gate0 + br * per + j))

    return pl.pallas_call(
        _merge_kernel,
        grid=(n_tiles, per),
        in_specs=[
            pl.BlockSpec((tm, ATTN_Q_WIDTH), lambda i, j: (i, 0)),
            pl.BlockSpec((tm, POOL_WIDTH), lambda i, j: (i, 0)),
            pl.BlockSpec((tm, RET_V_WIDTH), lambda i, j: (i, 0)),
            gate_spec(0), gate_spec(1), gate_spec(2),
            pl.BlockSpec((None, ATTN_Q_WIDTH, TN_MERGE), lambda i, j: (layer, 0, j)),
            pl.BlockSpec((None, POOL_WIDTH, TN_MERGE), lambda i, j: (layer, 0, j)),
            pl.BlockSpec((None, RET_V_WIDTH, TN_MERGE), lambda i, j: (layer, 0, j)),
        ],
        out_specs=pl.BlockSpec((tm, TN_MERGE), lambda i, j: (i, j)),
        out_shape=jax.ShapeDtypeStruct((n_tiles * tm, D_MODEL), bf16),
        compiler_params=_cparams(("parallel", "parallel")),
        name="merge",
    )(ya, yp, yr, p, p, p, w_a, w_p, w_r)


def _out_proj_kernel(m_ref, w_ref, x_ref, mod_ref, o_ref):
    y = _dot(m_ref[...], w_ref[...].astype(bf16))
    o_ref[...] = x_ref[...] + mod_ref[0, 2:3, :] * y


def _out_proj(merged, w_out, x, mod_l, *, layer, tm, row_block0, n_tiles, mod_sel):
    return pl.pallas_call(
        _out_proj_kernel,
        grid=(n_tiles, D_MODEL // TN_OUT),
        in_specs=[
            pl.BlockSpec((tm, D_MODEL), lambda i, j: (i, 0)),
            pl.BlockSpec((None, D_MODEL, TN_OUT), lambda i, j: (layer, 0, j)),
            pl.BlockSpec((tm, TN_OUT), lambda i, j: (row_block0 + i, j)),
            pl.BlockSpec((1, N_MOD, TN_OUT), lambda i, j: (mod_sel(i), 0, j)),
        ],
        out_specs=pl.BlockSpec((tm, TN_OUT), lambda i, j: (row_block0 + i, j)),
        out_shape=jax.ShapeDtypeStruct(x.shape, f32),
        input_output_aliases={2: 0},
        compiler_params=_cparams(("parallel", "parallel")),
        name="out_proj",
    )(merged, w_out, x, mod_l)


ROUTE_LANES = V7X_LANES
LANE_E1, LANE_E2, LANE_R1, LANE_R2, LANE_W1, LANE_W2 = 0, 1, 2, 3, 4, 5
PLAN_ROWS = 8
HI16 = 0xFFFF0000


def _pack_bf16_pairs(v):
    n = v.shape[1] // 2
    bits = lax.bitcast_convert_type(v.astype(bf16).astype(f32), u32)
    return (bits[:, :n] >> 16) | (bits[:, n:] & jnp.uint32(HI16))


def _unpack_bf16_pairs(words):
    return (lax.bitcast_convert_type(words << 16, f32),
            lax.bitcast_convert_type(words & jnp.uint32(HI16), f32))


def _route_kernel(x_ref, mod_ref, g_ref, w_ref, b_ref, h_ref, info_ref, plan_ref, cnt_ref,
                  hf_ref, tri_ref, run_ref):
    tm = TM_ROUTE
    step = pl.program_id(0)

    @pl.when(step == 0)
    def _():
        run_ref[...] = jnp.zeros_like(run_ref)
        r = lax.broadcasted_iota(i32, (tm, tm), 0)
        c = lax.broadcasted_iota(i32, (tm, tm), 1)
        tri_ref[...] = jnp.where(c < r, 1.0, 0.0).astype(bf16)

    g = g_ref[...]
    shift = mod_ref[0, 3:4, :]
    scale = mod_ref[0, 4:5, :]

    def body(r, carry):
        rows = pl.ds(pl.multiple_of(r * ROW_CHUNK, ROW_CHUNK), ROW_CHUNK)
        hf_ref[rows, :] = _norm_mod_rows(x_ref[rows, :], g, shift, scale).astype(bf16)
        return carry

    lax.fori_loop(0, tm // ROW_CHUNK, body, 0)

    for c in range(tm // ROW_CHUNK):
        words = _pack_bf16_pairs(hf_ref[c * ROW_CHUNK:(c + 1) * ROW_CHUNK, :])
        for s in range(XS_SUB):
            h_ref[pl.ds(c * ROW_CHUNK * XS_SUB + s, ROW_CHUNK, stride=XS_SUB), :] = (
                words[:, s * V7X_LANES:(s + 1) * V7X_LANES])

    logits = _dot(hf_ref[...], w_ref[...].astype(bf16)) + b_ref[...]
    lane = lax.broadcasted_iota(i32, (tm, ROUTE_LANES), 1)
    neg = jnp.float32(-jnp.inf)
    lane_f = lane.astype(f32)

    def first_argmax(vals):
        top = jnp.max(vals, axis=-1, keepdims=True)
        idx = jnp.min(jnp.where(vals == top, lane_f, float(ROUTE_LANES)), axis=-1, keepdims=True)
        return top, idx.astype(i32)

    g_logits = jnp.where(lane < MOE_GROUPS, logits, neg)
    g_top, grp = first_argmax(g_logits)
    p_grp = 1.0 / jnp.sum(jnp.exp(g_logits - g_top), axis=-1, keepdims=True)

    e_lane = lane - MOE_GROUPS
    in_grp = (e_lane >= 0) & (e_lane < MOE_EXPERTS) & (lax.shift_right_arithmetic(e_lane, 3) == grp)
    e_logits = jnp.where(in_grp, logits, neg)
    top1, idx1 = first_argmax(e_logits)
    e_logits2 = jnp.where(lane == idx1, neg, e_logits)
    top2, idx2 = first_argmax(e_logits2)
    a = jnp.exp(top2 - top1)
    w1 = (1.0 / (1.0 + a)) * p_grp
    w2 = (a / (1.0 + a)) * p_grp
    e1 = idx1 - MOE_GROUPS
    e2 = idx2 - MOE_GROUPS

    hot1 = lane == e1
    hot2 = lane == e2
    o1 = jnp.where(hot1, 1.0, 0.0)
    o2 = jnp.where(hot2, 1.0, 0.0)
    tri = tri_ref[...]
    before1 = _dot(tri, o1.astype(bf16))
    before2 = _dot(tri, o2.astype(bf16))
    c1 = jnp.sum(o1, axis=0, keepdims=True)
    c2 = jnp.sum(o2, axis=0, keepdims=True)
    run = run_ref[...]
    rank1 = jnp.sum(jnp.where(hot1, before1 + run, 0.0), axis=-1, keepdims=True)
    rank2 = jnp.sum(jnp.where(hot2, before2 + run + c1, 0.0), axis=-1, keepdims=True)
    run = run + c1 + c2
    run_ref[...] = run
    cnt_ref[...] = jnp.broadcast_to(run, cnt_ref.shape)

    info = jnp.where(lane == LANE_E1, e1.astype(f32), 0.0)
    info = jnp.where(lane == LANE_E2, e2.astype(f32), info)
    info = jnp.where(lane == LANE_W1, w1, info)
    info = jnp.where(lane == LANE_W2, w2, info)
    info = jnp.where(lane == LANE_R1, rank1, info)
    info = jnp.where(lane == LANE_R2, rank2, info)
    info_ref[...] = info
    plan_ref[0] = jnp.transpose(info)[0:PLAN_ROWS, :].astype(i32)


def _route(x, mod_l, g2, w_route, b_route, *, n_tiles, mod_sel):
    tm = TM_ROUTE
    return pl.pallas_call(
        _route_kernel,
        grid=(n_tiles,),
        in_specs=[
            pl.BlockSpec((tm, D_MODEL), lambda i: (i, 0)),
            pl.BlockSpec((1, N_MOD, D_MODEL), lambda i: (mod_sel(i), 0, 0)),
            pl.BlockSpec((1, D_MODEL), lambda i: (0, 0)),
            pl.BlockSpec((D_MODEL, ROUTE_LANES), lambda i: (0, 0)),
            pl.BlockSpec((1, ROUTE_LANES), lambda i: (0, 0)),
        ],
        out_specs=[
            pl.BlockSpec((tm * XS_SUB, V7X_LANES), lambda i: (i, 0)),
            pl.BlockSpec((tm, ROUTE_LANES), lambda i: (i, 0)),
            pl.BlockSpec((1, PLAN_ROWS, tm), lambda i: (i, 0, 0)),
            pl.BlockSpec((8, ROUTE_LANES), lambda i: (0, 0)),
        ],
        out_shape=[
            jax.ShapeDtypeStruct((n_tiles * tm * XS_SUB, V7X_LANES), u32),
            jax.ShapeDtypeStruct((n_tiles * tm, ROUTE_LANES), f32),
            jax.ShapeDtypeStruct((n_tiles, PLAN_ROWS, tm), i32),
            jax.ShapeDtypeStruct((8, ROUTE_LANES), f32),
        ],
        scratch_shapes=[
            pltpu.VMEM((tm, D_MODEL), bf16),
            pltpu.VMEM((tm, tm), bf16),
            pltpu.VMEM((1, ROUTE_LANES), f32),
        ],
        compiler_params=_cparams(("arbitrary",)),
        name="moe_route",
    )(x, mod_l, g2, w_route, b_route)


def _dest_kernel(pend_ref, plan_ref, o_ref):
    plan = plan_ref[...]
    start = jnp.zeros_like(plan)
    for e in range(MOE_EXPERTS):
        start = jnp.where(plan == e, pend_ref[e], start)
    o_ref[...] = start + pltpu.roll(plan, plan.shape[0] - MOE_TOP_K, 0)


def _dest(pend0, plan):
    rows = plan.shape[0] * PLAN_ROWS
    return pl.pallas_call(
        _dest_kernel,
        grid_spec=pltpu.PrefetchScalarGridSpec(
            num_scalar_prefetch=1,
            grid=(1,),
            in_specs=[pl.BlockSpec((rows, TM_ROUTE), lambda i, pend: (0, 0))],
            out_specs=pl.BlockSpec((rows, TM_ROUTE), lambda i, pend: (0, 0)),
        ),
        out_shape=jax.ShapeDtypeStruct((rows, TM_ROUTE), i32),
        compiler_params=_cparams(("arbitrary",)),
        name="moe_dest",
    )(pend0, plan.reshape(rows, TM_ROUTE))


def _dispatch_kernel(pend_ref, dest_ref, h_ref, xs_ref, zero_ref, sem):
    tm = TM_ROUTE
    step = pl.program_id(0)

    blk_rows = MOE_BLK * XS_SUB

    def zero_copy(e):
        start = pl.multiple_of((pend_ref[e + 1] - MOE_BLK) * XS_SUB, blk_rows)
        return pltpu.make_async_copy(zero_ref, xs_ref.at[pl.ds(start, blk_rows)], sem)

    def tail_copy(blk):
        start = pl.multiple_of(blk * blk_rows, blk_rows)
        return pltpu.make_async_copy(zero_ref, xs_ref.at[pl.ds(start, blk_rows)], sem)

    @pl.when(step == 0)
    def _():
        zero_ref[...] = jnp.zeros_like(zero_ref)
        first_unused = pend_ref[MOE_EXPERTS] // MOE_BLK
        for phase in range(2):
            def body(e, carry):
                @pl.when(pend_ref[e + 1] > pend_ref[e])
                def _():
                    if phase == 0:
                        zero_copy(e).start()
                    else:
                        zero_copy(e).wait()
                return carry
            lax.fori_loop(0, MOE_EXPERTS, body, 0)

            def tail(blk, carry):
                if phase == 0:
                    tail_copy(blk).start()
                else:
                    tail_copy(blk).wait()
                return carry
            lax.fori_loop(first_unused, MOE_ROWS // MOE_BLK, tail, 0)

    def row_copy(r, k):
        d = dest_ref[k, r]
        src = h_ref.at[pl.ds(pl.multiple_of(r * XS_SUB, XS_SUB), XS_SUB)]
        return pltpu.make_async_copy(src, xs_ref.at[pl.ds(pl.multiple_of(d * XS_SUB, XS_SUB), XS_SUB)], sem)

    def start(r, carry):
        row_copy(r, 0).start(priority=0)
        row_copy(r, 1).start(priority=1)
        return carry

    lax.fori_loop(0, tm, start, 0)
    for k in range(MOE_TOP_K):
        pltpu.make_async_copy(h_ref, xs_ref.at[pl.ds(0, tm * XS_SUB)], sem).wait()


def _dispatch(pend0, dest, h_packed, *, n_tiles):
    tm = TM_ROUTE
    return pl.pallas_call(
        _dispatch_kernel,
        grid_spec=pltpu.PrefetchScalarGridSpec(
            num_scalar_prefetch=1,
            grid=(n_tiles,),
            in_specs=[
                pl.BlockSpec((PLAN_ROWS, tm), lambda i, pend: (i, 0), memory_space=pltpu.SMEM),
                pl.BlockSpec((tm * XS_SUB, V7X_LANES), lambda i, pend: (i, 0)),
            ],
            out_specs=pl.BlockSpec(memory_space=pl.ANY),
            scratch_shapes=[pltpu.VMEM((MOE_BLK * XS_SUB, V7X_LANES), u32), pltpu.SemaphoreType.DMA(())],
        ),
        out_shape=jax.ShapeDtypeStruct((MOE_ROWS * XS_SUB, V7X_LANES), u32),
        compiler_params=_cparams(("arbitrary",)),
        name="moe_dispatch",
    )(pend0, dest, h_packed)


def _expert_kernel(be_ref, nv_ref, first_ref, slot_ref, next_ref, xs_ref, wg_hbm, wu_hbm, wd_hbm, ys_ref,
                   wg_buf, wu_buf, wd_buf, sems, *, layer):
    i = pl.program_id(0)

    def weight_copies(e, s):
        return (pltpu.make_async_copy(wg_hbm.at[layer, e], wg_buf.at[s], sems.at[s, 0]),
                pltpu.make_async_copy(wu_hbm.at[layer, e], wu_buf.at[s], sems.at[s, 1]),
                pltpu.make_async_copy(wd_hbm.at[layer, e], wd_buf.at[s], sems.at[s, 2]))

    @pl.when(i < nv_ref[0])
    def _():
        s = slot_ref[i]

        @pl.when(first_ref[i] == 1)
        def _():
            @pl.when(i == 0)
            def _():
                for cp in weight_copies(be_ref[0], 0):
                    cp.start()
            for cp in weight_copies(be_ref[i], s):
                cp.wait()

            @pl.when(next_ref[i] >= 0)
            def _():
                for cp in weight_copies(next_ref[i], 1 - s):
                    cp.start()

        parts = [_unpack_bf16_pairs(xs_ref[pl.ds(t, MOE_BLK, stride=XS_SUB), :]) for t in range(XS_SUB)]
        x = jnp.concatenate([p[0] for p in parts] + [p[1] for p in parts], axis=1).astype(bf16)
        gate = _dot(x, wg_buf[s].astype(bf16))
        up = _dot(x, wu_buf[s].astype(bf16))
        h = gate * jax.nn.sigmoid(gate) * up
        y = _dot(h.astype(bf16), wd_buf[s].astype(bf16))
        words = _pack_bf16_pairs(y)
        for t in range(XS_SUB):
            ys_ref[pl.ds(t, MOE_BLK, stride=XS_SUB), :] = words[:, t * V7X_LANES:(t + 1) * V7X_LANES]

    @pl.when(i >= nv_ref[0])
    def _():
        ys_ref[...] = jnp.zeros_like(ys_ref)


def _experts(block_expert, n_valid, first, slot, nxt, xs, w_gate, w_up, w_down, *, layer):
    n_blocks = MOE_ROWS // MOE_BLK

    def row(i, be, nv, *_):
        return jnp.minimum(i, nv[0] - 1)

    return pl.pallas_call(
        functools.partial(_expert_kernel, layer=layer),
        grid_spec=pltpu.PrefetchScalarGridSpec(
            num_scalar_prefetch=5,
            grid=(n_blocks,),
            in_specs=[
                pl.BlockSpec((MOE_BLK * XS_SUB, V7X_LANES), lambda i, *pf: (row(i, *pf), 0)),
                pl.BlockSpec(memory_space=pl.ANY),
                pl.BlockSpec(memory_space=pl.ANY),
                pl.BlockSpec(memory_space=pl.ANY),
            ],
            out_specs=pl.BlockSpec((MOE_BLK * XS_SUB, V7X_LANES), lambda i, *pf: (i, 0)),
            scratch_shapes=[
                pltpu.VMEM((2, D_MODEL, MOE_D_FF), f32),
                pltpu.VMEM((2, D_MODEL, MOE_D_FF), f32),
                pltpu.VMEM((2, MOE_D_FF, D_MODEL), f32),
                pltpu.SemaphoreType.DMA((2, 3)),
            ],
        ),
        out_shape=jax.ShapeDtypeStruct((MOE_ROWS * XS_SUB, V7X_LANES), u32),
        compiler_params=_cparams(("arbitrary",)),
        name="moe_experts",
    )(block_expert, n_valid, first, slot, nxt, xs, w_gate, w_up, w_down)


def _combine_kernel(dest0_ref, destn_ref, x_ref, info_ref, mod_ref, modn_ref, gn_ref, ys_ref,
                    o_ref, hn_ref, buf_ref, sems, *, n_tiles, modulated):
    tm = TM_COMBINE
    i = pl.program_id(0)

    def start_tile(dest_ref, slot):
        def row_copy(r, k):
            d = dest_ref[k, r]
            src = ys_ref.at[pl.ds(pl.multiple_of(d * XS_SUB, XS_SUB), XS_SUB)]
            dst = buf_ref.at[slot, k, pl.ds(pl.multiple_of(r * XS_SUB, XS_SUB), XS_SUB)]
            return pltpu.make_async_copy(src, dst, sems.at[slot])

        def body(r, carry):
            row_copy(r, 0).start(priority=0)
            row_copy(r, 1).start(priority=1)
            return carry

        lax.fori_loop(0, tm, body, 0)

    @pl.when(i == 0)
    def _():
        start_tile(dest0_ref, 0)

    slot = i % 2

    @pl.when(i + 1 < n_tiles)
    def _():
        start_tile(destn_ref, 1 - slot)

    for k in range(MOE_TOP_K):
        pltpu.make_async_copy(ys_ref.at[pl.ds(0, tm * XS_SUB)], buf_ref.at[slot, k], sems.at[slot]).wait()

    info = info_ref[...]
    w1 = info[:, LANE_W1:LANE_W1 + 1]
    w2 = info[:, LANE_W2:LANE_W2 + 1]
    for t in range(XS_SUB):
        rows = pl.ds(t, tm, stride=XS_SUB)
        lo1, hi1 = _unpack_bf16_pairs(buf_ref[slot, 0, rows, :])
        lo2, hi2 = _unpack_bf16_pairs(buf_ref[slot, 1, rows, :])
        for half, (y1, y2) in enumerate(((lo1, lo2), (hi1, hi2))):
            cols = slice(half * PACKED + t * V7X_LANES, half * PACKED + (t + 1) * V7X_LANES)
            o_ref[:, cols] = x_ref[:, cols] + mod_ref[0, 5:6, cols] * (y1 * w1 + y2 * w2)

    g = gn_ref[...]
    for c in range(tm // ROW_CHUNK):
        rows = slice(c * ROW_CHUNK, (c + 1) * ROW_CHUNK)
        if modulated:
            y = _norm_mod_rows(o_ref[rows, :], g, modn_ref[0, 0:1, :], modn_ref[0, 1:2, :])
        else:
            xr = o_ref[rows, :]
            y = xr * lax.rsqrt(jnp.mean(xr * xr, axis=-1, keepdims=True) + NORM_EPS) * g
        hn_ref[rows, :] = y.astype(hn_ref.dtype)


def _combine(dest, x, info, mod_l, mod_next, g_next, ys, *, n_tiles, mod_sel):
    tm = TM_COMBINE
    per = TM_ROUTE // tm
    modulated = mod_next is not None
    if not modulated:
        mod_next = mod_l

    def dest_spec(tile):
        return pl.BlockSpec((PLAN_ROWS, tm), lambda i: (tile(i) // per, tile(i) % per), memory_space=pltpu.SMEM)

    return pl.pallas_call(
        functools.partial(_combine_kernel, n_tiles=n_tiles, modulated=modulated),
        grid=(n_tiles,),
        in_specs=[
            dest_spec(lambda i: 0),
            dest_spec(lambda i: jnp.minimum(i + 1, n_tiles - 1)),
            pl.BlockSpec((tm, D_MODEL), lambda i: (i, 0)),
            pl.BlockSpec((tm, ROUTE_LANES), lambda i: (i, 0)),
            pl.BlockSpec((1, N_MOD, D_MODEL), lambda i: (mod_sel(i), 0, 0)),
            pl.BlockSpec((1, N_MOD, D_MODEL), lambda i: (mod_sel(i), 0, 0)),
            pl.BlockSpec((1, D_MODEL), lambda i: (0, 0)),
            pl.BlockSpec(memory_space=pl.ANY),
        ],
        out_specs=[pl.BlockSpec((tm, D_MODEL), lambda i: (i, 0)),
                   pl.BlockSpec((tm, D_MODEL), lambda i: (i, 0))],
        scratch_shapes=[pltpu.VMEM((2, MOE_TOP_K, tm * XS_SUB, V7X_LANES), u32),
                        pltpu.SemaphoreType.DMA((2,))],
        out_shape=[jax.ShapeDtypeStruct(x.shape, f32),
                   jax.ShapeDtypeStruct((n_tiles * tm, D_MODEL), bf16 if modulated else f32)],
        input_output_aliases={2: 0},
        compiler_params=_cparams(("arbitrary",)),
        name="moe_combine",
    )(dest, dest, x, info, mod_l, mod_next, g_next, ys)


def _moe(x, mod_l, g2, w_route, b_route, w_gate, w_up, w_down, mod_next, g_next, *, layer, with_ctx):
    n_tok = N_TOK if with_ctx else N_LAT
    rt = n_tok // TM_ROUTE
    lat_rt = N_LAT // TM_ROUTE
    h_packed, info, plan, cnt = _route(
        x, mod_l, g2, w_route, b_route, n_tiles=rt,
        mod_sel=lambda i: jnp.where(i < lat_rt, i // (SEQ // TM_ROUTE), BATCH))

    counts = cnt[0, :MOE_EXPERTS].astype(i32)
    padded = (counts + MOE_BLK - 1) // MOE_BLK * MOE_BLK
    pend = jnp.cumsum(padded)
    pend0 = jnp.concatenate([jnp.zeros((1,), i32), pend]).astype(i32)
    n_blocks = MOE_ROWS // MOE_BLK
    blk_id = jnp.arange(n_blocks, dtype=i32)

    def expert_at(row):
        return jnp.minimum(jnp.sum((pend[None, :] <= row[:, None]).astype(i32), axis=1), MOE_EXPERTS - 1)

    block_expert = expert_at(blk_id * MOE_BLK)
    n_valid = (pend[-1:] // MOE_BLK).astype(i32)
    valid = blk_id < n_valid[0]
    changed = jnp.concatenate([jnp.ones((1,), bool), block_expert[1:] != block_expert[:-1]])
    first = (changed & valid).astype(i32)
    slot = ((jnp.cumsum(first) - 1) % 2).astype(i32)
    own_end = jnp.sum(jnp.where(jnp.arange(MOE_EXPERTS, dtype=i32)[None, :] == block_expert[:, None],
                                pend[None, :], 0), axis=1)
    nxt = jnp.where(own_end < pend[-1], expert_at(own_end), -1).astype(i32)

    dest = _dest(pend0, plan)
    xs = _dispatch(pend0, dest, h_packed, n_tiles=rt)
    ys = _experts(block_expert, n_valid, first, slot, nxt, xs, w_gate, w_up, w_down, layer=layer)
    ct = n_tok // TM_COMBINE
    lat_ct = N_LAT // TM_COMBINE
    return _combine(dest, x, info, mod_l, mod_next, g_next, ys, n_tiles=ct,
                    mod_sel=lambda i: jnp.where(i < lat_ct, i // (SEQ // TM_COMBINE), BATCH))


def _rope_tables():
    rows = SEQ // GRID_W
    row = jnp.repeat(jnp.arange(rows, dtype=f32), GRID_W)
    col = jnp.tile(jnp.arange(GRID_W, dtype=f32), rows)
    n_freq = HEAD_DIM // 4
    inv_freq = ROPE_BASE ** (-jnp.arange(n_freq, dtype=f32) / n_freq)
    ang = jnp.concatenate([row[:, None] * inv_freq, col[:, None] * inv_freq], axis=-1)
    cos, sin = jnp.cos(ang), jnp.sin(ang)
    return jnp.concatenate([cos, cos], axis=-1), jnp.concatenate([-sin, sin], axis=-1)


def kernel(x, c, ctx, c_ctx, w_mod, b_mod, norm1_g, norm2_g, w_in, attn_sink, pool_w, pool_scale, ret_decay_fwd, ret_decay_bwd, w_br_attn, w_br_pool, w_br_ret, w_out, w_route_group, b_route_group, w_route_expert, b_route_expert, w_expert_gate, w_expert_up, w_expert_down, final_norm_g):
    assert x.shape == (BATCH, SEQ, D_MODEL) and ctx.shape == (BATCH, CTX_LEN, D_MODEL)
    cos_full, sin_signed = _rope_tables()

    cond8 = jnp.concatenate([c, c_ctx[None, :], jnp.zeros((8 - BATCH - 1, D_MODEL), f32)], axis=0)
    mod = _modulation(cond8, w_mod, b_mod).reshape(DEPTH, 8, N_MOD, D_MODEL)

    lg_fwd = jax.nn.log_sigmoid(ret_decay_fwd.astype(f32))
    lg_bwd = jax.nn.log_sigmoid(ret_decay_bwd.astype(f32))
    pad = ROUTE_LANES - MOE_GROUPS - MOE_EXPERTS
    w_route = jnp.concatenate(
        [w_route_group, w_route_expert, jnp.zeros((DEPTH, D_MODEL, pad), f32)], axis=-1)
    b_route = jnp.concatenate([b_route_group, b_route_expert, jnp.zeros((DEPTH, pad), f32)], axis=-1)

    tok = jnp.concatenate([x.reshape(N_LAT, D_MODEL), ctx.reshape(N_CTX, D_MODEL)], axis=0)
    ctx_block0 = N_LAT // TM_CTX
    sel_ctx = lambda i: BATCH

    def sel_rows(tm):
        return lambda i: jnp.where(i < N_LAT // tm, i // (SEQ // tm), BATCH)

    h1 = _norm(tok, mod[0], norm1_g[0][None, :], n_tiles=N_TOK // TM_NORM, mod_sel=sel_rows(TM_NORM))
    for l in range(DEPTH):
        with_ctx = l < DEPTH - 1
        mod_l = mod[l]
        p_lat = _proj(h1, w_in, layer=l, tm=TM_PROJ, row_block0=0, n_tiles=N_LAT // TM_PROJ)
        p_ctx = _proj(h1, w_in, layer=l, tm=TM_CTX, row_block0=ctx_block0, n_tiles=1)

        ya_lat = _attn_lat(p_lat, p_ctx, attn_sink[l], cos_full, sin_signed)
        yp_lat = _pool(p_lat, pool_w[l], pool_scale[l][None, :], n_seq=BATCH, seq_len=SEQ)
        yr_lat, yr_ctx = _retention(p_lat, p_ctx, lg_fwd[l], lg_bwd[l])

        merged = _merge(ya_lat, yp_lat, yr_lat, p_lat, w_br_attn, w_br_pool, w_br_ret,
                        layer=l, tm=TM_MERGE, n_tiles=N_LAT // TM_MERGE)
        tok = _out_proj(merged, w_out, tok, mod_l, layer=l, tm=TM_OUT, row_block0=0, n_tiles=N_LAT // TM_OUT,
                        mod_sel=sel_rows(TM_OUT))
        if with_ctx:
            ya_ctx = _attn_ctx(p_ctx, attn_sink[l])
            yp_ctx = _pool(p_ctx, pool_w[l], pool_scale[l][None, :], n_seq=BATCH, seq_len=CTX_LEN)
            merged_c = _merge(ya_ctx, yp_ctx, yr_ctx, p_ctx, w_br_attn, w_br_pool, w_br_ret,
                              layer=l, tm=TM_CTX, n_tiles=1)
            tok = _out_proj(merged_c, w_out, tok, mod_l, layer=l, tm=TM_CTX, row_block0=ctx_block0, n_tiles=1,
                            mod_sel=sel_ctx)

        mod_next = mod[l + 1] if with_ctx else None
        g_next = norm1_g[l + 1][None, :] if with_ctx else final_norm_g[None, :]
        tok, h1 = _moe(tok, mod_l, norm2_g[l][None, :], w_route[l], b_route[l][None, :],
                       w_expert_gate, w_expert_up, w_expert_down, mod_next, g_next, layer=l, with_ctx=with_ctx)

    return h1.reshape(BATCH, SEQ, D_MODEL)
```

```python
import functools

import jax
import jax.numpy as jnp
from jax import lax
from jax.experimental import pallas as pl
from jax.experimental.pallas import tpu as pltpu

f32 = jnp.float32
bf16 = jnp.bfloat16
i32 = jnp.int32
u32 = jnp.uint32

D_MODEL = 2048
BATCH = 2
SEQ = 4096
DEPTH = 4
GRID_W = 64
CTX_LEN = 256
NORM_EPS = 1e-6
N_MOD = 6
HEAD_DIM = 128
ATTN_HEADS = 8
ATTN_KV_HEADS = 2
ATTN_GROUP = ATTN_HEADS // ATTN_KV_HEADS
ATTN_BLOCK = 128
ROPE_BASE = 10000.0
LOG2_E = 1.4426950408889634
POOL_GROUPS = 4
POOL_GROUP_DIM = 128
POOL_WIDTH = POOL_GROUPS * POOL_GROUP_DIM
POOL_SPANS = (2, 4, 8, 16)
RET_HEADS = 4
RET_QK_DIM = 64
RET_V_DIM = 128
N_BRANCHES = 3
ATTN_Q_WIDTH = ATTN_HEADS * HEAD_DIM
ATTN_KV_WIDTH = ATTN_KV_HEADS * HEAD_DIM
RET_QK_WIDTH = RET_HEADS * RET_QK_DIM
RET_V_WIDTH = RET_HEADS * RET_V_DIM
IN_SIZES = (ATTN_Q_WIDTH, ATTN_KV_WIDTH, ATTN_KV_WIDTH, POOL_WIDTH,
            RET_QK_WIDTH, RET_QK_WIDTH, RET_V_WIDTH, RET_V_WIDTH, N_BRANCHES * D_MODEL)
IN_WIDTH = sum(IN_SIZES)
MOE_GROUPS = 4
MOE_EXPERTS_PER_GROUP = 8
MOE_EXPERTS = MOE_GROUPS * MOE_EXPERTS_PER_GROUP
MOE_TOP_K = 2
MOE_D_FF = 512

COL_QA = 0
COL_KA = COL_QA + ATTN_Q_WIDTH
COL_VA = COL_KA + ATTN_KV_WIDTH
COL_U = COL_VA + ATTN_KV_WIDTH
COL_QR = COL_U + POOL_WIDTH
COL_KR = COL_QR + RET_QK_WIDTH
COL_VR = COL_KR + RET_QK_WIDTH
COL_GR = COL_VR + RET_V_WIDTH
COL_GATES = COL_GR + RET_V_WIDTH

V7X_LANES = 128
V7X_VMEM_LIMIT_BYTES = 56 * 1024 * 1024

N_LAT = BATCH * SEQ
N_CTX = BATCH * CTX_LEN
N_TOK = N_LAT + N_CTX
TM_PROJ = 2048
TM_MERGE = 2048
TM_CTX = N_CTX
TM_NORM = 512
TN_PROJ = 512
TN_MERGE = 256
TN_MOD = 2048
ROW_CHUNK = 128
RET_CHUNK = 256
TM_ROUTE = 512
TM_COMBINE = 256
MOE_BLK = 256
MOE_ROWS = (N_TOK * MOE_TOP_K + MOE_EXPERTS * (MOE_BLK - 1)) // MOE_BLK * MOE_BLK
PACKED = D_MODEL // 2
XS_SUB = PACKED // V7X_LANES
YS_SUB = D_MODEL // V7X_LANES

assert CTX_LEN == RET_CHUNK and SEQ % RET_CHUNK == 0
assert IN_WIDTH % TN_PROJ == 0 and COL_GATES % TN_MERGE == 0


def _cparams(sem, vmem=V7X_VMEM_LIMIT_BYTES):
    return pltpu.CompilerParams(dimension_semantics=sem, vmem_limit_bytes=vmem)


def _dot(a, b):
    return jnp.dot(a, b, preferred_element_type=f32)


def _dot_nt(a, b):
    return lax.dot_general(a, b, (((1,), (1,)), ((), ())), preferred_element_type=f32)


def _dot_tn(a, b):
    return lax.dot_general(a, b, (((0,), (0,)), ((), ())), preferred_element_type=f32)


def _mod_kernel(cond_ref, w_ref, b_ref, o_ref):
    c = cond_ref[...]
    s = (c * jax.nn.sigmoid(c)).astype(bf16)
    o_ref[0] = _dot(s, w_ref[0].astype(bf16)) + b_ref[0]


def _modulation(cond8, w_mod, b_mod):
    n = N_MOD * D_MODEL
    return pl.pallas_call(
        _mod_kernel,
        grid=(DEPTH, n // TN_MOD),
        in_specs=[
            pl.BlockSpec((8, D_MODEL), lambda l, j: (0, 0)),
            pl.BlockSpec((1, D_MODEL, TN_MOD), lambda l, j: (l, 0, j)),
            pl.BlockSpec((1, 1, TN_MOD), lambda l, j: (l, 0, j)),
        ],
        out_specs=pl.BlockSpec((1, 8, TN_MOD), lambda l, j: (l, 0, j)),
        out_shape=jax.ShapeDtypeStruct((DEPTH, 8, n), f32),
        compiler_params=_cparams(("parallel", "parallel")),
        name="modulation",
    )(cond8, w_mod, b_mod.reshape(DEPTH, 1, n))


def _norm_mod_rows(x, g, shift, scale):
    ms = jnp.mean(x * x, axis=-1, keepdims=True)
    y = x * lax.rsqrt(ms + NORM_EPS) * g
    return y * (1.0 + scale) + shift


def _norm_kernel(x_ref, mod_ref, g_ref, o_ref, *, tm):
    g = g_ref[...]
    shift = mod_ref[0, 0:1, :]
    scale = mod_ref[0, 1:2, :]

    def body(r, carry):
        rows = pl.ds(pl.multiple_of(r * ROW_CHUNK, ROW_CHUNK), ROW_CHUNK)
        o_ref[rows, :] = _norm_mod_rows(x_ref[rows, :], g, shift, scale).astype(bf16)
        return carry

    lax.fori_loop(0, tm // ROW_CHUNK, body, 0)


def _norm(x, mod_l, g1, *, n_tiles, mod_sel):
    tm = TM_NORM
    return pl.pallas_call(
        functools.partial(_norm_kernel, tm=tm),
        grid=(n_tiles,),
        in_specs=[
            pl.BlockSpec((tm, D_MODEL), lambda i: (i, 0)),
            pl.BlockSpec((1, N_MOD, D_MODEL), lambda i: (mod_sel(i), 0, 0)),
            pl.BlockSpec((1, D_MODEL), lambda i: (0, 0)),
        ],
        out_specs=pl.BlockSpec((tm, D_MODEL), lambda i: (i, 0)),
        out_shape=jax.ShapeDtypeStruct((n_tiles * tm, D_MODEL), bf16),
        compiler_params=_cparams(("parallel",)),
        name="norm1",
    )(x, mod_l, g1)


def _proj_kernel(h_ref, w_ref, o_ref):
    o_ref[...] = _dot(h_ref[...], w_ref[...].astype(bf16)).astype(o_ref.dtype)


def _proj(h, w_in, *, layer, tm, row_block0, n_tiles):
    return pl.pallas_call(
        _proj_kernel,
        grid=(n_tiles, IN_WIDTH // TN_PROJ),
        in_specs=[
            pl.BlockSpec((tm, D_MODEL), lambda i, j: (row_block0 + i, 0)),
            pl.BlockSpec((None, D_MODEL, TN_PROJ), lambda i, j: (layer, 0, j)),
        ],
        out_specs=pl.BlockSpec((tm, TN_PROJ), lambda i, j: (i, j)),
        out_shape=jax.ShapeDtypeStruct((n_tiles * tm, IN_WIDTH), bf16),
        compiler_params=_cparams(("parallel", "arbitrary")),
        name="in_proj",
    )(h, w_in)


def _rope(t, cos_full, sin_signed):
    return t * cos_full + pltpu.roll(t, HEAD_DIM // 2, 1) * sin_signed


def _sink_column(sink_ref, g, rows_per_head):
    n = ATTN_GROUP * rows_per_head
    row = lax.broadcasted_iota(i32, (n, 1), 0)
    col = jnp.full((n, 1), sink_ref[g * ATTN_GROUP + ATTN_GROUP - 1], f32)
    for j in range(ATTN_GROUP - 2, -1, -1):
        col = jnp.where(row < (j + 1) * rows_per_head, sink_ref[g * ATTN_GROUP + j], col)
    return col


def _softmax_pv(scores, values, sink_col):
    def lane_chunks(arrs):
        return [a[:, c * V7X_LANES:(c + 1) * V7X_LANES] for a in arrs for c in range(a.shape[1] // V7X_LANES)]

    folded = functools.reduce(jnp.maximum, lane_chunks(scores))
    m = jnp.maximum(sink_col, jnp.max(folded, axis=-1, keepdims=True))
    es = [jnp.exp2(s - m) for s in scores]
    total = functools.reduce(lambda a, b: a + b, lane_chunks(es))
    denom = jnp.exp2(sink_col - m) + jnp.sum(total, axis=-1, keepdims=True)
    out = None
    for e, v in zip(es, values):
        o = _dot(e.astype(bf16), v)
        out = o if out is None else out + o
    return out * (1.0 / denom)


def _attn_lat_kernel(sink_ref, q_ref, kp_ref, kc_ref, kn_ref, vp_ref, vc_ref, vn_ref, kx_ref, vx_ref,
                     cq_ref, sq_ref, cp_ref, sp_ref, cn_ref, sn_ref, o_ref, *, n_blocks):
    n = pl.program_id(1)
    blk = ATTN_BLOCK
    scale = HEAD_DIM ** -0.5 * LOG2_E
    cq, sq = cq_ref[...], sq_ref[...]
    cp, sp = cp_ref[...], sp_ref[...]
    cn, sn = cn_ref[...], sn_ref[...]
    rows = ATTN_GROUP * blk
    qi = lax.broadcasted_iota(i32, (rows, blk), 0) % blk
    kj = lax.broadcasted_iota(i32, (rows, blk), 1)
    neg = jnp.float32(-jnp.inf)
    keep_prev = (kj >= qi) & (n > 0)
    keep_next = (kj <= qi) & (n < n_blocks - 1)

    for g in range(ATTN_KV_HEADS):
        kv = slice(g * HEAD_DIM, (g + 1) * HEAD_DIM)
        qs = jnp.concatenate(
            [(_rope(q_ref[:, (g * ATTN_GROUP + j) * HEAD_DIM:(g * ATTN_GROUP + j + 1) * HEAD_DIM].astype(f32),
                    cq, sq) * scale).astype(bf16) for j in range(ATTN_GROUP)],
            axis=0)
        k_prev = _rope(kp_ref[:, kv].astype(f32), cp, sp).astype(bf16)
        k_cur = _rope(kc_ref[:, kv].astype(f32), cq, sq).astype(bf16)
        k_next = _rope(kn_ref[:, kv].astype(f32), cn, sn).astype(bf16)
        k_ctx = kx_ref[:, kv]
        s_prev = jnp.where(keep_prev, _dot_nt(qs, k_prev), neg)
        s_cur = _dot_nt(qs, k_cur)
        s_next = jnp.where(keep_next, _dot_nt(qs, k_next), neg)
        s_ctx = _dot_nt(qs, k_ctx)
        out = _softmax_pv(
            [s_prev, s_cur, s_next, s_ctx],
            [vp_ref[:, kv], vc_ref[:, kv], vn_ref[:, kv], vx_ref[:, kv]],
            _sink_column(sink_ref, g, blk) * LOG2_E)
        for j in range(ATTN_GROUP):
            h = g * ATTN_GROUP + j
            o_ref[:, h * HEAD_DIM:(h + 1) * HEAD_DIM] = out[j * blk:(j + 1) * blk, :].astype(o_ref.dtype)


def _attn_lat(p_lat, p_ctx, sink, cos_full, sin_signed):
    nb = SEQ // ATTN_BLOCK
    blk = ATTN_BLOCK
    kcol = COL_KA // ATTN_KV_WIDTH
    vcol = COL_VA // ATTN_KV_WIDTH

    def kv_spec(col, shift):
        return pl.BlockSpec((blk, ATTN_KV_WIDTH), lambda b, n: (b * nb + jnp.clip(n + shift, 0, nb - 1), col))

    def tab_spec(shift):
        return pl.BlockSpec((blk, HEAD_DIM), lambda b, n: (jnp.clip(n + shift, 0, nb - 1), 0))

    return pl.pallas_call(
        functools.partial(_attn_lat_kernel, n_blocks=nb),
        grid=(BATCH, nb),
        in_specs=[
            pl.BlockSpec(memory_space=pltpu.SMEM),
            pl.BlockSpec((blk, ATTN_Q_WIDTH), lambda b, n: (b * nb + n, 0)),
            kv_spec(kcol, -1), kv_spec(kcol, 0), kv_spec(kcol, 1),
            kv_spec(vcol, -1), kv_spec(vcol, 0), kv_spec(vcol, 1),
            pl.BlockSpec((CTX_LEN, ATTN_KV_WIDTH), lambda b, n: (b, kcol)),
            pl.BlockSpec((CTX_LEN, ATTN_KV_WIDTH), lambda b, n: (b, vcol)),
            tab_spec(0), tab_spec(0), tab_spec(-1), tab_spec(-1), tab_spec(1), tab_spec(1),
        ],
        out_specs=pl.BlockSpec((blk, ATTN_Q_WIDTH), lambda b, n: (b * nb + n, 0)),
        out_shape=jax.ShapeDtypeStruct((N_LAT, ATTN_Q_WIDTH), bf16),
        compiler_params=_cparams(("parallel", "arbitrary")),
        name="attn_lat",
    )(sink, p_lat, p_lat, p_lat, p_lat, p_lat, p_lat, p_lat, p_ctx, p_ctx,
      cos_full, sin_signed, cos_full, sin_signed, cos_full, sin_signed)


def _attn_ctx_kernel(sink_ref, q_ref, kx_ref, vx_ref, o_ref):
    g = pl.program_id(1)
    scale = HEAD_DIM ** -0.5 * LOG2_E
    q = q_ref[...].astype(f32)
    qs = jnp.concatenate(
        [(q[:, j * HEAD_DIM:(j + 1) * HEAD_DIM] * scale).astype(bf16) for j in range(ATTN_GROUP)], axis=0)
    s_ctx = _dot_nt(qs, kx_ref[...])
    out = _softmax_pv([s_ctx], [vx_ref[...]], _sink_column(sink_ref, g, CTX_LEN) * LOG2_E)
    for j in range(ATTN_GROUP):
        o_ref[:, j * HEAD_DIM:(j + 1) * HEAD_DIM] = out[j * CTX_LEN:(j + 1) * CTX_LEN, :].astype(o_ref.dtype)


def _attn_ctx(p_ctx, sink):
    gw = ATTN_GROUP * HEAD_DIM
    kcol = COL_KA // HEAD_DIM
    vcol = COL_VA // HEAD_DIM
    return pl.pallas_call(
        _attn_ctx_kernel,
        grid=(BATCH, ATTN_KV_HEADS),
        in_specs=[
            pl.BlockSpec(memory_space=pltpu.SMEM),
            pl.BlockSpec((CTX_LEN, gw), lambda b, g: (b, g)),
            pl.BlockSpec((CTX_LEN, HEAD_DIM), lambda b, g: (b, kcol + g)),
            pl.BlockSpec((CTX_LEN, HEAD_DIM), lambda b, g: (b, vcol + g)),
        ],
        out_specs=pl.BlockSpec((CTX_LEN, gw), lambda b, g: (b, g)),
        out_shape=jax.ShapeDtypeStruct((N_CTX, ATTN_Q_WIDTH), bf16),
        compiler_params=_cparams(("parallel", "parallel")),
        name="attn_ctx",
    )(sink, p_ctx, p_ctx, p_ctx)


POOL_HALO = 8
POOL_ROWS = 256


def _pool_kernel(u_ref, w_ref, s_ref, o_ref, pad_ref, *, seq_len):
    zeros = jnp.zeros((POOL_HALO, POOL_GROUP_DIM), f32)
    for gi in range(POOL_GROUPS):
        radius = POOL_SPANS[gi] // 2
        cols = slice(gi * POOL_GROUP_DIM, (gi + 1) * POOL_GROUP_DIM)
        pad_ref[0:POOL_HALO, :] = zeros
        pad_ref[POOL_HALO + seq_len:POOL_HALO + seq_len + POOL_HALO, :] = zeros
        pad_ref[POOL_HALO:POOL_HALO + seq_len, :] = u_ref[:, cols].astype(f32)
        w = w_ref[gi].astype(bf16)
        sc = s_ref[:, cols]
        for c in range(seq_len // POOL_ROWS):
            base = c * POOL_ROWS
            acc = pad_ref[POOL_HALO + base:POOL_HALO + base + POOL_ROWS, :]
            center = acc
            for d in range(1, radius + 1):
                acc = acc + pad_ref[POOL_HALO + base - d:POOL_HALO + base - d + POOL_ROWS, :]
                acc = acc + pad_ref[POOL_HALO + base + d:POOL_HALO + base + d + POOL_ROWS, :]
            t = lax.broadcasted_iota(i32, (POOL_ROWS, 1), 0) + base
            lo = jnp.maximum(t - radius, 0)
            hi = jnp.minimum(t + radius + 1, seq_len)
            count = (hi - lo).astype(f32)
            pooled = (acc / count - center).astype(bf16)
            o_ref[base:base + POOL_ROWS, cols] = (_dot(pooled, w) * sc).astype(o_ref.dtype)


def _pool(p, pool_w, pool_scale, *, n_seq, seq_len):
    ucol = COL_U // POOL_WIDTH
    return pl.pallas_call(
        functools.partial(_pool_kernel, seq_len=seq_len),
        grid=(n_seq,),
        in_specs=[
            pl.BlockSpec((seq_len, POOL_WIDTH), lambda s: (s, ucol)),
            pl.BlockSpec((POOL_GROUPS, POOL_GROUP_DIM, POOL_GROUP_DIM), lambda s: (0, 0, 0)),
            pl.BlockSpec((1, POOL_WIDTH), lambda s: (0, 0)),
        ],
        out_specs=pl.BlockSpec((seq_len, POOL_WIDTH), lambda s: (s, 0)),
        out_shape=jax.ShapeDtypeStruct((n_seq * seq_len, POOL_WIDTH), bf16),
        scratch_shapes=[pltpu.VMEM((seq_len + 2 * POOL_HALO, POOL_GROUP_DIM), f32)],
        compiler_params=_cparams(("parallel",)),
        name="pool",
    )(p, pool_w, pool_scale)


def _ret_tables(lg_ref, intra_ref, qdec_ref, kdec_ref, *, reverse):
    c = RET_CHUNK
    i = lax.broadcasted_iota(i32, (c, c), 0).astype(f32)
    j = lax.broadcasted_iota(i32, (c, c), 1).astype(f32)
    pos = lax.broadcasted_iota(i32, (c, V7X_LANES), 0).astype(f32)
    for h in range(RET_HEADS):
        lg = lg_ref[h]
        if reverse:
            diff = j - i
            qd = c - pos
            kd = pos
        else:
            diff = i - j
            qd = pos + 1.0
            kd = c - 1.0 - pos
        intra_ref[h] = jnp.where(diff >= 0, jnp.exp(lg * jnp.maximum(diff, 0.0)), 0.0)
        qdec_ref[h] = jnp.exp(lg * qd)
        kdec_ref[h] = jnp.exp(lg * kd)


def _ret_chunk(q_ref, k_ref, v_ref, lg_ref, intra_ref, qdec_ref, kdec_ref, state_ref):
    outs = []
    k_scale = RET_QK_DIM ** -0.5
    for h in range(RET_HEADS):
        q = q_ref[:, h * RET_QK_DIM:(h + 1) * RET_QK_DIM].astype(bf16)
        k = k_ref[:, h * RET_QK_DIM:(h + 1) * RET_QK_DIM].astype(f32) * k_scale
        v = v_ref[:, h * RET_V_DIM:(h + 1) * RET_V_DIM].astype(bf16)
        scores = _dot_nt(q, k.astype(bf16)) * intra_ref[h]
        state = state_ref[h]
        out = _dot(scores.astype(bf16), v) + _dot(q, state.astype(bf16)) * qdec_ref[h]
        kd = (k * kdec_ref[h][:, :RET_QK_DIM]).astype(bf16)
        chunk_decay = jnp.exp(jnp.full((1, RET_V_DIM), lg_ref[h] * RET_CHUNK, f32))
        state_ref[h] = state * chunk_decay + _dot_tn(kd, v)
        outs.append(out)
    return outs


def _ret_bwd_kernel(lg_ref, ql_ref, kl_ref, vl_ref, qc_ref, kc_ref, vc_ref, ol_ref, oc_ref,
                    intra_ref, qdec_ref, kdec_ref, state_ref):
    t = pl.program_id(1)

    @pl.when((pl.program_id(0) == 0) & (t == 0))
    def _():
        _ret_tables(lg_ref, intra_ref, qdec_ref, kdec_ref, reverse=True)

    @pl.when(t == 0)
    def _():
        state_ref[...] = jnp.zeros_like(state_ref)
        outs = _ret_chunk(qc_ref, kc_ref, vc_ref, lg_ref, intra_ref, qdec_ref, kdec_ref, state_ref)
        for h in range(RET_HEADS):
            oc_ref[:, h * RET_V_DIM:(h + 1) * RET_V_DIM] = outs[h]

    @pl.when(t > 0)
    def _():
        outs = _ret_chunk(ql_ref, kl_ref, vl_ref, lg_ref, intra_ref, qdec_ref, kdec_ref, state_ref)
        for h in range(RET_HEADS):
            ol_ref[:, h * RET_V_DIM:(h + 1) * RET_V_DIM] = outs[h]


def _ret_finish(outs, yb_ref, g_ref, o_ref):
    for h in range(RET_HEADS):
        cols = slice(h * RET_V_DIM, (h + 1) * RET_V_DIM)
        y = outs[h] + yb_ref[:, cols]
        mu = jnp.mean(y, axis=-1, keepdims=True)
        yc = y - mu
        var = jnp.mean(yc * yc, axis=-1, keepdims=True)
        yn = yc * lax.rsqrt(var + NORM_EPS)
        g = g_ref[:, cols].astype(f32)
        o_ref[:, cols] = (g * jax.nn.sigmoid(g) * yn).astype(o_ref.dtype)


def _ret_fwd_kernel(lg_ref, ql_ref, kl_ref, vl_ref, gl_ref, ybl_ref, qc_ref, kc_ref, vc_ref, gc_ref, ybc_ref,
                    ol_ref, oc_ref, intra_ref, qdec_ref, kdec_ref, state_ref):
    t = pl.program_id(1)

    @pl.when((pl.program_id(0) == 0) & (t == 0))
    def _():
        _ret_tables(lg_ref, intra_ref, qdec_ref, kdec_ref, reverse=False)

    @pl.when(t == 0)
    def _():
        state_ref[...] = jnp.zeros_like(state_ref)
        outs = _ret_chunk(qc_ref, kc_ref, vc_ref, lg_ref, intra_ref, qdec_ref, kdec_ref, state_ref)
        _ret_finish(outs, ybc_ref, gc_ref, oc_ref)

    @pl.when(t > 0)
    def _():
        outs = _ret_chunk(ql_ref, kl_ref, vl_ref, lg_ref, intra_ref, qdec_ref, kdec_ref, state_ref)
        _ret_finish(outs, ybl_ref, gl_ref, ol_ref)


def _ret_scratch():
    c = RET_CHUNK
    return [
        pltpu.VMEM((RET_HEADS, c, c), f32),
        pltpu.VMEM((RET_HEADS, c, V7X_LANES), f32),
        pltpu.VMEM((RET_HEADS, c, V7X_LANES), f32),
        pltpu.VMEM((RET_HEADS, RET_QK_DIM, RET_V_DIM), f32),
    ]


def _retention(p_lat, p_ctx, lg_fwd, lg_bwd):
    c = RET_CHUNK
    nc = SEQ // c
    qcol, kcol = COL_QR // RET_QK_WIDTH, COL_KR // RET_QK_WIDTH
    vcol, gcol = COL_VR // RET_V_WIDTH, COL_GR // RET_V_WIDTH

    def lat_row(reverse):
        def f(b, t):
            s = jnp.maximum(t, 1)
            return b * nc + ((nc - s) if reverse else (s - 1))
        return f

    def specs(reverse, with_gate):
        row = lat_row(reverse)
        lat = [pl.BlockSpec((c, RET_QK_WIDTH), lambda b, t: (row(b, t), qcol)),
               pl.BlockSpec((c, RET_QK_WIDTH), lambda b, t: (row(b, t), kcol)),
               pl.BlockSpec((c, RET_V_WIDTH), lambda b, t: (row(b, t), vcol))]
        ctx = [pl.BlockSpec((c, RET_QK_WIDTH), lambda b, t: (b, qcol)),
               pl.BlockSpec((c, RET_QK_WIDTH), lambda b, t: (b, kcol)),
               pl.BlockSpec((c, RET_V_WIDTH), lambda b, t: (b, vcol))]
        if with_gate:
            lat += [pl.BlockSpec((c, RET_V_WIDTH), lambda b, t: (row(b, t), gcol)),
                    pl.BlockSpec((c, RET_V_WIDTH), lambda b, t: (row(b, t), 0))]
            ctx += [pl.BlockSpec((c, RET_V_WIDTH), lambda b, t: (b, gcol)),
                    pl.BlockSpec((c, RET_V_WIDTH), lambda b, t: (b, 0))]
        outs = [pl.BlockSpec((c, RET_V_WIDTH), lambda b, t: (row(b, t), 0)),
                pl.BlockSpec((c, RET_V_WIDTH), lambda b, t: (b, 0))]
        return [pl.BlockSpec(memory_space=pltpu.SMEM)] + lat + ctx, outs

    in_specs, out_specs = specs(True, False)
    yb_lat, yb_ctx = pl.pallas_call(
        _ret_bwd_kernel,
        grid=(BATCH, nc + 1),
        in_specs=in_specs,
        out_specs=out_specs,
        out_shape=[jax.ShapeDtypeStruct((N_LAT, RET_V_WIDTH), f32),
                   jax.ShapeDtypeStruct((N_CTX, RET_V_WIDTH), f32)],
        scratch_shapes=_ret_scratch(),
        compiler_params=_cparams(("arbitrary", "arbitrary")),
        name="retention_bwd",
    )(lg_bwd, p_lat, p_lat, p_lat, p_ctx, p_ctx, p_ctx)

    in_specs, out_specs = specs(False, True)
    return pl.pallas_call(
        _ret_fwd_kernel,
        grid=(BATCH, nc + 1),
        in_specs=in_specs,
        out_specs=out_specs,
        out_shape=[jax.ShapeDtypeStruct((N_LAT, RET_V_WIDTH), bf16),
                   jax.ShapeDtypeStruct((N_CTX, RET_V_WIDTH), bf16)],
        scratch_shapes=_ret_scratch(),
        compiler_params=_cparams(("arbitrary", "arbitrary")),
        name="retention_fwd",
    )(lg_fwd, p_lat, p_lat, p_lat, p_lat, yb_lat, p_ctx, p_ctx, p_ctx, p_ctx, yb_ctx)


def _merge_out_kernel(ya_ref, yp_ref, yr_ref, ga_ref, gp_ref, gr_ref, wa_ref, wp_ref, wr_ref, wo_ref,
                      x_ref, mod_ref, o_ref, m_ref, *, n_col):
    j = pl.program_id(1)

    @pl.when(j < n_col)
    def _():
        acc = jax.nn.sigmoid(ga_ref[...].astype(f32)) * _dot(ya_ref[...], wa_ref[...].astype(bf16))
        acc = acc + jax.nn.sigmoid(gp_ref[...].astype(f32)) * _dot(yp_ref[...], wp_ref[...].astype(bf16))
        acc = acc + jax.nn.sigmoid(gr_ref[...].astype(f32)) * _dot(yr_ref[...], wr_ref[...].astype(bf16))
        m_ref[j] = acc.astype(m_ref.dtype)

    @pl.when(j >= n_col)
    def _():
        y = None
        for c in range(n_col):
            part = _dot(m_ref[c], wo_ref[c * TN_MERGE:(c + 1) * TN_MERGE, :].astype(bf16))
            y = part if y is None else y + part
        o_ref[...] = x_ref[...] + mod_ref[0, 2:3, :] * y


def _merge_out(ya, yp, yr, p, w_a, w_p, w_r, w_out, x, mod_l, *, layer, tm, row_block0, n_tiles, mod_sel):
    n_col = D_MODEL // TN_MERGE
    gate0 = COL_GATES // TN_MERGE

    def first(j):
        return jnp.minimum(j, n_col - 1)

    def second(j):
        return jnp.maximum(j - n_col, 0)

    def gate_spec(br):
        return pl.BlockSpec((tm, TN_MERGE), lambda i, j: (i, gate0 + br * n_col + first(j)))

    return pl.pallas_call(
        functools.partial(_merge_out_kernel, n_col=n_col),
        grid=(n_tiles, 2 * n_col),
        in_specs=[
            pl.BlockSpec((tm, ATTN_Q_WIDTH), lambda i, j: (i, 0)),
            pl.BlockSpec((tm, POOL_WIDTH), lambda i, j: (i, 0)),
            pl.BlockSpec((tm, RET_V_WIDTH), lambda i, j: (i, 0)),
            gate_spec(0), gate_spec(1), gate_spec(2),
            pl.BlockSpec((None, ATTN_Q_WIDTH, TN_MERGE), lambda i, j: (layer, 0, first(j))),
            pl.BlockSpec((None, POOL_WIDTH, TN_MERGE), lambda i, j: (layer, 0, first(j))),
            pl.BlockSpec((None, RET_V_WIDTH, TN_MERGE), lambda i, j: (layer, 0, first(j))),
            pl.BlockSpec((None, D_MODEL, TN_MERGE), lambda i, j: (layer, 0, second(j))),
            pl.BlockSpec((tm, TN_MERGE), lambda i, j: (row_block0 + i, second(j))),
            pl.BlockSpec((1, N_MOD, TN_MERGE), lambda i, j: (mod_sel(i), 0, second(j))),
        ],
        out_specs=pl.BlockSpec((tm, TN_MERGE), lambda i, j: (row_block0 + i, second(j))),
        out_shape=jax.ShapeDtypeStruct(x.shape, f32),
        scratch_shapes=[pltpu.VMEM((n_col, tm, TN_MERGE), bf16)],
        input_output_aliases={10: 0},
        compiler_params=_cparams(("parallel", "arbitrary")),
        name="merge_out",
    )(ya, yp, yr, p, p, p, w_a, w_p, w_r, w_out, x, mod_l)


ROUTE_LANES = V7X_LANES
LANE_E1, LANE_E2, LANE_R1, LANE_R2, LANE_W1, LANE_W2 = 0, 1, 2, 3, 4, 5
PLAN_ROWS = 8
HI16 = 0xFFFF0000


def _pack_bf16_pairs(v):
    n = v.shape[1] // 2
    bits = lax.bitcast_convert_type(v.astype(bf16).astype(f32), u32)
    return (bits[:, :n] >> 16) | (bits[:, n:] & jnp.uint32(HI16))


def _unpack_bf16_pairs(words):
    return (lax.bitcast_convert_type(words << 16, f32),
            lax.bitcast_convert_type(words & jnp.uint32(HI16), f32))


def _route_kernel(x_ref, mod_ref, g_ref, w_ref, b_ref, h_ref, info_ref, plan_ref, cnt_ref,
                  hf_ref, tri_ref, run_ref):
    tm = TM_ROUTE
    step = pl.program_id(0)

    @pl.when(step == 0)
    def _():
        run_ref[...] = jnp.zeros_like(run_ref)
        r = lax.broadcasted_iota(i32, (tm, tm), 0)
        c = lax.broadcasted_iota(i32, (tm, tm), 1)
        tri_ref[...] = jnp.where(c < r, 1.0, 0.0).astype(bf16)

    g = g_ref[...]
    shift = mod_ref[0, 3:4, :]
    scale = mod_ref[0, 4:5, :]

    def body(r, carry):
        rows = pl.ds(pl.multiple_of(r * ROW_CHUNK, ROW_CHUNK), ROW_CHUNK)
        hf_ref[rows, :] = _norm_mod_rows(x_ref[rows, :], g, shift, scale).astype(bf16)
        return carry

    lax.fori_loop(0, tm // ROW_CHUNK, body, 0)

    for c in range(tm // ROW_CHUNK):
        words = _pack_bf16_pairs(hf_ref[c * ROW_CHUNK:(c + 1) * ROW_CHUNK, :])
        for s in range(XS_SUB):
            h_ref[pl.ds(c * ROW_CHUNK * XS_SUB + s, ROW_CHUNK, stride=XS_SUB), :] = (
                words[:, s * V7X_LANES:(s + 1) * V7X_LANES])

    logits = _dot(hf_ref[...], w_ref[...].astype(bf16)) + b_ref[...]
    lane = lax.broadcasted_iota(i32, (tm, ROUTE_LANES), 1)
    neg = jnp.float32(-jnp.inf)
    lane_f = lane.astype(f32)

    def first_argmax(vals):
        top = jnp.max(vals, axis=-1, keepdims=True)
        idx = jnp.min(jnp.where(vals == top, lane_f, float(ROUTE_LANES)), axis=-1, keepdims=True)
        return top, idx.astype(i32)

    g_logits = jnp.where(lane < MOE_GROUPS, logits, neg)
    g_top, grp = first_argmax(g_logits)
    p_grp = 1.0 / jnp.sum(jnp.exp(g_logits - g_top), axis=-1, keepdims=True)

    e_lane = lane - MOE_GROUPS
    in_grp = (e_lane >= 0) & (e_lane < MOE_EXPERTS) & (lax.shift_right_arithmetic(e_lane, 3) == grp)
    e_logits = jnp.where(in_grp, logits, neg)
    top1, idx1 = first_argmax(e_logits)
    e_logits2 = jnp.where(lane == idx1, neg, e_logits)
    top2, idx2 = first_argmax(e_logits2)
    a = jnp.exp(top2 - top1)
    w1 = (1.0 / (1.0 + a)) * p_grp
    w2 = (a / (1.0 + a)) * p_grp
    e1 = idx1 - MOE_GROUPS
    e2 = idx2 - MOE_GROUPS

    hot1 = lane == e1
    hot2 = lane == e2
    o1 = jnp.where(hot1, 1.0, 0.0)
    o2 = jnp.where(hot2, 1.0, 0.0)
    tri = tri_ref[...]
    before1 = _dot(tri, o1.astype(bf16))
    before2 = _dot(tri, o2.astype(bf16))
    c1 = jnp.sum(o1, axis=0, keepdims=True)
    c2 = jnp.sum(o2, axis=0, keepdims=True)
    run = run_ref[...]
    rank1 = jnp.sum(jnp.where(hot1, before1 + run, 0.0), axis=-1, keepdims=True)
    rank2 = jnp.sum(jnp.where(hot2, before2 + run + c1, 0.0), axis=-1, keepdims=True)
    run = run + c1 + c2
    run_ref[...] = run
    cnt_ref[...] = jnp.broadcast_to(run, cnt_ref.shape)

    info = jnp.where(lane == LANE_E1, e1.astype(f32), 0.0)
    info = jnp.where(lane == LANE_E2, e2.astype(f32), info)
    info = jnp.where(lane == LANE_W1, w1, info)
    info = jnp.where(lane == LANE_W2, w2, info)
    info = jnp.where(lane == LANE_R1, rank1, info)
    info = jnp.where(lane == LANE_R2, rank2, info)
    info_ref[...] = info
    plan_ref[0] = jnp.transpose(info)[0:PLAN_ROWS, :].astype(i32)


def _route(x, mod_l, g2, w_route, b_route, *, n_tiles, mod_sel):
    tm = TM_ROUTE
    return pl.pallas_call(
        _route_kernel,
        grid=(n_tiles,),
        in_specs=[
            pl.BlockSpec((tm, D_MODEL), lambda i: (i, 0)),
            pl.BlockSpec((1, N_MOD, D_MODEL), lambda i: (mod_sel(i), 0, 0)),
            pl.BlockSpec((1, D_MODEL), lambda i: (0, 0)),
            pl.BlockSpec((D_MODEL, ROUTE_LANES), lambda i: (0, 0)),
            pl.BlockSpec((1, ROUTE_LANES), lambda i: (0, 0)),
        ],
        out_specs=[
            pl.BlockSpec((tm * XS_SUB, V7X_LANES), lambda i: (i, 0)),
            pl.BlockSpec((tm, ROUTE_LANES), lambda i: (i, 0)),
            pl.BlockSpec((1, PLAN_ROWS, tm), lambda i: (i, 0, 0)),
            pl.BlockSpec((8, ROUTE_LANES), lambda i: (0, 0)),
        ],
        out_shape=[
            jax.ShapeDtypeStruct((n_tiles * tm * XS_SUB, V7X_LANES), u32),
            jax.ShapeDtypeStruct((n_tiles * tm, ROUTE_LANES), f32),
            jax.ShapeDtypeStruct((n_tiles, PLAN_ROWS, tm), i32),
            jax.ShapeDtypeStruct((8, ROUTE_LANES), f32),
        ],
        scratch_shapes=[
            pltpu.VMEM((tm, D_MODEL), bf16),
            pltpu.VMEM((tm, tm), bf16),
            pltpu.VMEM((1, ROUTE_LANES), f32),
        ],
        compiler_params=_cparams(("arbitrary",)),
        name="moe_route",
    )(x, mod_l, g2, w_route, b_route)


def _dest_kernel(pend_ref, plan_ref, o_ref):
    plan = plan_ref[...]
    start = jnp.zeros_like(plan)
    for e in range(MOE_EXPERTS):
        start = jnp.where(plan == e, pend_ref[e], start)
    o_ref[...] = start + pltpu.roll(plan, plan.shape[0] - MOE_TOP_K, 0)


def _dest(pend0, plan):
    rows = plan.shape[0] * PLAN_ROWS
    return pl.pallas_call(
        _dest_kernel,
        grid_spec=pltpu.PrefetchScalarGridSpec(
            num_scalar_prefetch=1,
            grid=(1,),
            in_specs=[pl.BlockSpec((rows, TM_ROUTE), lambda i, pend: (0, 0))],
            out_specs=pl.BlockSpec((rows, TM_ROUTE), lambda i, pend: (0, 0)),
        ),
        out_shape=jax.ShapeDtypeStruct((rows, TM_ROUTE), i32),
        compiler_params=_cparams(("arbitrary",)),
        name="moe_dest",
    )(pend0, plan.reshape(rows, TM_ROUTE))


def _dispatch_kernel(pend_ref, dest_ref, h_ref, xs_ref, zero_ref, sem):
    tm = TM_ROUTE
    step = pl.program_id(0)

    blk_rows = MOE_BLK * XS_SUB

    def zero_copy(e):
        start = pl.multiple_of((pend_ref[e + 1] - MOE_BLK) * XS_SUB, blk_rows)
        return pltpu.make_async_copy(zero_ref, xs_ref.at[pl.ds(start, blk_rows)], sem)

    def tail_copy(blk):
        start = pl.multiple_of(blk * blk_rows, blk_rows)
        return pltpu.make_async_copy(zero_ref, xs_ref.at[pl.ds(start, blk_rows)], sem)

    @pl.when(step == 0)
    def _():
        zero_ref[...] = jnp.zeros_like(zero_ref)
        first_unused = pend_ref[MOE_EXPERTS] // MOE_BLK
        for phase in range(2):
            def body(e, carry):
                @pl.when(pend_ref[e + 1] > pend_ref[e])
                def _():
                    if phase == 0:
                        zero_copy(e).start()
                    else:
                        zero_copy(e).wait()
                return carry
            lax.fori_loop(0, MOE_EXPERTS, body, 0)

            def tail(blk, carry):
                if phase == 0:
                    tail_copy(blk).start()
                else:
                    tail_copy(blk).wait()
                return carry
            lax.fori_loop(first_unused, MOE_ROWS // MOE_BLK, tail, 0)

    def row_copy(r, k):
        d = dest_ref[k, r]
        src = h_ref.at[pl.ds(pl.multiple_of(r * XS_SUB, XS_SUB), XS_SUB)]
        return pltpu.make_async_copy(src, xs_ref.at[pl.ds(pl.multiple_of(d * XS_SUB, XS_SUB), XS_SUB)], sem)

    def start(r, carry):
        row_copy(r, 0).start(priority=0)
        row_copy(r, 1).start(priority=1)
        return carry

    lax.fori_loop(0, tm, start, 0)
    for k in range(MOE_TOP_K):
        pltpu.make_async_copy(h_ref, xs_ref.at[pl.ds(0, tm * XS_SUB)], sem).wait()


def _dispatch(pend0, dest, h_packed, *, n_tiles):
    tm = TM_ROUTE
    return pl.pallas_call(
        _dispatch_kernel,
        grid_spec=pltpu.PrefetchScalarGridSpec(
            num_scalar_prefetch=1,
            grid=(n_tiles,),
            in_specs=[
                pl.BlockSpec((PLAN_ROWS, tm), lambda i, pend: (i, 0), memory_space=pltpu.SMEM),
                pl.BlockSpec((tm * XS_SUB, V7X_LANES), lambda i, pend: (i, 0)),
            ],
            out_specs=pl.BlockSpec(memory_space=pl.ANY),
            scratch_shapes=[pltpu.VMEM((MOE_BLK * XS_SUB, V7X_LANES), u32), pltpu.SemaphoreType.DMA(())],
        ),
        out_shape=jax.ShapeDtypeStruct((MOE_ROWS * XS_SUB, V7X_LANES), u32),
        compiler_params=_cparams(("arbitrary",)),
        name="moe_dispatch",
    )(pend0, dest, h_packed)


def _expert_kernel(be_ref, nv_ref, first_ref, slot_ref, next_ref, xs_ref, wg_hbm, wu_hbm, wd_hbm, ys_ref,
                   wg_buf, wu_buf, wd_buf, wg16, wu16, wd16, sems, *, layer):
    i = pl.program_id(0)

    def weight_copies(e, s):
        return (pltpu.make_async_copy(wg_hbm.at[layer, e], wg_buf.at[s], sems.at[s, 0]),
                pltpu.make_async_copy(wu_hbm.at[layer, e], wu_buf.at[s], sems.at[s, 1]),
                pltpu.make_async_copy(wd_hbm.at[layer, e], wd_buf.at[s], sems.at[s, 2]))

    def swiglu_block(wg, wu, wd):
        parts = [_unpack_bf16_pairs(xs_ref[pl.ds(t, MOE_BLK, stride=XS_SUB), :]) for t in range(XS_SUB)]
        x = jnp.concatenate([p[0] for p in parts] + [p[1] for p in parts], axis=1).astype(bf16)
        gate = _dot(x, wg)
        up = _dot(x, wu)
        h = gate * jax.nn.sigmoid(gate) * up
        words = _pack_bf16_pairs(_dot(h.astype(bf16), wd))
        for t in range(XS_SUB):
            ys_ref[pl.ds(t, MOE_BLK, stride=XS_SUB), :] = words[:, t * V7X_LANES:(t + 1) * V7X_LANES]

    @pl.when(i < nv_ref[0])
    def _():
        s = slot_ref[i]

        @pl.when(first_ref[i] == 1)
        def _():
            @pl.when(i == 0)
            def _():
                for cp in weight_copies(be_ref[0], 0):
                    cp.start()
            for cp in weight_copies(be_ref[i], s):
                cp.wait()

            @pl.when(next_ref[i] >= 0)
            def _():
                for cp in weight_copies(next_ref[i], 1 - s):
                    cp.start()

            wg, wu, wd = wg_buf[s].astype(bf16), wu_buf[s].astype(bf16), wd_buf[s].astype(bf16)
            wg16[...] = wg
            wu16[...] = wu
            wd16[...] = wd
            swiglu_block(wg, wu, wd)

        @pl.when(first_ref[i] == 0)
        def _():
            swiglu_block(wg16[...], wu16[...], wd16[...])

    @pl.when(i >= nv_ref[0])
    def _():
        ys_ref[...] = jnp.zeros_like(ys_ref)


def _experts(block_expert, n_valid, first, slot, nxt, xs, w_gate, w_up, w_down, *, layer):
    n_blocks = MOE_ROWS // MOE_BLK

    def row(i, be, nv, *_):
        return jnp.minimum(i, nv[0] - 1)

    return pl.pallas_call(
        functools.partial(_expert_kernel, layer=layer),
        grid_spec=pltpu.PrefetchScalarGridSpec(
            num_scalar_prefetch=5,
            grid=(n_blocks,),
            in_specs=[
                pl.BlockSpec((MOE_BLK * XS_SUB, V7X_LANES), lambda i, *pf: (row(i, *pf), 0)),
                pl.BlockSpec(memory_space=pl.ANY),
                pl.BlockSpec(memory_space=pl.ANY),
                pl.BlockSpec(memory_space=pl.ANY),
            ],
            out_specs=pl.BlockSpec((MOE_BLK * XS_SUB, V7X_LANES), lambda i, *pf: (i, 0)),
            scratch_shapes=[
                pltpu.VMEM((2, D_MODEL, MOE_D_FF), f32),
                pltpu.VMEM((2, D_MODEL, MOE_D_FF), f32),
                pltpu.VMEM((2, MOE_D_FF, D_MODEL), f32),
                pltpu.VMEM((D_MODEL, MOE_D_FF), bf16),
                pltpu.VMEM((D_MODEL, MOE_D_FF), bf16),
                pltpu.VMEM((MOE_D_FF, D_MODEL), bf16),
                pltpu.SemaphoreType.DMA((2, 3)),
            ],
        ),
        out_shape=jax.ShapeDtypeStruct((MOE_ROWS * XS_SUB, V7X_LANES), u32),
        compiler_params=_cparams(("arbitrary",)),
        name="moe_experts",
    )(block_expert, n_valid, first, slot, nxt, xs, w_gate, w_up, w_down)


def _combine_kernel(dest0_ref, destn_ref, x_ref, info_ref, mod_ref, modn_ref, gn_ref, ys_ref,
                    o_ref, hn_ref, buf_ref, sems, *, n_tiles, modulated):
    tm = TM_COMBINE
    i = pl.program_id(0)

    def start_tile(dest_ref, slot):
        def row_copy(r, k):
            d = dest_ref[k, r]
            src = ys_ref.at[pl.ds(pl.multiple_of(d * XS_SUB, XS_SUB), XS_SUB)]
            dst = buf_ref.at[slot, k, pl.ds(pl.multiple_of(r * XS_SUB, XS_SUB), XS_SUB)]
            return pltpu.make_async_copy(src, dst, sems.at[slot])

        def body(r, carry):
            row_copy(r, 0).start(priority=0)
            row_copy(r, 1).start(priority=1)
            return carry

        lax.fori_loop(0, tm, body, 0, unroll=4)

    @pl.when(i == 0)
    def _():
        start_tile(dest0_ref, 0)

    slot = i % 2

    @pl.when(i + 1 < n_tiles)
    def _():
        start_tile(destn_ref, 1 - slot)

    for k in range(MOE_TOP_K):
        pltpu.make_async_copy(ys_ref.at[pl.ds(0, tm * XS_SUB)], buf_ref.at[slot, k], sems.at[slot]).wait()

    info = info_ref[...]
    w1 = info[:, LANE_W1:LANE_W1 + 1]
    w2 = info[:, LANE_W2:LANE_W2 + 1]
    for t in range(XS_SUB):
        rows = pl.ds(t, tm, stride=XS_SUB)
        lo1, hi1 = _unpack_bf16_pairs(buf_ref[slot, 0, rows, :])
        lo2, hi2 = _unpack_bf16_pairs(buf_ref[slot, 1, rows, :])
        for half, (y1, y2) in enumerate(((lo1, lo2), (hi1, hi2))):
            cols = slice(half * PACKED + t * V7X_LANES, half * PACKED + (t + 1) * V7X_LANES)
            o_ref[:, cols] = x_ref[:, cols] + mod_ref[0, 5:6, cols] * (y1 * w1 + y2 * w2)

    g = gn_ref[...]
    for c in range(tm // ROW_CHUNK):
        rows = slice(c * ROW_CHUNK, (c + 1) * ROW_CHUNK)
        if modulated:
            y = _norm_mod_rows(o_ref[rows, :], g, modn_ref[0, 0:1, :], modn_ref[0, 1:2, :])
        else:
            xr = o_ref[rows, :]
            y = xr * lax.rsqrt(jnp.mean(xr * xr, axis=-1, keepdims=True) + NORM_EPS) * g
        hn_ref[rows, :] = y.astype(hn_ref.dtype)


def _combine(dest, x, info, mod_l, mod_next, g_next, ys, *, n_tiles, mod_sel):
    tm = TM_COMBINE
    per = TM_ROUTE // tm
    modulated = mod_next is not None
    if not modulated:
        mod_next = mod_l

    def dest_spec(tile):
        return pl.BlockSpec((PLAN_ROWS, tm), lambda i: (tile(i) // per, tile(i) % per), memory_space=pltpu.SMEM)

    return pl.pallas_call(
        functools.partial(_combine_kernel, n_tiles=n_tiles, modulated=modulated),
        grid=(n_tiles,),
        in_specs=[
            dest_spec(lambda i: 0),
            dest_spec(lambda i: jnp.minimum(i + 1, n_tiles - 1)),
            pl.BlockSpec((tm, D_MODEL), lambda i: (i, 0)),
            pl.BlockSpec((tm, ROUTE_LANES), lambda i: (i, 0)),
            pl.BlockSpec((1, N_MOD, D_MODEL), lambda i: (mod_sel(i), 0, 0)),
            pl.BlockSpec((1, N_MOD, D_MODEL), lambda i: (mod_sel(i), 0, 0)),
            pl.BlockSpec((1, D_MODEL), lambda i: (0, 0)),
            pl.BlockSpec(memory_space=pl.ANY),
        ],
        out_specs=[pl.BlockSpec((tm, D_MODEL), lambda i: (i, 0)),
                   pl.BlockSpec((tm, D_MODEL), lambda i: (i, 0))],
        scratch_shapes=[pltpu.VMEM((2, MOE_TOP_K, tm * XS_SUB, V7X_LANES), u32),
                        pltpu.SemaphoreType.DMA((2,))],
        out_shape=[jax.ShapeDtypeStruct(x.shape, f32),
                   jax.ShapeDtypeStruct((n_tiles * tm, D_MODEL), bf16 if modulated else f32)],
        input_output_aliases={2: 0},
        compiler_params=_cparams(("arbitrary",)),
        name="moe_combine",
    )(dest, dest, x, info, mod_l, mod_next, g_next, ys)


def _moe(x, mod_l, g2, w_route, b_route, w_gate, w_up, w_down, mod_next, g_next, *, layer, with_ctx):
    n_tok = N_TOK if with_ctx else N_LAT
    rt = n_tok // TM_ROUTE
    lat_rt = N_LAT // TM_ROUTE
    h_packed, info, plan, cnt = _route(
        x, mod_l, g2, w_route, b_route, n_tiles=rt,
        mod_sel=lambda i: jnp.where(i < lat_rt, i // (SEQ // TM_ROUTE), BATCH))

    counts = cnt[0, :MOE_EXPERTS].astype(i32)
    padded = (counts + MOE_BLK - 1) // MOE_BLK * MOE_BLK
    pend = jnp.cumsum(padded)
    pend0 = jnp.concatenate([jnp.zeros((1,), i32), pend]).astype(i32)
    n_blocks = MOE_ROWS // MOE_BLK
    blk_id = jnp.arange(n_blocks, dtype=i32)

    def expert_at(row):
        return jnp.minimum(jnp.sum((pend[None, :] <= row[:, None]).astype(i32), axis=1), MOE_EXPERTS - 1)

    block_expert = expert_at(blk_id * MOE_BLK)
    n_valid = (pend[-1:] // MOE_BLK).astype(i32)
    valid = blk_id < n_valid[0]
    changed = jnp.concatenate([jnp.ones((1,), bool), block_expert[1:] != block_expert[:-1]])
    first = (changed & valid).astype(i32)
    slot = ((jnp.cumsum(first) - 1) % 2).astype(i32)
    own_end = jnp.sum(jnp.where(jnp.arange(MOE_EXPERTS, dtype=i32)[None, :] == block_expert[:, None],
                                pend[None, :], 0), axis=1)
    nxt = jnp.where(own_end < pend[-1], expert_at(own_end), -1).astype(i32)

    dest = _dest(pend0, plan)
    xs = _dispatch(pend0, dest, h_packed, n_tiles=rt)
    ys = _experts(block_expert, n_valid, first, slot, nxt, xs, w_gate, w_up, w_down, layer=layer)
    ct = n_tok // TM_COMBINE
    lat_ct = N_LAT // TM_COMBINE
    return _combine(dest, x, info, mod_l, mod_next, g_next, ys, n_tiles=ct,
                    mod_sel=lambda i: jnp.where(i < lat_ct, i // (SEQ // TM_COMBINE), BATCH))


def _rope_tables():
    rows = SEQ // GRID_W
    row = jnp.repeat(jnp.arange(rows, dtype=f32), GRID_W)
    col = jnp.tile(jnp.arange(GRID_W, dtype=f32), rows)
    n_freq = HEAD_DIM // 4
    inv_freq = ROPE_BASE ** (-jnp.arange(n_freq, dtype=f32) / n_freq)
    ang = jnp.concatenate([row[:, None] * inv_freq, col[:, None] * inv_freq], axis=-1)
    cos, sin = jnp.cos(ang), jnp.sin(ang)
    return jnp.concatenate([cos, cos], axis=-1), jnp.concatenate([-sin, sin], axis=-1)


def kernel(x, c, ctx, c_ctx, w_mod, b_mod, norm1_g, norm2_g, w_in, attn_sink, pool_w, pool_scale, ret_decay_fwd, ret_decay_bwd, w_br_attn, w_br_pool, w_br_ret, w_out, w_route_group, b_route_group, w_route_expert, b_route_expert, w_expert_gate, w_expert_up, w_expert_down, final_norm_g):
    assert x.shape == (BATCH, SEQ, D_MODEL) and ctx.shape == (BATCH, CTX_LEN, D_MODEL)
    cos_full, sin_signed = _rope_tables()

    cond8 = jnp.concatenate([c, c_ctx[None, :], jnp.zeros((8 - BATCH - 1, D_MODEL), f32)], axis=0)
    mod = _modulation(cond8, w_mod, b_mod).reshape(DEPTH, 8, N_MOD, D_MODEL)

    lg_fwd = jax.nn.log_sigmoid(ret_decay_fwd.astype(f32))
    lg_bwd = jax.nn.log_sigmoid(ret_decay_bwd.astype(f32))
    pad = ROUTE_LANES - MOE_GROUPS - MOE_EXPERTS
    w_route = jnp.concatenate(
        [w_route_group, w_route_expert, jnp.zeros((DEPTH, D_MODEL, pad), f32)], axis=-1)
    b_route = jnp.concatenate([b_route_group, b_route_expert, jnp.zeros((DEPTH, pad), f32)], axis=-1)

    tok = jnp.concatenate([x.reshape(N_LAT, D_MODEL), ctx.reshape(N_CTX, D_MODEL)], axis=0)
    ctx_block0 = N_LAT // TM_CTX
    sel_ctx = lambda i: BATCH

    def sel_rows(tm):
        return lambda i: jnp.where(i < N_LAT // tm, i // (SEQ // tm), BATCH)

    h1 = _norm(tok, mod[0], norm1_g[0][None, :], n_tiles=N_TOK // TM_NORM, mod_sel=sel_rows(TM_NORM))
    for l in range(DEPTH):
        with_ctx = l < DEPTH - 1
        mod_l = mod[l]
        p_lat = _proj(h1, w_in, layer=l, tm=TM_PROJ, row_block0=0, n_tiles=N_LAT // TM_PROJ)
        p_ctx = _proj(h1, w_in, layer=l, tm=TM_CTX, row_block0=ctx_block0, n_tiles=1)

        ya_lat = _attn_lat(p_lat, p_ctx, attn_sink[l], cos_full, sin_signed)
        yp_lat = _pool(p_lat, pool_w[l], pool_scale[l][None, :], n_seq=BATCH, seq_len=SEQ)
        yr_lat, yr_ctx = _retention(p_lat, p_ctx, lg_fwd[l], lg_bwd[l])

        tok = _merge_out(ya_lat, yp_lat, yr_lat, p_lat, w_br_attn, w_br_pool, w_br_ret, w_out, tok, mod_l,
                         layer=l, tm=TM_MERGE, row_block0=0, n_tiles=N_LAT // TM_MERGE,
                         mod_sel=sel_rows(TM_MERGE))
        if with_ctx:
            ya_ctx = _attn_ctx(p_ctx, attn_sink[l])
            yp_ctx = _pool(p_ctx, pool_w[l], pool_scale[l][None, :], n_seq=BATCH, seq_len=CTX_LEN)
            tok = _merge_out(ya_ctx, yp_ctx, yr_ctx, p_ctx, w_br_attn, w_br_pool, w_br_ret, w_out, tok, mod_l,
                             layer=l, tm=TM_CTX, row_block0=ctx_block0, n_tiles=1, mod_sel=sel_ctx)

        mod_next = mod[l + 1] if with_ctx else None
        g_next = norm1_g[l + 1][None, :] if with_ctx else final_norm_g[None, :]
        tok, h1 = _moe(tok, mod_l, norm2_g[l][None, :], w_route[l], b_route[l][None, :],
                       w_expert_gate, w_expert_up, w_expert_down, mod_next, g_next, layer=l, with_ctx=with_ctx)

    return h1.reshape(BATCH, SEQ, D_MODEL)
```

```python
import functools

import jax
import jax.numpy as jnp
from jax import lax
from jax.experimental import pallas as pl
from jax.experimental.pallas import tpu as pltpu

f32 = jnp.float32
bf16 = jnp.bfloat16
i32 = jnp.int32
u32 = jnp.uint32

D_MODEL = 2048
BATCH = 2
SEQ = 4096
DEPTH = 4
GRID_W = 64
CTX_LEN = 256
NORM_EPS = 1e-6
N_MOD = 6
HEAD_DIM = 128
ATTN_HEADS = 8
ATTN_KV_HEADS = 2
ATTN_GROUP = ATTN_HEADS // ATTN_KV_HEADS
ATTN_BLOCK = 128
ROPE_BASE = 10000.0
LOG2_E = 1.4426950408889634
POOL_GROUPS = 4
POOL_GROUP_DIM = 128
POOL_WIDTH = POOL_GROUPS * POOL_GROUP_DIM
POOL_SPANS = (2, 4, 8, 16)
RET_HEADS = 4
RET_QK_DIM = 64
RET_V_DIM = 128
N_BRANCHES = 3
ATTN_Q_WIDTH = ATTN_HEADS * HEAD_DIM
ATTN_KV_WIDTH = ATTN_KV_HEADS * HEAD_DIM
RET_QK_WIDTH = RET_HEADS * RET_QK_DIM
RET_V_WIDTH = RET_HEADS * RET_V_DIM
IN_SIZES = (ATTN_Q_WIDTH, ATTN_KV_WIDTH, ATTN_KV_WIDTH, POOL_WIDTH,
            RET_QK_WIDTH, RET_QK_WIDTH, RET_V_WIDTH, RET_V_WIDTH, N_BRANCHES * D_MODEL)
IN_WIDTH = sum(IN_SIZES)
MOE_GROUPS = 4
MOE_EXPERTS_PER_GROUP = 8
MOE_EXPERTS = MOE_GROUPS * MOE_EXPERTS_PER_GROUP
MOE_TOP_K = 2
MOE_D_FF = 512

COL_QA = 0
COL_KA = COL_QA + ATTN_Q_WIDTH
COL_VA = COL_KA + ATTN_KV_WIDTH
COL_U = COL_VA + ATTN_KV_WIDTH
COL_QR = COL_U + POOL_WIDTH
COL_KR = COL_QR + RET_QK_WIDTH
COL_VR = COL_KR + RET_QK_WIDTH
COL_GR = COL_VR + RET_V_WIDTH
COL_GATES = COL_GR + RET_V_WIDTH

V7X_LANES = 128
V7X_VMEM_LIMIT_BYTES = 56 * 1024 * 1024

N_LAT = BATCH * SEQ
N_CTX = BATCH * CTX_LEN
N_TOK = N_LAT + N_CTX
TM_PROJ = 2048
TM_MERGE = 2048
TM_CTX = N_CTX
TM_NORM = 512
TN_PROJ = 512
TN_MERGE = 256
TN_MOD = 2048
ROW_CHUNK = 128
RET_CHUNK = 256
TM_ROUTE = 512
TM_COMBINE = 256
MOE_BLK = 256
MOE_ROWS = (N_TOK * MOE_TOP_K + MOE_EXPERTS * (MOE_BLK - 1)) // MOE_BLK * MOE_BLK
PACKED = D_MODEL // 2
XS_SUB = PACKED // V7X_LANES
YS_SUB = D_MODEL // V7X_LANES

assert CTX_LEN == RET_CHUNK and SEQ % RET_CHUNK == 0
assert IN_WIDTH % TN_PROJ == 0 and COL_GATES % TN_MERGE == 0


def _cparams(sem, vmem=V7X_VMEM_LIMIT_BYTES):
    return pltpu.CompilerParams(dimension_semantics=sem, vmem_limit_bytes=vmem)


def _dot(a, b):
    return jnp.dot(a, b, preferred_element_type=f32)


def _dot_nt(a, b):
    return lax.dot_general(a, b, (((1,), (1,)), ((), ())), preferred_element_type=f32)


def _sigmoid(x):
    return 0.5 * jnp.tanh(0.5 * x) + 0.5


def _dot_tn(a, b):
    return lax.dot_general(a, b, (((0,), (0,)), ((), ())), preferred_element_type=f32)


def _mod_kernel(cond_ref, w_ref, b_ref, o_ref):
    c = cond_ref[...]
    s = (c * jax.nn.sigmoid(c)).astype(bf16)
    o_ref[0] = _dot(s, w_ref[0].astype(bf16)) + b_ref[0]


def _modulation(cond8, w_mod, b_mod):
    n = N_MOD * D_MODEL
    return pl.pallas_call(
        _mod_kernel,
        grid=(DEPTH, n // TN_MOD),
        in_specs=[
            pl.BlockSpec((8, D_MODEL), lambda l, j: (0, 0)),
            pl.BlockSpec((1, D_MODEL, TN_MOD), lambda l, j: (l, 0, j)),
            pl.BlockSpec((1, 1, TN_MOD), lambda l, j: (l, 0, j)),
        ],
        out_specs=pl.BlockSpec((1, 8, TN_MOD), lambda l, j: (l, 0, j)),
        out_shape=jax.ShapeDtypeStruct((DEPTH, 8, n), f32),
        compiler_params=_cparams(("parallel", "parallel")),
        name="modulation",
    )(cond8, w_mod, b_mod.reshape(DEPTH, 1, n))


def _norm_mod_rows(x, g, shift, scale):
    ms = jnp.mean(x * x, axis=-1, keepdims=True)
    y = x * lax.rsqrt(ms + NORM_EPS) * g
    return y * (1.0 + scale) + shift


def _norm_kernel(x_ref, mod_ref, g_ref, o_ref, *, tm):
    g = g_ref[...]
    shift = mod_ref[0, 0:1, :]
    scale = mod_ref[0, 1:2, :]

    def body(r, carry):
        rows = pl.ds(pl.multiple_of(r * ROW_CHUNK, ROW_CHUNK), ROW_CHUNK)
        o_ref[rows, :] = _norm_mod_rows(x_ref[rows, :], g, shift, scale).astype(bf16)
        return carry

    lax.fori_loop(0, tm // ROW_CHUNK, body, 0)


def _norm(x, mod_l, g1, *, n_tiles, mod_sel):
    tm = TM_NORM
    return pl.pallas_call(
        functools.partial(_norm_kernel, tm=tm),
        grid=(n_tiles,),
        in_specs=[
            pl.BlockSpec((tm, D_MODEL), lambda i: (i, 0)),
            pl.BlockSpec((1, N_MOD, D_MODEL), lambda i: (mod_sel(i), 0, 0)),
            pl.BlockSpec((1, D_MODEL), lambda i: (0, 0)),
        ],
        out_specs=pl.BlockSpec((tm, D_MODEL), lambda i: (i, 0)),
        out_shape=jax.ShapeDtypeStruct((n_tiles * tm, D_MODEL), bf16),
        compiler_params=_cparams(("parallel",)),
        name="norm1",
    )(x, mod_l, g1)


def _proj_kernel(h_ref, w_ref, o_ref):
    o_ref[...] = _dot(h_ref[...], w_ref[...].astype(bf16)).astype(o_ref.dtype)


def _proj(h, w_in, *, layer, tm, row_block0, n_tiles):
    return pl.pallas_call(
        _proj_kernel,
        grid=(n_tiles, IN_WIDTH // TN_PROJ),
        in_specs=[
            pl.BlockSpec((tm, D_MODEL), lambda i, j: (row_block0 + i, 0)),
            pl.BlockSpec((None, D_MODEL, TN_PROJ), lambda i, j: (layer, 0, j)),
        ],
        out_specs=pl.BlockSpec((tm, TN_PROJ), lambda i, j: (i, j)),
        out_shape=jax.ShapeDtypeStruct((n_tiles * tm, IN_WIDTH), bf16),
        compiler_params=_cparams(("parallel", "arbitrary")),
        name="in_proj",
    )(h, w_in)


def _rope(t, cos_full, sin_signed):
    return t * cos_full + pltpu.roll(t, HEAD_DIM // 2, 1) * sin_signed


def _sink_column(sink_ref, g, rows_per_head):
    n = ATTN_GROUP * rows_per_head
    row = lax.broadcasted_iota(i32, (n, 1), 0)
    col = jnp.full((n, 1), sink_ref[g * ATTN_GROUP + ATTN_GROUP - 1], f32)
    for j in range(ATTN_GROUP - 2, -1, -1):
        col = jnp.where(row < (j + 1) * rows_per_head, sink_ref[g * ATTN_GROUP + j], col)
    return col


def _softmax_pv(scores, values, sink_col):
    def lane_chunks(arrs):
        return [a[:, c * V7X_LANES:(c + 1) * V7X_LANES] for a in arrs for c in range(a.shape[1] // V7X_LANES)]

    folded = functools.reduce(jnp.maximum, lane_chunks(scores))
    m = jnp.maximum(sink_col, jnp.max(folded, axis=-1, keepdims=True))
    es = [jnp.exp2(s - m) for s in scores]
    total = functools.reduce(lambda a, b: a + b, lane_chunks(es))
    denom = jnp.exp2(sink_col - m) + jnp.sum(total, axis=-1, keepdims=True)
    out = None
    for e, v in zip(es, values):
        o = _dot(e.astype(bf16), v)
        out = o if out is None else out + o
    return out * (1.0 / denom)


def _attn_lat_kernel(sink_ref, q_ref, kp_ref, km_ref, kn_ref, vp_ref, vm_ref, vn_ref, kx_ref, vx_ref,
                     cm_ref, sm_ref, cp_ref, sp_ref, cn_ref, sn_ref, o_ref, *, n_steps):
    step = pl.program_id(1)
    blk = ATTN_BLOCK
    scale = HEAD_DIM ** -0.5 * LOG2_E
    cm, sm = cm_ref[...], sm_ref[...]
    rows = ATTN_GROUP * blk
    qi = lax.broadcasted_iota(i32, (rows, blk), 0) % blk
    kj = lax.broadcasted_iota(i32, (rows, blk), 1)
    neg = jnp.float32(-jnp.inf)
    keep_prev = [(kj >= qi) & (step > 0), kj >= qi]
    keep_next = [kj <= qi, (kj <= qi) & (step < n_steps - 1)]

    for g in range(ATTN_KV_HEADS):
        kv = slice(g * HEAD_DIM, (g + 1) * HEAD_DIM)
        k_mid = _rope(km_ref[:, kv].astype(f32), cm, sm).astype(bf16)
        k_blocks = [_rope(kp_ref[:, kv].astype(f32), cp_ref[...], sp_ref[...]).astype(bf16),
                    k_mid[:blk], k_mid[blk:],
                    _rope(kn_ref[:, kv].astype(f32), cn_ref[...], sn_ref[...]).astype(bf16)]
        v_blocks = [vp_ref[:, kv], vm_ref[:blk, kv], vm_ref[blk:, kv], vn_ref[:, kv]]
        k_ctx, v_ctx = kx_ref[:, kv], vx_ref[:, kv]
        sink_col = _sink_column(sink_ref, g, blk) * LOG2_E
        for half in range(2):
            qrows = slice(half * blk, (half + 1) * blk)
            cq, sq = cm[qrows], sm[qrows]
            qs = jnp.concatenate(
                [(_rope(q_ref[qrows, (g * ATTN_GROUP + j) * HEAD_DIM:(g * ATTN_GROUP + j + 1) * HEAD_DIM]
                        .astype(f32), cq, sq) * scale).astype(bf16) for j in range(ATTN_GROUP)],
                axis=0)
            s_prev = jnp.where(keep_prev[half], _dot_nt(qs, k_blocks[half]), neg)
            s_cur = _dot_nt(qs, k_blocks[half + 1])
            s_next = jnp.where(keep_next[half], _dot_nt(qs, k_blocks[half + 2]), neg)
            s_ctx = _dot_nt(qs, k_ctx)
            out = _softmax_pv([s_prev, s_cur, s_next, s_ctx],
                              [v_blocks[half], v_blocks[half + 1], v_blocks[half + 2], v_ctx], sink_col)
            for j in range(ATTN_GROUP):
                h = g * ATTN_GROUP + j
                o_ref[qrows, h * HEAD_DIM:(h + 1) * HEAD_DIM] = out[j * blk:(j + 1) * blk, :].astype(o_ref.dtype)


def _attn_lat(p_lat, p_ctx, sink, cos_full, sin_signed):
    nb = SEQ // ATTN_BLOCK
    blk = ATTN_BLOCK
    pair = 2 * blk
    n_steps = nb // 2
    kcol = COL_KA // ATTN_KV_WIDTH
    vcol = COL_VA // ATTN_KV_WIDTH

    def edge_block(s, shift):
        return jnp.clip(2 * s + shift, 0, nb - 1)

    def edge_spec(col, shift):
        return pl.BlockSpec((blk, ATTN_KV_WIDTH), lambda b, s: (b * nb + edge_block(s, shift), col))

    def mid_spec(col):
        return pl.BlockSpec((pair, ATTN_KV_WIDTH), lambda b, s: (b * n_steps + s, col))

    def edge_tab(shift):
        return pl.BlockSpec((blk, HEAD_DIM), lambda b, s: (edge_block(s, shift), 0))

    mid_tab = pl.BlockSpec((pair, HEAD_DIM), lambda b, s: (s, 0))
    return pl.pallas_call(
        functools.partial(_attn_lat_kernel, n_steps=n_steps),
        grid=(BATCH, n_steps),
        in_specs=[
            pl.BlockSpec(memory_space=pltpu.SMEM),
            pl.BlockSpec((pair, ATTN_Q_WIDTH), lambda b, s: (b * n_steps + s, 0)),
            edge_spec(kcol, -1), mid_spec(kcol), edge_spec(kcol, 2),
            edge_spec(vcol, -1), mid_spec(vcol), edge_spec(vcol, 2),
            pl.BlockSpec((CTX_LEN, ATTN_KV_WIDTH), lambda b, s: (b, kcol)),
            pl.BlockSpec((CTX_LEN, ATTN_KV_WIDTH), lambda b, s: (b, vcol)),
            mid_tab, mid_tab, edge_tab(-1), edge_tab(-1), edge_tab(2), edge_tab(2),
        ],
        out_specs=pl.BlockSpec((pair, ATTN_Q_WIDTH), lambda b, s: (b * n_steps + s, 0)),
        out_shape=jax.ShapeDtypeStruct((N_LAT, ATTN_Q_WIDTH), bf16),
        compiler_params=_cparams(("parallel", "arbitrary")),
        name="attn_lat",
    )(sink, p_lat, p_lat, p_lat, p_lat, p_lat, p_lat, p_lat, p_ctx, p_ctx,
      cos_full, sin_signed, cos_full, sin_signed, cos_full, sin_signed)


def _attn_ctx_kernel(sink_ref, q_ref, kx_ref, vx_ref, o_ref):
    g = pl.program_id(1)
    scale = HEAD_DIM ** -0.5 * LOG2_E
    q = q_ref[...].astype(f32)
    qs = jnp.concatenate(
        [(q[:, j * HEAD_DIM:(j + 1) * HEAD_DIM] * scale).astype(bf16) for j in range(ATTN_GROUP)], axis=0)
    s_ctx = _dot_nt(qs, kx_ref[...])
    out = _softmax_pv([s_ctx], [vx_ref[...]], _sink_column(sink_ref, g, CTX_LEN) * LOG2_E)
    for j in range(ATTN_GROUP):
        o_ref[:, j * HEAD_DIM:(j + 1) * HEAD_DIM] = out[j * CTX_LEN:(j + 1) * CTX_LEN, :].astype(o_ref.dtype)


def _attn_ctx(p_ctx, sink):
    gw = ATTN_GROUP * HEAD_DIM
    kcol = COL_KA // HEAD_DIM
    vcol = COL_VA // HEAD_DIM
    return pl.pallas_call(
        _attn_ctx_kernel,
        grid=(BATCH, ATTN_KV_HEADS),
        in_specs=[
            pl.BlockSpec(memory_space=pltpu.SMEM),
            pl.BlockSpec((CTX_LEN, gw), lambda b, g: (b, g)),
            pl.BlockSpec((CTX_LEN, HEAD_DIM), lambda b, g: (b, kcol + g)),
            pl.BlockSpec((CTX_LEN, HEAD_DIM), lambda b, g: (b, vcol + g)),
        ],
        out_specs=pl.BlockSpec((CTX_LEN, gw), lambda b, g: (b, g)),
        out_shape=jax.ShapeDtypeStruct((N_CTX, ATTN_Q_WIDTH), bf16),
        compiler_params=_cparams(("parallel", "parallel")),
        name="attn_ctx",
    )(sink, p_ctx, p_ctx, p_ctx)


POOL_HALO = 8
POOL_ROWS = 256


def _pool_kernel(u_ref, w_ref, s_ref, o_ref, pad_ref, *, seq_len):
    zeros = jnp.zeros((POOL_HALO, POOL_GROUP_DIM), f32)
    for gi in range(POOL_GROUPS):
        radius = POOL_SPANS[gi] // 2
        cols = slice(gi * POOL_GROUP_DIM, (gi + 1) * POOL_GROUP_DIM)
        pad_ref[0:POOL_HALO, :] = zeros
        pad_ref[POOL_HALO + seq_len:POOL_HALO + seq_len + POOL_HALO, :] = zeros
        pad_ref[POOL_HALO:POOL_HALO + seq_len, :] = u_ref[:, cols].astype(f32)
        w = w_ref[gi].astype(bf16)
        sc = s_ref[:, cols]
        for c in range(seq_len // POOL_ROWS):
            base = c * POOL_ROWS
            acc = pad_ref[POOL_HALO + base:POOL_HALO + base + POOL_ROWS, :]
            center = acc
            for d in range(1, radius + 1):
                acc = acc + pad_ref[POOL_HALO + base - d:POOL_HALO + base - d + POOL_ROWS, :]
                acc = acc + pad_ref[POOL_HALO + base + d:POOL_HALO + base + d + POOL_ROWS, :]
            t = lax.broadcasted_iota(i32, (POOL_ROWS, 1), 0) + base
            lo = jnp.maximum(t - radius, 0)
            hi = jnp.minimum(t + radius + 1, seq_len)
            count = (hi - lo).astype(f32)
            pooled = (acc / count - center).astype(bf16)
            o_ref[base:base + POOL_ROWS, cols] = (_dot(pooled, w) * sc).astype(o_ref.dtype)


def _pool(p, pool_w, pool_scale, *, n_seq, seq_len):
    ucol = COL_U // POOL_WIDTH
    return pl.pallas_call(
        functools.partial(_pool_kernel, seq_len=seq_len),
        grid=(n_seq,),
        in_specs=[
            pl.BlockSpec((seq_len, POOL_WIDTH), lambda s: (s, ucol)),
            pl.BlockSpec((POOL_GROUPS, POOL_GROUP_DIM, POOL_GROUP_DIM), lambda s: (0, 0, 0)),
            pl.BlockSpec((1, POOL_WIDTH), lambda s: (0, 0)),
        ],
        out_specs=pl.BlockSpec((seq_len, POOL_WIDTH), lambda s: (s, 0)),
        out_shape=jax.ShapeDtypeStruct((n_seq * seq_len, POOL_WIDTH), bf16),
        scratch_shapes=[pltpu.VMEM((seq_len + 2 * POOL_HALO, POOL_GROUP_DIM), f32)],
        compiler_params=_cparams(("parallel",)),
        name="pool",
    )(p, pool_w, pool_scale)


def _ret_tables(lg_ref, intra_ref, qdec_ref, kdec_ref, *, reverse):
    c = RET_CHUNK
    i = lax.broadcasted_iota(i32, (c, c), 0).astype(f32)
    j = lax.broadcasted_iota(i32, (c, c), 1).astype(f32)
    pos = lax.broadcasted_iota(i32, (c, V7X_LANES), 0).astype(f32)
    for h in range(RET_HEADS):
        lg = lg_ref[h]
        if reverse:
            diff = j - i
            qd = c - pos
            kd = pos
        else:
            diff = i - j
            qd = pos + 1.0
            kd = c - 1.0 - pos
        intra_ref[h] = jnp.where(diff >= 0, jnp.exp(lg * jnp.maximum(diff, 0.0)), 0.0)
        qdec_ref[h] = jnp.exp(lg * qd)
        kdec_ref[h] = jnp.exp(lg * kd)


def _ret_chunk(q_ref, k_ref, v_ref, lg_ref, intra_ref, qdec_ref, kdec_ref, state_ref):
    outs = []
    k_scale = RET_QK_DIM ** -0.5
    for h in range(RET_HEADS):
        q = q_ref[:, h * RET_QK_DIM:(h + 1) * RET_QK_DIM].astype(bf16)
        k = k_ref[:, h * RET_QK_DIM:(h + 1) * RET_QK_DIM].astype(f32) * k_scale
        v = v_ref[:, h * RET_V_DIM:(h + 1) * RET_V_DIM].astype(bf16)
        scores = _dot_nt(q, k.astype(bf16)) * intra_ref[h]
        state = state_ref[h]
        out = _dot(scores.astype(bf16), v) + _dot(q, state.astype(bf16)) * qdec_ref[h]
        kd = (k * kdec_ref[h][:, :RET_QK_DIM]).astype(bf16)
        chunk_decay = jnp.exp(jnp.full((1, RET_V_DIM), lg_ref[h] * RET_CHUNK, f32))
        state_ref[h] = state * chunk_decay + _dot_tn(kd, v)
        outs.append(out)
    return outs


def _ret_bwd_kernel(lg_ref, ql_ref, kl_ref, vl_ref, qc_ref, kc_ref, vc_ref, ol_ref, oc_ref,
                    intra_ref, qdec_ref, kdec_ref, state_ref):
    t = pl.program_id(1)

    @pl.when((pl.program_id(0) == 0) & (t == 0))
    def _():
        _ret_tables(lg_ref, intra_ref, qdec_ref, kdec_ref, reverse=True)

    @pl.when(t == 0)
    def _():
        state_ref[...] = jnp.zeros_like(state_ref)
        outs = _ret_chunk(qc_ref, kc_ref, vc_ref, lg_ref, intra_ref, qdec_ref, kdec_ref, state_ref)
        for h in range(RET_HEADS):
            oc_ref[:, h * RET_V_DIM:(h + 1) * RET_V_DIM] = outs[h]

    @pl.when(t > 0)
    def _():
        outs = _ret_chunk(ql_ref, kl_ref, vl_ref, lg_ref, intra_ref, qdec_ref, kdec_ref, state_ref)
        for h in range(RET_HEADS):
            ol_ref[:, h * RET_V_DIM:(h + 1) * RET_V_DIM] = outs[h]


def _ret_finish(outs, yb_ref, g_ref, o_ref):
    for h in range(RET_HEADS):
        cols = slice(h * RET_V_DIM, (h + 1) * RET_V_DIM)
        y = outs[h] + yb_ref[:, cols]
        mu = jnp.mean(y, axis=-1, keepdims=True)
        yc = y - mu
        var = jnp.mean(yc * yc, axis=-1, keepdims=True)
        yn = yc * lax.rsqrt(var + NORM_EPS)
        g = g_ref[:, cols].astype(f32)
        o_ref[:, cols] = (g * jax.nn.sigmoid(g) * yn).astype(o_ref.dtype)


def _ret_fwd_kernel(lg_ref, ql_ref, kl_ref, vl_ref, gl_ref, ybl_ref, qc_ref, kc_ref, vc_ref, gc_ref, ybc_ref,
                    ol_ref, oc_ref, intra_ref, qdec_ref, kdec_ref, state_ref):
    t = pl.program_id(1)

    @pl.when((pl.program_id(0) == 0) & (t == 0))
    def _():
        _ret_tables(lg_ref, intra_ref, qdec_ref, kdec_ref, reverse=False)

    @pl.when(t == 0)
    def _():
        state_ref[...] = jnp.zeros_like(state_ref)
        outs = _ret_chunk(qc_ref, kc_ref, vc_ref, lg_ref, intra_ref, qdec_ref, kdec_ref, state_ref)
        _ret_finish(outs, ybc_ref, gc_ref, oc_ref)

    @pl.when(t > 0)
    def _():
        outs = _ret_chunk(ql_ref, kl_ref, vl_ref, lg_ref, intra_ref, qdec_ref, kdec_ref, state_ref)
        _ret_finish(outs, ybl_ref, gl_ref, ol_ref)


def _ret_scratch():
    c = RET_CHUNK
    return [
        pltpu.VMEM((RET_HEADS, c, c), f32),
        pltpu.VMEM((RET_HEADS, c, V7X_LANES), f32),
        pltpu.VMEM((RET_HEADS, c, V7X_LANES), f32),
        pltpu.VMEM((RET_HEADS, RET_QK_DIM, RET_V_DIM), f32),
    ]


def _retention(p_lat, p_ctx, lg_fwd, lg_bwd):
    c = RET_CHUNK
    nc = SEQ // c
    qcol, kcol = COL_QR // RET_QK_WIDTH, COL_KR // RET_QK_WIDTH
    vcol, gcol = COL_VR // RET_V_WIDTH, COL_GR // RET_V_WIDTH

    def lat_row(reverse):
        def f(b, t):
            s = jnp.maximum(t, 1)
            return b * nc + ((nc - s) if reverse else (s - 1))
        return f

    def specs(reverse, with_gate):
        row = lat_row(reverse)
        lat = [pl.BlockSpec((c, RET_QK_WIDTH), lambda b, t: (row(b, t), qcol)),
               pl.BlockSpec((c, RET_QK_WIDTH), lambda b, t: (row(b, t), kcol)),
               pl.BlockSpec((c, RET_V_WIDTH), lambda b, t: (row(b, t), vcol))]
        ctx = [pl.BlockSpec((c, RET_QK_WIDTH), lambda b, t: (b, qcol)),
               pl.BlockSpec((c, RET_QK_WIDTH), lambda b, t: (b, kcol)),
               pl.BlockSpec((c, RET_V_WIDTH), lambda b, t: (b, vcol))]
        if with_gate:
            lat += [pl.BlockSpec((c, RET_V_WIDTH), lambda b, t: (row(b, t), gcol)),
                    pl.BlockSpec((c, RET_V_WIDTH), lambda b, t: (row(b, t), 0))]
            ctx += [pl.BlockSpec((c, RET_V_WIDTH), lambda b, t: (b, gcol)),
                    pl.BlockSpec((c, RET_V_WIDTH), lambda b, t: (b, 0))]
        outs = [pl.BlockSpec((c, RET_V_WIDTH), lambda b, t: (row(b, t), 0)),
                pl.BlockSpec((c, RET_V_WIDTH), lambda b, t: (b, 0))]
        return [pl.BlockSpec(memory_space=pltpu.SMEM)] + lat + ctx, outs

    in_specs, out_specs = specs(True, False)
    yb_lat, yb_ctx = pl.pallas_call(
        _ret_bwd_kernel,
        grid=(BATCH, nc + 1),
        in_specs=in_specs,
        out_specs=out_specs,
        out_shape=[jax.ShapeDtypeStruct((N_LAT, RET_V_WIDTH), f32),
                   jax.ShapeDtypeStruct((N_CTX, RET_V_WIDTH), f32)],
        scratch_shapes=_ret_scratch(),
        compiler_params=_cparams(("arbitrary", "arbitrary")),
        name="retention_bwd",
    )(lg_bwd, p_lat, p_lat, p_lat, p_ctx, p_ctx, p_ctx)

    in_specs, out_specs = specs(False, True)
    return pl.pallas_call(
        _ret_fwd_kernel,
        grid=(BATCH, nc + 1),
        in_specs=in_specs,
        out_specs=out_specs,
        out_shape=[jax.ShapeDtypeStruct((N_LAT, RET_V_WIDTH), bf16),
                   jax.ShapeDtypeStruct((N_CTX, RET_V_WIDTH), bf16)],
        scratch_shapes=_ret_scratch(),
        compiler_params=_cparams(("arbitrary", "arbitrary")),
        name="retention_fwd",
    )(lg_fwd, p_lat, p_lat, p_lat, p_lat, yb_lat, p_ctx, p_ctx, p_ctx, p_ctx, yb_ctx)


def _merge_out_kernel(ya_ref, yp_ref, yr_ref, ga_ref, gp_ref, gr_ref, wa_ref, wp_ref, wr_ref, wo_ref,
                      x_ref, mod_ref, o_ref, m_ref, *, n_col):
    j = pl.program_id(1)

    @pl.when(j < n_col)
    def _():
        acc = _sigmoid(ga_ref[...].astype(f32)) * _dot(ya_ref[...], wa_ref[...].astype(bf16))
        acc = acc + _sigmoid(gp_ref[...].astype(f32)) * _dot(yp_ref[...], wp_ref[...].astype(bf16))
        acc = acc + _sigmoid(gr_ref[...].astype(f32)) * _dot(yr_ref[...], wr_ref[...].astype(bf16))
        m_ref[j] = acc.astype(m_ref.dtype)

    @pl.when(j >= n_col)
    def _():
        y = None
        for c in range(n_col):
            part = _dot(m_ref[c], wo_ref[c * TN_MERGE:(c + 1) * TN_MERGE, :].astype(bf16))
            y = part if y is None else y + part
        o_ref[...] = x_ref[...] + mod_ref[0, 2:3, :] * y


def _merge_out(ya, yp, yr, p, w_a, w_p, w_r, w_out, x, mod_l, *, layer, tm, row_block0, n_tiles, mod_sel):
    n_col = D_MODEL // TN_MERGE
    gate0 = COL_GATES // TN_MERGE

    def first(j):
        return jnp.minimum(j, n_col - 1)

    def second(j):
        return jnp.maximum(j - n_col, 0)

    def gate_spec(br):
        return pl.BlockSpec((tm, TN_MERGE), lambda i, j: (i, gate0 + br * n_col + first(j)))

    return pl.pallas_call(
        functools.partial(_merge_out_kernel, n_col=n_col),
        grid=(n_tiles, 2 * n_col),
        in_specs=[
            pl.BlockSpec((tm, ATTN_Q_WIDTH), lambda i, j: (i, 0)),
            pl.BlockSpec((tm, POOL_WIDTH), lambda i, j: (i, 0)),
            pl.BlockSpec((tm, RET_V_WIDTH), lambda i, j: (i, 0)),
            gate_spec(0), gate_spec(1), gate_spec(2),
            pl.BlockSpec((None, ATTN_Q_WIDTH, TN_MERGE), lambda i, j: (layer, 0, first(j))),
            pl.BlockSpec((None, POOL_WIDTH, TN_MERGE), lambda i, j: (layer, 0, first(j))),
            pl.BlockSpec((None, RET_V_WIDTH, TN_MERGE), lambda i, j: (layer, 0, first(j))),
            pl.BlockSpec((None, D_MODEL, TN_MERGE), lambda i, j: (layer, 0, second(j))),
            pl.BlockSpec((tm, TN_MERGE), lambda i, j: (row_block0 + i, second(j))),
            pl.BlockSpec((1, N_MOD, TN_MERGE), lambda i, j: (mod_sel(i), 0, second(j))),
        ],
        out_specs=pl.BlockSpec((tm, TN_MERGE), lambda i, j: (row_block0 + i, second(j))),
        out_shape=jax.ShapeDtypeStruct(x.shape, f32),
        scratch_shapes=[pltpu.VMEM((n_col, tm, TN_MERGE), bf16)],
        input_output_aliases={10: 0},
        compiler_params=_cparams(("parallel", "arbitrary")),
        name="merge_out",
    )(ya, yp, yr, p, p, p, w_a, w_p, w_r, w_out, x, mod_l)


ROUTE_LANES = V7X_LANES
LANE_E1, LANE_E2, LANE_R1, LANE_R2, LANE_W1, LANE_W2 = 0, 1, 2, 3, 4, 5
PLAN_ROWS = 8
HI16 = 0xFFFF0000


def _pack_bf16_pairs(v):
    n = v.shape[1] // 2
    bits = lax.bitcast_convert_type(v.astype(bf16).astype(f32), u32)
    return (bits[:, :n] >> 16) | (bits[:, n:] & jnp.uint32(HI16))


def _unpack_bf16_pairs(words):
    return (lax.bitcast_convert_type(words << 16, f32),
            lax.bitcast_convert_type(words & jnp.uint32(HI16), f32))


def _route_kernel(x_ref, mod_ref, g_ref, w_ref, b_ref, h_ref, info_ref, plan_ref, cnt_ref,
                  hf_ref, tri_ref, run_ref):
    tm = TM_ROUTE
    step = pl.program_id(0)

    @pl.when(step == 0)
    def _():
        run_ref[...] = jnp.zeros_like(run_ref)
        r = lax.broadcasted_iota(i32, (tm, tm), 0)
        c = lax.broadcasted_iota(i32, (tm, tm), 1)
        tri_ref[...] = jnp.where(c < r, 1.0, 0.0).astype(bf16)

    g = g_ref[...]
    shift = mod_ref[0, 3:4, :]
    scale = mod_ref[0, 4:5, :]

    def body(r, carry):
        rows = pl.ds(pl.multiple_of(r * ROW_CHUNK, ROW_CHUNK), ROW_CHUNK)
        hf_ref[rows, :] = _norm_mod_rows(x_ref[rows, :], g, shift, scale).astype(bf16)
        return carry

    lax.fori_loop(0, tm // ROW_CHUNK, body, 0)

    for c in range(tm // ROW_CHUNK):
        words = _pack_bf16_pairs(hf_ref[c * ROW_CHUNK:(c + 1) * ROW_CHUNK, :])
        for s in range(XS_SUB):
            h_ref[pl.ds(c * ROW_CHUNK * XS_SUB + s, ROW_CHUNK, stride=XS_SUB), :] = (
                words[:, s * V7X_LANES:(s + 1) * V7X_LANES])

    logits = _dot(hf_ref[...], w_ref[...].astype(bf16)) + b_ref[...]
    lane = lax.broadcasted_iota(i32, (tm, ROUTE_LANES), 1)
    neg = jnp.float32(-jnp.inf)
    lane_f = lane.astype(f32)

    def first_argmax(vals):
        top = jnp.max(vals, axis=-1, keepdims=True)
        idx = jnp.min(jnp.where(vals == top, lane_f, float(ROUTE_LANES)), axis=-1, keepdims=True)
        return top, idx.astype(i32)

    g_logits = jnp.where(lane < MOE_GROUPS, logits, neg)
    g_top, grp = first_argmax(g_logits)
    p_grp = 1.0 / jnp.sum(jnp.exp(g_logits - g_top), axis=-1, keepdims=True)

    e_lane = lane - MOE_GROUPS
    in_grp = (e_lane >= 0) & (e_lane < MOE_EXPERTS) & (lax.shift_right_arithmetic(e_lane, 3) == grp)
    e_logits = jnp.where(in_grp, logits, neg)
    top1, idx1 = first_argmax(e_logits)
    e_logits2 = jnp.where(lane == idx1, neg, e_logits)
    top2, idx2 = first_argmax(e_logits2)
    a = jnp.exp(top2 - top1)
    w1 = (1.0 / (1.0 + a)) * p_grp
    w2 = (a / (1.0 + a)) * p_grp
    e1 = idx1 - MOE_GROUPS
    e2 = idx2 - MOE_GROUPS

    hot1 = lane == e1
    hot2 = lane == e2
    o1 = jnp.where(hot1, 1.0, 0.0)
    o2 = jnp.where(hot2, 1.0, 0.0)
    tri = tri_ref[...]
    before1 = _dot(tri, o1.astype(bf16))
    before2 = _dot(tri, o2.astype(bf16))
    c1 = jnp.sum(o1, axis=0, keepdims=True)
    c2 = jnp.sum(o2, axis=0, keepdims=True)
    run = run_ref[...]
    rank1 = jnp.sum(jnp.where(hot1, before1 + run, 0.0), axis=-1, keepdims=True)
    rank2 = jnp.sum(jnp.where(hot2, before2 + run + c1, 0.0), axis=-1, keepdims=True)
    run = run + c1 + c2
    run_ref[...] = run
    cnt_ref[...] = jnp.broadcast_to(run, cnt_ref.shape)

    info = jnp.where(lane == LANE_E1, e1.astype(f32), 0.0)
    info = jnp.where(lane == LANE_E2, e2.astype(f32), info)
    info = jnp.where(lane == LANE_W1, w1, info)
    info = jnp.where(lane == LANE_W2, w2, info)
    info = jnp.where(lane == LANE_R1, rank1, info)
    info = jnp.where(lane == LANE_R2, rank2, info)
    info_ref[...] = info
    plan_ref[0] = jnp.transpose(info)[0:PLAN_ROWS, :].astype(i32)


def _route(x, mod_l, g2, w_route, b_route, *, n_tiles, mod_sel):
    tm = TM_ROUTE
    return pl.pallas_call(
        _route_kernel,
        grid=(n_tiles,),
        in_specs=[
            pl.BlockSpec((tm, D_MODEL), lambda i: (i, 0)),
            pl.BlockSpec((1, N_MOD, D_MODEL), lambda i: (mod_sel(i), 0, 0)),
            pl.BlockSpec((1, D_MODEL), lambda i: (0, 0)),
            pl.BlockSpec((D_MODEL, ROUTE_LANES), lambda i: (0, 0)),
            pl.BlockSpec((1, ROUTE_LANES), lambda i: (0, 0)),
        ],
        out_specs=[
            pl.BlockSpec((tm * XS_SUB, V7X_LANES), lambda i: (i, 0)),
            pl.BlockSpec((tm, ROUTE_LANES), lambda i: (i, 0)),
            pl.BlockSpec((1, PLAN_ROWS, tm), lambda i: (i, 0, 0)),
            pl.BlockSpec((8, ROUTE_LANES), lambda i: (0, 0)),
        ],
        out_shape=[
            jax.ShapeDtypeStruct((n_tiles * tm * XS_SUB, V7X_LANES), u32),
            jax.ShapeDtypeStruct((n_tiles * tm, ROUTE_LANES), f32),
            jax.ShapeDtypeStruct((n_tiles, PLAN_ROWS, tm), i32),
            jax.ShapeDtypeStruct((8, ROUTE_LANES), f32),
        ],
        scratch_shapes=[
            pltpu.VMEM((tm, D_MODEL), bf16),
            pltpu.VMEM((tm, tm), bf16),
            pltpu.VMEM((1, ROUTE_LANES), f32),
        ],
        compiler_params=_cparams(("arbitrary",)),
        name="moe_route",
    )(x, mod_l, g2, w_route, b_route)


def _dest_kernel(pend_ref, plan_ref, o_ref):
    plan = plan_ref[...]
    start = jnp.zeros_like(plan)
    for e in range(MOE_EXPERTS):
        start = jnp.where(plan == e, pend_ref[e], start)
    o_ref[...] = start + pltpu.roll(plan, plan.shape[0] - MOE_TOP_K, 0)


def _dest(pend0, plan):
    rows = plan.shape[0] * PLAN_ROWS
    return pl.pallas_call(
        _dest_kernel,
        grid_spec=pltpu.PrefetchScalarGridSpec(
            num_scalar_prefetch=1,
            grid=(1,),
            in_specs=[pl.BlockSpec((rows, TM_ROUTE), lambda i, pend: (0, 0))],
            out_specs=pl.BlockSpec((rows, TM_ROUTE), lambda i, pend: (0, 0)),
        ),
        out_shape=jax.ShapeDtypeStruct((rows, TM_ROUTE), i32),
        compiler_params=_cparams(("arbitrary",)),
        name="moe_dest",
    )(pend0, plan.reshape(rows, TM_ROUTE))


def _dispatch_kernel(pend_ref, dest_ref, h_ref, xs_ref, zero_ref, sem):
    tm = TM_ROUTE
    step = pl.program_id(0)

    blk_rows = MOE_BLK * XS_SUB

    def zero_copy(e):
        start = pl.multiple_of((pend_ref[e + 1] - MOE_BLK) * XS_SUB, blk_rows)
        return pltpu.make_async_copy(zero_ref, xs_ref.at[pl.ds(start, blk_rows)], sem)

    def tail_copy(blk):
        start = pl.multiple_of(blk * blk_rows, blk_rows)
        return pltpu.make_async_copy(zero_ref, xs_ref.at[pl.ds(start, blk_rows)], sem)

    @pl.when(step == 0)
    def _():
        zero_ref[...] = jnp.zeros_like(zero_ref)
        first_unused = pend_ref[MOE_EXPERTS] // MOE_BLK
        for phase in range(2):
            def body(e, carry):
                @pl.when(pend_ref[e + 1] > pend_ref[e])
                def _():
                    if phase == 0:
                        zero_copy(e).start()
                    else:
                        zero_copy(e).wait()
                return carry
            lax.fori_loop(0, MOE_EXPERTS, body, 0)

            def tail(blk, carry):
                if phase == 0:
                    tail_copy(blk).start()
                else:
                    tail_copy(blk).wait()
                return carry
            lax.fori_loop(first_unused, MOE_ROWS // MOE_BLK, tail, 0)

    def row_copy(r, k):
        d = dest_ref[k, r]
        src = h_ref.at[pl.ds(pl.multiple_of(r * XS_SUB, XS_SUB), XS_SUB)]
        return pltpu.make_async_copy(src, xs_ref.at[pl.ds(pl.multiple_of(d * XS_SUB, XS_SUB), XS_SUB)], sem)

    def start(r, carry):
        row_copy(r, 0).start(priority=0)
        row_copy(r, 1).start(priority=1)
        return carry

    lax.fori_loop(0, tm, start, 0, unroll=4)
    for k in range(MOE_TOP_K):
        pltpu.make_async_copy(h_ref, xs_ref.at[pl.ds(0, tm * XS_SUB)], sem).wait()


def _dispatch(pend0, dest, h_packed, *, n_tiles):
    tm = TM_ROUTE
    return pl.pallas_call(
        _dispatch_kernel,
        grid_spec=pltpu.PrefetchScalarGridSpec(
            num_scalar_prefetch=1,
            grid=(n_tiles,),
            in_specs=[
                pl.BlockSpec((PLAN_ROWS, tm), lambda i, pend: (i, 0), memory_space=pltpu.SMEM),
                pl.BlockSpec((tm * XS_SUB, V7X_LANES), lambda i, pend: (i, 0)),
            ],
            out_specs=pl.BlockSpec(memory_space=pl.ANY),
            scratch_shapes=[pltpu.VMEM((MOE_BLK * XS_SUB, V7X_LANES), u32), pltpu.SemaphoreType.DMA(())],
        ),
        out_shape=jax.ShapeDtypeStruct((MOE_ROWS * XS_SUB, V7X_LANES), u32),
        compiler_params=_cparams(("arbitrary",)),
        name="moe_dispatch",
    )(pend0, dest, h_packed)


WEIGHT_SLOTS = 3


def _expert_kernel(be_ref, nv_ref, first_ref, slot_ref, next_ref, next2_ref, xs_ref, wg_hbm, wu_hbm, wd_hbm,
                   ys_ref, wg_buf, wu_buf, wd_buf, wg16, wu16, wd16, sems, *, layer):
    i = pl.program_id(0)

    def weight_copies(e, s):
        return (pltpu.make_async_copy(wg_hbm.at[layer, e], wg_buf.at[s], sems.at[s, 0]),
                pltpu.make_async_copy(wu_hbm.at[layer, e], wu_buf.at[s], sems.at[s, 1]),
                pltpu.make_async_copy(wd_hbm.at[layer, e], wd_buf.at[s], sems.at[s, 2]))

    def swiglu_block(wg, wu, wd):
        parts = [_unpack_bf16_pairs(xs_ref[pl.ds(t, MOE_BLK, stride=XS_SUB), :]) for t in range(XS_SUB)]
        x = jnp.concatenate([p[0] for p in parts] + [p[1] for p in parts], axis=1).astype(bf16)
        gate = _dot(x, wg)
        up = _dot(x, wu)
        h = gate * _sigmoid(gate) * up
        words = _pack_bf16_pairs(_dot(h.astype(bf16), wd))
        for t in range(XS_SUB):
            ys_ref[pl.ds(t, MOE_BLK, stride=XS_SUB), :] = words[:, t * V7X_LANES:(t + 1) * V7X_LANES]

    @pl.when(i < nv_ref[0])
    def _():
        s = slot_ref[i]

        @pl.when(first_ref[i] == 1)
        def _():
            @pl.when(i == 0)
            def _():
                for cp in weight_copies(be_ref[0], 0):
                    cp.start()

                @pl.when(next_ref[0] >= 0)
                def _():
                    for cp in weight_copies(next_ref[0], 1):
                        cp.start()
            for cp in weight_copies(be_ref[i], s):
                cp.wait()

            @pl.when(next2_ref[i] >= 0)
            def _():
                for cp in weight_copies(next2_ref[i], (s + 2) % WEIGHT_SLOTS):
                    cp.start()

            wg, wu, wd = wg_buf[s].astype(bf16), wu_buf[s].astype(bf16), wd_buf[s].astype(bf16)
            wg16[...] = wg
            wu16[...] = wu
            wd16[...] = wd
            swiglu_block(wg, wu, wd)

        @pl.when(first_ref[i] == 0)
        def _():
            swiglu_block(wg16[...], wu16[...], wd16[...])

    @pl.when(i >= nv_ref[0])
    def _():
        ys_ref[...] = jnp.zeros_like(ys_ref)


def _experts(block_expert, n_valid, first, slot, nxt, nxt2, xs, w_gate, w_up, w_down, *, layer):
    n_blocks = MOE_ROWS // MOE_BLK

    def row(i, be, nv, *_):
        return jnp.minimum(i, nv[0] - 1)

    return pl.pallas_call(
        functools.partial(_expert_kernel, layer=layer),
        grid_spec=pltpu.PrefetchScalarGridSpec(
            num_scalar_prefetch=6,
            grid=(n_blocks,),
            in_specs=[
                pl.BlockSpec((MOE_BLK * XS_SUB, V7X_LANES), lambda i, *pf: (row(i, *pf), 0)),
                pl.BlockSpec(memory_space=pl.ANY),
                pl.BlockSpec(memory_space=pl.ANY),
                pl.BlockSpec(memory_space=pl.ANY),
            ],
            out_specs=pl.BlockSpec((MOE_BLK * XS_SUB, V7X_LANES), lambda i, *pf: (i, 0)),
            scratch_shapes=[
                pltpu.VMEM((WEIGHT_SLOTS, D_MODEL, MOE_D_FF), f32),
                pltpu.VMEM((WEIGHT_SLOTS, D_MODEL, MOE_D_FF), f32),
                pltpu.VMEM((WEIGHT_SLOTS, MOE_D_FF, D_MODEL), f32),
                pltpu.VMEM((D_MODEL, MOE_D_FF), bf16),
                pltpu.VMEM((D_MODEL, MOE_D_FF), bf16),
                pltpu.VMEM((MOE_D_FF, D_MODEL), bf16),
                pltpu.SemaphoreType.DMA((WEIGHT_SLOTS, 3)),
            ],
        ),
        out_shape=jax.ShapeDtypeStruct((MOE_ROWS * XS_SUB, V7X_LANES), u32),
        compiler_params=_cparams(("arbitrary",)),
        name="moe_experts",
    )(block_expert, n_valid, first, slot, nxt, nxt2, xs, w_gate, w_up, w_down)


def _combine_kernel(dest0_ref, destn_ref, x_ref, info_ref, mod_ref, modn_ref, gn_ref, ys_ref,
                    o_ref, hn_ref, buf_ref, sems, *, n_tiles, modulated):
    tm = TM_COMBINE
    i = pl.program_id(0)

    def start_tile(dest_ref, slot):
        def row_copy(r, k):
            d = dest_ref[k, r]
            src = ys_ref.at[pl.ds(pl.multiple_of(d * XS_SUB, XS_SUB), XS_SUB)]
            dst = buf_ref.at[slot, k, pl.ds(pl.multiple_of(r * XS_SUB, XS_SUB), XS_SUB)]
            return pltpu.make_async_copy(src, dst, sems.at[slot])

        def body(r, carry):
            row_copy(r, 0).start(priority=0)
            row_copy(r, 1).start(priority=1)
            return carry

        lax.fori_loop(0, tm, body, 0, unroll=4)

    @pl.when(i == 0)
    def _():
        start_tile(dest0_ref, 0)

    slot = i % 2

    @pl.when(i + 1 < n_tiles)
    def _():
        start_tile(destn_ref, 1 - slot)

    for k in range(MOE_TOP_K):
        pltpu.make_async_copy(ys_ref.at[pl.ds(0, tm * XS_SUB)], buf_ref.at[slot, k], sems.at[slot]).wait()

    info = info_ref[...]
    w1 = info[:, LANE_W1:LANE_W1 + 1]
    w2 = info[:, LANE_W2:LANE_W2 + 1]
    for t in range(XS_SUB):
        rows = pl.ds(t, tm, stride=XS_SUB)
        lo1, hi1 = _unpack_bf16_pairs(buf_ref[slot, 0, rows, :])
        lo2, hi2 = _unpack_bf16_pairs(buf_ref[slot, 1, rows, :])
        for half, (y1, y2) in enumerate(((lo1, lo2), (hi1, hi2))):
            cols = slice(half * PACKED + t * V7X_LANES, half * PACKED + (t + 1) * V7X_LANES)
            o_ref[:, cols] = x_ref[:, cols] + mod_ref[0, 5:6, cols] * (y1 * w1 + y2 * w2)

    g = gn_ref[...]
    for c in range(tm // ROW_CHUNK):
        rows = slice(c * ROW_CHUNK, (c + 1) * ROW_CHUNK)
        if modulated:
            y = _norm_mod_rows(o_ref[rows, :], g, modn_ref[0, 0:1, :], modn_ref[0, 1:2, :])
        else:
            xr = o_ref[rows, :]
            y = xr * lax.rsqrt(jnp.mean(xr * xr, axis=-1, keepdims=True) + NORM_EPS) * g
        hn_ref[rows, :] = y.astype(hn_ref.dtype)


def _combine(dest, x, info, mod_l, mod_next, g_next, ys, *, n_tiles, mod_sel):
    tm = TM_COMBINE
    per = TM_ROUTE // tm
    modulated = mod_next is not None
    if not modulated:
        mod_next = mod_l

    def dest_spec(tile):
        return pl.BlockSpec((PLAN_ROWS, tm), lambda i: (tile(i) // per, tile(i) % per), memory_space=pltpu.SMEM)

    return pl.pallas_call(
        functools.partial(_combine_kernel, n_tiles=n_tiles, modulated=modulated),
        grid=(n_tiles,),
        in_specs=[
            dest_spec(lambda i: 0),
            dest_spec(lambda i: jnp.minimum(i + 1, n_tiles - 1)),
            pl.BlockSpec((tm, D_MODEL), lambda i: (i, 0)),
            pl.BlockSpec((tm, ROUTE_LANES), lambda i: (i, 0)),
            pl.BlockSpec((1, N_MOD, D_MODEL), lambda i: (mod_sel(i), 0, 0)),
            pl.BlockSpec((1, N_MOD, D_MODEL), lambda i: (mod_sel(i), 0, 0)),
            pl.BlockSpec((1, D_MODEL), lambda i: (0, 0)),
            pl.BlockSpec(memory_space=pl.ANY),
        ],
        out_specs=[pl.BlockSpec((tm, D_MODEL), lambda i: (i, 0)),
                   pl.BlockSpec((tm, D_MODEL), lambda i: (i, 0))],
        scratch_shapes=[pltpu.VMEM((2, MOE_TOP_K, tm * XS_SUB, V7X_LANES), u32),
                        pltpu.SemaphoreType.DMA((2,))],
        out_shape=[jax.ShapeDtypeStruct(x.shape, f32),
                   jax.ShapeDtypeStruct((n_tiles * tm, D_MODEL), bf16 if modulated else f32)],
        input_output_aliases={2: 0},
        compiler_params=_cparams(("arbitrary",)),
        name="moe_combine",
    )(dest, dest, x, info, mod_l, mod_next, g_next, ys)


def _moe(x, mod_l, g2, w_route, b_route, w_gate, w_up, w_down, mod_next, g_next, *, layer, with_ctx):
    n_tok = N_TOK if with_ctx else N_LAT
    rt = n_tok // TM_ROUTE
    lat_rt = N_LAT // TM_ROUTE
    h_packed, info, plan, cnt = _route(
        x, mod_l, g2, w_route, b_route, n_tiles=rt,
        mod_sel=lambda i: jnp.where(i < lat_rt, i // (SEQ // TM_ROUTE), BATCH))

    counts = cnt[0, :MOE_EXPERTS].astype(i32)
    padded = (counts + MOE_BLK - 1) // MOE_BLK * MOE_BLK
    pend = jnp.cumsum(padded)
    pend0 = jnp.concatenate([jnp.zeros((1,), i32), pend]).astype(i32)
    n_blocks = MOE_ROWS // MOE_BLK
    blk_id = jnp.arange(n_blocks, dtype=i32)

    def expert_at(row):
        return jnp.minimum(jnp.sum((pend[None, :] <= row[:, None]).astype(i32), axis=1), MOE_EXPERTS - 1)

    block_expert = expert_at(blk_id * MOE_BLK)
    n_valid = (pend[-1:] // MOE_BLK).astype(i32)
    valid = blk_id < n_valid[0]
    changed = jnp.concatenate([jnp.ones((1,), bool), block_expert[1:] != block_expert[:-1]])
    first = (changed & valid).astype(i32)
    slot = ((jnp.cumsum(first) - 1) % WEIGHT_SLOTS).astype(i32)

    def following(expert):
        end = jnp.sum(jnp.where(jnp.arange(MOE_EXPERTS, dtype=i32)[None, :] == expert[:, None],
                                pend[None, :], 0), axis=1)
        return jnp.where((expert >= 0) & (end < pend[-1]), expert_at(end), -1).astype(i32)

    nxt = following(block_expert)
    nxt2 = following(nxt)

    dest = _dest(pend0, plan)
    xs = _dispatch(pend0, dest, h_packed, n_tiles=rt)
    ys = _experts(block_expert, n_valid, first, slot, nxt, nxt2, xs, w_gate, w_up, w_down, layer=layer)
    ct = n_tok // TM_COMBINE
    lat_ct = N_LAT // TM_COMBINE
    return _combine(dest, x, info, mod_l, mod_next, g_next, ys, n_tiles=ct,
                    mod_sel=lambda i: jnp.where(i < lat_ct, i // (SEQ // TM_COMBINE), BATCH))


def _rope_tables():
    rows = SEQ // GRID_W
    row = jnp.repeat(jnp.arange(rows, dtype=f32), GRID_W)
    col = jnp.tile(jnp.arange(GRID_W, dtype=f32), rows)
    n_freq = HEAD_DIM // 4
    inv_freq = ROPE_BASE ** (-jnp.arange(n_freq, dtype=f32) / n_freq)
    ang = jnp.concatenate([row[:, None] * inv_freq, col[:, None] * inv_freq], axis=-1)
    cos, sin = jnp.cos(ang), jnp.sin(ang)
    return jnp.concatenate([cos, cos], axis=-1), jnp.concatenate([-sin, sin], axis=-1)


def kernel(x, c, ctx, c_ctx, w_mod, b_mod, norm1_g, norm2_g, w_in, attn_sink, pool_w, pool_scale, ret_decay_fwd, ret_decay_bwd, w_br_attn, w_br_pool, w_br_ret, w_out, w_route_group, b_route_group, w_route_expert, b_route_expert, w_expert_gate, w_expert_up, w_expert_down, final_norm_g):
    assert x.shape == (BATCH, SEQ, D_MODEL) and ctx.shape == (BATCH, CTX_LEN, D_MODEL)
    cos_full, sin_signed = _rope_tables()

    cond8 = jnp.concatenate([c, c_ctx[None, :], jnp.zeros((8 - BATCH - 1, D_MODEL), f32)], axis=0)
    mod = _modulation(cond8, w_mod, b_mod).reshape(DEPTH, 8, N_MOD, D_MODEL)

    lg_fwd = jax.nn.log_sigmoid(ret_decay_fwd.astype(f32))
    lg_bwd = jax.nn.log_sigmoid(ret_decay_bwd.astype(f32))
    pad = ROUTE_LANES - MOE_GROUPS - MOE_EXPERTS
    w_route = jnp.concatenate(
        [w_route_group, w_route_expert, jnp.zeros((DEPTH, D_MODEL, pad), f32)], axis=-1)
    b_route = jnp.concatenate([b_route_group, b_route_expert, jnp.zeros((DEPTH, pad), f32)], axis=-1)

    tok = jnp.concatenate([x.reshape(N_LAT, D_MODEL), ctx.reshape(N_CTX, D_MODEL)], axis=0)
    ctx_block0 = N_LAT // TM_CTX
    sel_ctx = lambda i: BATCH

    def sel_rows(tm):
        return lambda i: jnp.where(i < N_LAT // tm, i // (SEQ // tm), BATCH)

    h1 = _norm(tok, mod[0], norm1_g[0][None, :], n_tiles=N_TOK // TM_NORM, mod_sel=sel_rows(TM_NORM))
    for l in range(DEPTH):
        with_ctx = l < DEPTH - 1
        mod_l = mod[l]
        p_lat = _proj(h1, w_in, layer=l, tm=TM_PROJ, row_block0=0, n_tiles=N_LAT // TM_PROJ)
        p_ctx = _proj(h1, w_in, layer=l, tm=TM_CTX, row_block0=ctx_block0, n_tiles=1)

        ya_lat = _attn_lat(p_lat, p_ctx, attn_sink[l], cos_full, sin_signed)
        yp_lat = _pool(p_lat, pool_w[l], pool_scale[l][None, :], n_seq=BATCH, seq_len=SEQ)
        yr_lat, yr_ctx = _retention(p_lat, p_ctx, lg_fwd[l], lg_bwd[l])

        tok = _merge_out(ya_lat, yp_lat, yr_lat, p_lat, w_br_attn, w_br_pool, w_br_ret, w_out, tok, mod_l,
                         layer=l, tm=TM_MERGE, row_block0=0, n_tiles=N_LAT // TM_MERGE,
                         mod_sel=sel_rows(TM_MERGE))
        if with_ctx:
            ya_ctx = _attn_ctx(p_ctx, attn_sink[l])
            yp_ctx = _pool(p_ctx, pool_w[l], pool_scale[l][None, :], n_seq=BATCH, seq_len=CTX_LEN)
            tok = _merge_out(ya_ctx, yp_ctx, yr_ctx, p_ctx, w_br_attn, w_br_pool, w_br_ret, w_out, tok, mod_l,
                             layer=l, tm=TM_CTX, row_block0=ctx_block0, n_tiles=1, mod_sel=sel_ctx)

        mod_next = mod[l + 1] if with_ctx else None
        g_next = norm1_g[l + 1][None, :] if with_ctx else final_norm_g[None, :]
        tok, h1 = _moe(tok, mod_l, norm2_g[l][None, :], w_route[l], b_route[l][None, :],
                       w_expert_gate, w_expert_up, w_expert_down, mod_next, g_next, layer=l, with_ctx=with_ctx)

    return h1.reshape(BATCH, SEQ, D_MODEL)
```

```python
import functools

import jax
import jax.numpy as jnp
from jax import lax
from jax.experimental import pallas as pl
from jax.experimental.pallas import tpu as pltpu

f32 = jnp.float32
bf16 = jnp.bfloat16
i32 = jnp.int32
u32 = jnp.uint32

D_MODEL = 2048
BATCH = 2
SEQ = 4096
DEPTH = 4
GRID_W = 64
CTX_LEN = 256
NORM_EPS = 1e-6
N_MOD = 6
HEAD_DIM = 128
ATTN_HEADS = 8
ATTN_KV_HEADS = 2
ATTN_GROUP = ATTN_HEADS // ATTN_KV_HEADS
ATTN_BLOCK = 128
ROPE_BASE = 10000.0
LOG2_E = 1.4426950408889634
POOL_GROUPS = 4
POOL_GROUP_DIM = 128
POOL_WIDTH = POOL_GROUPS * POOL_GROUP_DIM
POOL_SPANS = (2, 4, 8, 16)
RET_HEADS = 4
RET_QK_DIM = 64
RET_V_DIM = 128
N_BRANCHES = 3
ATTN_Q_WIDTH = ATTN_HEADS * HEAD_DIM
ATTN_KV_WIDTH = ATTN_KV_HEADS * HEAD_DIM
RET_QK_WIDTH = RET_HEADS * RET_QK_DIM
RET_V_WIDTH = RET_HEADS * RET_V_DIM
IN_SIZES = (ATTN_Q_WIDTH, ATTN_KV_WIDTH, ATTN_KV_WIDTH, POOL_WIDTH,
            RET_QK_WIDTH, RET_QK_WIDTH, RET_V_WIDTH, RET_V_WIDTH, N_BRANCHES * D_MODEL)
IN_WIDTH = sum(IN_SIZES)
MOE_GROUPS = 4
MOE_EXPERTS_PER_GROUP = 8
MOE_EXPERTS = MOE_GROUPS * MOE_EXPERTS_PER_GROUP
MOE_TOP_K = 2
MOE_D_FF = 512

COL_QA = 0
COL_KA = COL_QA + ATTN_Q_WIDTH
COL_VA = COL_KA + ATTN_KV_WIDTH
COL_U = COL_VA + ATTN_KV_WIDTH
COL_QR = COL_U + POOL_WIDTH
COL_KR = COL_QR + RET_QK_WIDTH
COL_VR = COL_KR + RET_QK_WIDTH
COL_GR = COL_VR + RET_V_WIDTH
COL_GATES = COL_GR + RET_V_WIDTH

V7X_LANES = 128
V7X_VMEM_LIMIT_BYTES = 56 * 1024 * 1024

N_LAT = BATCH * SEQ
N_CTX = BATCH * CTX_LEN
N_TOK = N_LAT + N_CTX
TM_PROJ = N_TOK // 4
CTX_BLK0 = N_LAT // CTX_LEN
TM_MERGE = 2048
TM_CTX = N_CTX
TM_NORM = 512
TN_PROJ = 512
TN_MERGE = 256
TN_MOD = 2048
ROW_CHUNK = 128
RET_CHUNK = 256
TM_ROUTE = 512
TM_COMBINE = 256
MOE_BLK = 256
MOE_ROWS = (N_TOK * MOE_TOP_K + MOE_EXPERTS * (MOE_BLK - 1)) // MOE_BLK * MOE_BLK
PACKED = D_MODEL // 2
XS_SUB = PACKED // V7X_LANES
YS_SUB = D_MODEL // V7X_LANES

assert CTX_LEN == RET_CHUNK and SEQ % RET_CHUNK == 0
assert IN_WIDTH % TN_PROJ == 0 and COL_GATES % TN_MERGE == 0


def _cparams(sem, vmem=V7X_VMEM_LIMIT_BYTES):
    return pltpu.CompilerParams(dimension_semantics=sem, vmem_limit_bytes=vmem)


def _dot(a, b):
    return jnp.dot(a, b, preferred_element_type=f32)


def _dot_nt(a, b):
    return lax.dot_general(a, b, (((1,), (1,)), ((), ())), preferred_element_type=f32)


def _sigmoid(x):
    return 0.5 * jnp.tanh(0.5 * x) + 0.5


def _dot_tn(a, b):
    return lax.dot_general(a, b, (((0,), (0,)), ((), ())), preferred_element_type=f32)


def _mod_kernel(cond_ref, w_ref, b_ref, o_ref):
    c = cond_ref[...]
    s = (c * jax.nn.sigmoid(c)).astype(bf16)
    o_ref[0] = _dot(s, w_ref[0].astype(bf16)) + b_ref[0]


def _modulation(cond8, w_mod, b_mod):
    n = N_MOD * D_MODEL
    return pl.pallas_call(
        _mod_kernel,
        grid=(DEPTH, n // TN_MOD),
        in_specs=[
            pl.BlockSpec((8, D_MODEL), lambda l, j: (0, 0)),
            pl.BlockSpec((1, D_MODEL, TN_MOD), lambda l, j: (l, 0, j)),
            pl.BlockSpec((1, 1, TN_MOD), lambda l, j: (l, 0, j)),
        ],
        out_specs=pl.BlockSpec((1, 8, TN_MOD), lambda l, j: (l, 0, j)),
        out_shape=jax.ShapeDtypeStruct((DEPTH, 8, n), f32),
        compiler_params=_cparams(("parallel", "parallel")),
        name="modulation",
    )(cond8, w_mod, b_mod.reshape(DEPTH, 1, n))


def _norm_mod_rows(x, g, shift, scale):
    ms = jnp.mean(x * x, axis=-1, keepdims=True)
    y = x * lax.rsqrt(ms + NORM_EPS) * g
    return y * (1.0 + scale) + shift


def _norm_kernel(x_ref, mod_ref, g_ref, o_ref, *, tm):
    g = g_ref[...]
    shift = mod_ref[0, 0:1, :]
    scale = mod_ref[0, 1:2, :]

    def body(r, carry):
        rows = pl.ds(pl.multiple_of(r * ROW_CHUNK, ROW_CHUNK), ROW_CHUNK)
        o_ref[rows, :] = _norm_mod_rows(x_ref[rows, :], g, shift, scale).astype(bf16)
        return carry

    lax.fori_loop(0, tm // ROW_CHUNK, body, 0)


def _norm(x, mod_l, g1, *, n_tiles, mod_sel):
    tm = TM_NORM
    return pl.pallas_call(
        functools.partial(_norm_kernel, tm=tm),
        grid=(n_tiles,),
        in_specs=[
            pl.BlockSpec((tm, D_MODEL), lambda i: (i, 0)),
            pl.BlockSpec((1, N_MOD, D_MODEL), lambda i: (mod_sel(i), 0, 0)),
            pl.BlockSpec((1, D_MODEL), lambda i: (0, 0)),
        ],
        out_specs=pl.BlockSpec((tm, D_MODEL), lambda i: (i, 0)),
        out_shape=jax.ShapeDtypeStruct((n_tiles * tm, D_MODEL), bf16),
        compiler_params=_cparams(("parallel",)),
        name="norm1",
    )(x, mod_l, g1)


def _proj_kernel(h_ref, w_ref, o_ref):
    o_ref[...] = _dot(h_ref[...], w_ref[...].astype(bf16)).astype(o_ref.dtype)


def _proj(h, w_in, *, layer):
    tm = TM_PROJ
    return pl.pallas_call(
        _proj_kernel,
        grid=(N_TOK // tm, IN_WIDTH // TN_PROJ),
        in_specs=[
            pl.BlockSpec((tm, D_MODEL), lambda i, j: (i, 0)),
            pl.BlockSpec((None, D_MODEL, TN_PROJ), lambda i, j: (layer, 0, j)),
        ],
        out_specs=pl.BlockSpec((tm, TN_PROJ), lambda i, j: (i, j)),
        out_shape=jax.ShapeDtypeStruct((N_TOK, IN_WIDTH), bf16),
        compiler_params=_cparams(("parallel", "arbitrary")),
        name="in_proj",
    )(h, w_in)


def _rope(t, cos_full, sin_signed):
    return t * cos_full + pltpu.roll(t, HEAD_DIM // 2, 1) * sin_signed


def _sink_column(sink_ref, g, rows_per_head):
    n = ATTN_GROUP * rows_per_head
    row = lax.broadcasted_iota(i32, (n, 1), 0)
    col = jnp.full((n, 1), sink_ref[g * ATTN_GROUP + ATTN_GROUP - 1], f32)
    for j in range(ATTN_GROUP - 2, -1, -1):
        col = jnp.where(row < (j + 1) * rows_per_head, sink_ref[g * ATTN_GROUP + j], col)
    return col


def _softmax_pv(scores, values, sink_col):
    def lane_chunks(arrs):
        return [a[:, c * V7X_LANES:(c + 1) * V7X_LANES] for a in arrs for c in range(a.shape[1] // V7X_LANES)]

    folded = functools.reduce(jnp.maximum, lane_chunks(scores))
    m = jnp.maximum(sink_col, jnp.max(folded, axis=-1, keepdims=True))
    es = [jnp.exp2(s - m) for s in scores]
    total = functools.reduce(lambda a, b: a + b, lane_chunks(es))
    denom = jnp.exp2(sink_col - m) + jnp.sum(total, axis=-1, keepdims=True)
    out = None
    for e, v in zip(es, values):
        o = _dot(e.astype(bf16), v)
        out = o if out is None else out + o
    return out * (1.0 / denom)


def _attn_lat_kernel(sink_ref, q_ref, kp_ref, km_ref, kn_ref, vp_ref, vm_ref, vn_ref, kx_ref, vx_ref,
                     cm_ref, sm_ref, cp_ref, sp_ref, cn_ref, sn_ref, o_ref, *, n_steps):
    step = pl.program_id(1)
    blk = ATTN_BLOCK
    scale = HEAD_DIM ** -0.5 * LOG2_E
    cm, sm = cm_ref[...], sm_ref[...]
    rows = ATTN_GROUP * blk
    qi = lax.broadcasted_iota(i32, (rows, blk), 0) % blk
    kj = lax.broadcasted_iota(i32, (rows, blk), 1)
    neg = jnp.float32(-jnp.inf)
    keep_prev = [(kj >= qi) & (step > 0), kj >= qi]
    keep_next = [kj <= qi, (kj <= qi) & (step < n_steps - 1)]

    for g in range(ATTN_KV_HEADS):
        kv = slice(g * HEAD_DIM, (g + 1) * HEAD_DIM)
        k_mid = _rope(km_ref[:, kv].astype(f32), cm, sm).astype(bf16)
        k_blocks = [_rope(kp_ref[:, kv].astype(f32), cp_ref[...], sp_ref[...]).astype(bf16),
                    k_mid[:blk], k_mid[blk:],
                    _rope(kn_ref[:, kv].astype(f32), cn_ref[...], sn_ref[...]).astype(bf16)]
        v_blocks = [vp_ref[:, kv], vm_ref[:blk, kv], vm_ref[blk:, kv], vn_ref[:, kv]]
        k_ctx, v_ctx = kx_ref[:, kv], vx_ref[:, kv]
        sink_col = _sink_column(sink_ref, g, blk) * LOG2_E
        for half in range(2):
            qrows = slice(half * blk, (half + 1) * blk)
            cq, sq = cm[qrows], sm[qrows]
            qs = jnp.concatenate(
                [(_rope(q_ref[qrows, (g * ATTN_GROUP + j) * HEAD_DIM:(g * ATTN_GROUP + j + 1) * HEAD_DIM]
                        .astype(f32), cq, sq) * scale).astype(bf16) for j in range(ATTN_GROUP)],
                axis=0)
            s_prev = jnp.where(keep_prev[half], _dot_nt(qs, k_blocks[half]), neg)
            s_cur = _dot_nt(qs, k_blocks[half + 1])
            s_next = jnp.where(keep_next[half], _dot_nt(qs, k_blocks[half + 2]), neg)
            s_ctx = _dot_nt(qs, k_ctx)
            out = _softmax_pv([s_prev, s_cur, s_next, s_ctx],
                              [v_blocks[half], v_blocks[half + 1], v_blocks[half + 2], v_ctx], sink_col)
            for j in range(ATTN_GROUP):
                h = g * ATTN_GROUP + j
                o_ref[qrows, h * HEAD_DIM:(h + 1) * HEAD_DIM] = out[j * blk:(j + 1) * blk, :].astype(o_ref.dtype)


def _attn_lat(p, sink, cos_full, sin_signed):
    nb = SEQ // ATTN_BLOCK
    blk = ATTN_BLOCK
    pair = 2 * blk
    n_steps = nb // 2
    kcol = COL_KA // ATTN_KV_WIDTH
    vcol = COL_VA // ATTN_KV_WIDTH

    def edge_block(s, shift):
        return jnp.clip(2 * s + shift, 0, nb - 1)

    def edge_spec(col, shift):
        return pl.BlockSpec((blk, ATTN_KV_WIDTH), lambda b, s: (b * nb + edge_block(s, shift), col))

    def mid_spec(col):
        return pl.BlockSpec((pair, ATTN_KV_WIDTH), lambda b, s: (b * n_steps + s, col))

    def edge_tab(shift):
        return pl.BlockSpec((blk, HEAD_DIM), lambda b, s: (edge_block(s, shift), 0))

    mid_tab = pl.BlockSpec((pair, HEAD_DIM), lambda b, s: (s, 0))
    return pl.pallas_call(
        functools.partial(_attn_lat_kernel, n_steps=n_steps),
        grid=(BATCH, n_steps),
        in_specs=[
            pl.BlockSpec(memory_space=pltpu.SMEM),
            pl.BlockSpec((pair, ATTN_Q_WIDTH), lambda b, s: (b * n_steps + s, 0)),
            edge_spec(kcol, -1), mid_spec(kcol), edge_spec(kcol, 2),
            edge_spec(vcol, -1), mid_spec(vcol), edge_spec(vcol, 2),
            pl.BlockSpec((CTX_LEN, ATTN_KV_WIDTH), lambda b, s: (CTX_BLK0 + b, kcol)),
            pl.BlockSpec((CTX_LEN, ATTN_KV_WIDTH), lambda b, s: (CTX_BLK0 + b, vcol)),
            mid_tab, mid_tab, edge_tab(-1), edge_tab(-1), edge_tab(2), edge_tab(2),
        ],
        out_specs=pl.BlockSpec((pair, ATTN_Q_WIDTH), lambda b, s: (b * n_steps + s, 0)),
        out_shape=jax.ShapeDtypeStruct((N_LAT, ATTN_Q_WIDTH), bf16),
        compiler_params=_cparams(("parallel", "arbitrary")),
        name="attn_lat",
    )(sink, p, p, p, p, p, p, p, p, p,
      cos_full, sin_signed, cos_full, sin_signed, cos_full, sin_signed)


def _attn_ctx_kernel(sink_ref, q_ref, kx_ref, vx_ref, o_ref):
    g = pl.program_id(1)
    scale = HEAD_DIM ** -0.5 * LOG2_E
    q = q_ref[...].astype(f32)
    qs = jnp.concatenate(
        [(q[:, j * HEAD_DIM:(j + 1) * HEAD_DIM] * scale).astype(bf16) for j in range(ATTN_GROUP)], axis=0)
    s_ctx = _dot_nt(qs, kx_ref[...])
    out = _softmax_pv([s_ctx], [vx_ref[...]], _sink_column(sink_ref, g, CTX_LEN) * LOG2_E)
    for j in range(ATTN_GROUP):
        o_ref[:, j * HEAD_DIM:(j + 1) * HEAD_DIM] = out[j * CTX_LEN:(j + 1) * CTX_LEN, :].astype(o_ref.dtype)


def _attn_ctx(p, sink):
    gw = ATTN_GROUP * HEAD_DIM
    kcol = COL_KA // HEAD_DIM
    vcol = COL_VA // HEAD_DIM
    return pl.pallas_call(
        _attn_ctx_kernel,
        grid=(BATCH, ATTN_KV_HEADS),
        in_specs=[
            pl.BlockSpec(memory_space=pltpu.SMEM),
            pl.BlockSpec((CTX_LEN, gw), lambda b, g: (CTX_BLK0 + b, g)),
            pl.BlockSpec((CTX_LEN, HEAD_DIM), lambda b, g: (CTX_BLK0 + b, kcol + g)),
            pl.BlockSpec((CTX_LEN, HEAD_DIM), lambda b, g: (CTX_BLK0 + b, vcol + g)),
        ],
        out_specs=pl.BlockSpec((CTX_LEN, gw), lambda b, g: (b, g)),
        out_shape=jax.ShapeDtypeStruct((N_CTX, ATTN_Q_WIDTH), bf16),
        compiler_params=_cparams(("parallel", "parallel")),
        name="attn_ctx",
    )(sink, p, p, p)


POOL_HALO = 8
POOL_ROWS = 256


def _pool_kernel(u_ref, w_ref, s_ref, o_ref, pad_ref, *, seq_len):
    zeros = jnp.zeros((POOL_HALO, POOL_GROUP_DIM), f32)
    for gi in range(POOL_GROUPS):
        radius = POOL_SPANS[gi] // 2
        cols = slice(gi * POOL_GROUP_DIM, (gi + 1) * POOL_GROUP_DIM)
        pad_ref[0:POOL_HALO, :] = zeros
        pad_ref[POOL_HALO + seq_len:POOL_HALO + seq_len + POOL_HALO, :] = zeros
        pad_ref[POOL_HALO:POOL_HALO + seq_len, :] = u_ref[:, cols].astype(f32)
        w = w_ref[gi].astype(bf16)
        sc = s_ref[:, cols]
        for c in range(seq_len // POOL_ROWS):
            base = c * POOL_ROWS
            acc = pad_ref[POOL_HALO + base:POOL_HALO + base + POOL_ROWS, :]
            center = acc
            for d in range(1, radius + 1):
                acc = acc + pad_ref[POOL_HALO + base - d:POOL_HALO + base - d + POOL_ROWS, :]
                acc = acc + pad_ref[POOL_HALO + base + d:POOL_HALO + base + d + POOL_ROWS, :]
            t = lax.broadcasted_iota(i32, (POOL_ROWS, 1), 0) + base
            lo = jnp.maximum(t - radius, 0)
            hi = jnp.minimum(t + radius + 1, seq_len)
            count = (hi - lo).astype(f32)
            pooled = (acc / count - center).astype(bf16)
            o_ref[base:base + POOL_ROWS, cols] = (_dot(pooled, w) * sc).astype(o_ref.dtype)


def _pool(p, pool_w, pool_scale, *, n_seq, seq_len, seq0):
    ucol = COL_U // POOL_WIDTH
    return pl.pallas_call(
        functools.partial(_pool_kernel, seq_len=seq_len),
        grid=(n_seq,),
        in_specs=[
            pl.BlockSpec((seq_len, POOL_WIDTH), lambda s: (seq0 + s, ucol)),
            pl.BlockSpec((POOL_GROUPS, POOL_GROUP_DIM, POOL_GROUP_DIM), lambda s: (0, 0, 0)),
            pl.BlockSpec((1, POOL_WIDTH), lambda s: (0, 0)),
        ],
        out_specs=pl.BlockSpec((seq_len, POOL_WIDTH), lambda s: (s, 0)),
        out_shape=jax.ShapeDtypeStruct((n_seq * seq_len, POOL_WIDTH), bf16),
        scratch_shapes=[pltpu.VMEM((seq_len + 2 * POOL_HALO, POOL_GROUP_DIM), f32)],
        compiler_params=_cparams(("parallel",)),
        name="pool",
    )(p, pool_w, pool_scale)


def _ret_tables(lg_ref, intra_ref, qdec_ref, kdec_ref, *, reverse):
    c = RET_CHUNK
    i = lax.broadcasted_iota(i32, (c, c), 0).astype(f32)
    j = lax.broadcasted_iota(i32, (c, c), 1).astype(f32)
    pos = lax.broadcasted_iota(i32, (c, V7X_LANES), 0).astype(f32)
    for h in range(RET_HEADS):
        lg = lg_ref[h]
        if reverse:
            diff = j - i
            qd = c - pos
            kd = pos
        else:
            diff = i - j
            qd = pos + 1.0
            kd = c - 1.0 - pos
        intra_ref[h] = jnp.where(diff >= 0, jnp.exp(lg * jnp.maximum(diff, 0.0)), 0.0)
        qdec_ref[h] = jnp.exp(lg * qd)
        kdec_ref[h] = jnp.exp(lg * kd)


def _ret_chunk(q_ref, k_ref, v_ref, lg_ref, intra_ref, qdec_ref, kdec_ref, state_ref):
    outs = []
    k_scale = RET_QK_DIM ** -0.5
    for h in range(RET_HEADS):
        q = q_ref[:, h * RET_QK_DIM:(h + 1) * RET_QK_DIM].astype(bf16)
        k = k_ref[:, h * RET_QK_DIM:(h + 1) * RET_QK_DIM].astype(f32) * k_scale
        v = v_ref[:, h * RET_V_DIM:(h + 1) * RET_V_DIM].astype(bf16)
        scores = _dot_nt(q, k.astype(bf16)) * intra_ref[h]
        state = state_ref[h]
        out = _dot(scores.astype(bf16), v) + _dot(q, state.astype(bf16)) * qdec_ref[h]
        kd = (k * kdec_ref[h][:, :RET_QK_DIM]).astype(bf16)
        chunk_decay = jnp.exp(jnp.full((1, RET_V_DIM), lg_ref[h] * RET_CHUNK, f32))
        state_ref[h] = state * chunk_decay + _dot_tn(kd, v)
        outs.append(out)
    return outs


def _ret_bwd_kernel(lg_ref, ql_ref, kl_ref, vl_ref, qc_ref, kc_ref, vc_ref, ol_ref, oc_ref,
                    intra_ref, qdec_ref, kdec_ref, state_ref):
    t = pl.program_id(1)

    @pl.when((pl.program_id(0) == 0) & (t == 0))
    def _():
        _ret_tables(lg_ref, intra_ref, qdec_ref, kdec_ref, reverse=True)

    @pl.when(t == 0)
    def _():
        state_ref[...] = jnp.zeros_like(state_ref)
        outs = _ret_chunk(qc_ref, kc_ref, vc_ref, lg_ref, intra_ref, qdec_ref, kdec_ref, state_ref)
        for h in range(RET_HEADS):
            oc_ref[:, h * RET_V_DIM:(h + 1) * RET_V_DIM] = outs[h]

    @pl.when(t > 0)
    def _():
        outs = _ret_chunk(ql_ref, kl_ref, vl_ref, lg_ref, intra_ref, qdec_ref, kdec_ref, state_ref)
        for h in range(RET_HEADS):
            ol_ref[:, h * RET_V_DIM:(h + 1) * RET_V_DIM] = outs[h]


def _ret_finish(outs, yb_ref, g_ref, o_ref):
    for h in range(RET_HEADS):
        cols = slice(h * RET_V_DIM, (h + 1) * RET_V_DIM)
        y = outs[h] + yb_ref[:, cols]
        mu = jnp.mean(y, axis=-1, keepdims=True)
        yc = y - mu
        var = jnp.mean(yc * yc, axis=-1, keepdims=True)
        yn = yc * lax.rsqrt(var + NORM_EPS)
        g = g_ref[:, cols].astype(f32)
        o_ref[:, cols] = (g * jax.nn.sigmoid(g) * yn).astype(o_ref.dtype)


def _ret_fwd_kernel(lg_ref, ql_ref, kl_ref, vl_ref, gl_ref, ybl_ref, qc_ref, kc_ref, vc_ref, gc_ref, ybc_ref,
                    ol_ref, oc_ref, intra_ref, qdec_ref, kdec_ref, state_ref):
    t = pl.program_id(1)

    @pl.when((pl.program_id(0) == 0) & (t == 0))
    def _():
        _ret_tables(lg_ref, intra_ref, qdec_ref, kdec_ref, reverse=False)

    @pl.when(t == 0)
    def _():
        state_ref[...] = jnp.zeros_like(state_ref)
        outs = _ret_chunk(qc_ref, kc_ref, vc_ref, lg_ref, intra_ref, qdec_ref, kdec_ref, state_ref)
        _ret_finish(outs, ybc_ref, gc_ref, oc_ref)

    @pl.when(t > 0)
    def _():
        outs = _ret_chunk(ql_ref, kl_ref, vl_ref, lg_ref, intra_ref, qdec_ref, kdec_ref, state_ref)
        _ret_finish(outs, ybl_ref, gl_ref, ol_ref)


def _ret_scratch():
    c = RET_CHUNK
    return [
        pltpu.VMEM((RET_HEADS, c, c), f32),
        pltpu.VMEM((RET_HEADS, c, V7X_LANES), f32),
        pltpu.VMEM((RET_HEADS, c, V7X_LANES), f32),
        pltpu.VMEM((RET_HEADS, RET_QK_DIM, RET_V_DIM), f32),
    ]


def _retention(p, lg_fwd, lg_bwd):
    c = RET_CHUNK
    nc = SEQ // c
    qcol, kcol = COL_QR // RET_QK_WIDTH, COL_KR // RET_QK_WIDTH
    vcol, gcol = COL_VR // RET_V_WIDTH, COL_GR // RET_V_WIDTH

    def lat_row(reverse):
        def f(b, t):
            s = jnp.maximum(t, 1)
            return b * nc + ((nc - s) if reverse else (s - 1))
        return f

    def specs(reverse, with_gate):
        row = lat_row(reverse)
        lat = [pl.BlockSpec((c, RET_QK_WIDTH), lambda b, t: (row(b, t), qcol)),
               pl.BlockSpec((c, RET_QK_WIDTH), lambda b, t: (row(b, t), kcol)),
               pl.BlockSpec((c, RET_V_WIDTH), lambda b, t: (row(b, t), vcol))]
        ctx = [pl.BlockSpec((c, RET_QK_WIDTH), lambda b, t: (CTX_BLK0 + b, qcol)),
               pl.BlockSpec((c, RET_QK_WIDTH), lambda b, t: (CTX_BLK0 + b, kcol)),
               pl.BlockSpec((c, RET_V_WIDTH), lambda b, t: (CTX_BLK0 + b, vcol))]
        if with_gate:
            lat += [pl.BlockSpec((c, RET_V_WIDTH), lambda b, t: (row(b, t), gcol)),
                    pl.BlockSpec((c, RET_V_WIDTH), lambda b, t: (row(b, t), 0))]
            ctx += [pl.BlockSpec((c, RET_V_WIDTH), lambda b, t: (CTX_BLK0 + b, gcol)),
                    pl.BlockSpec((c, RET_V_WIDTH), lambda b, t: (b, 0))]
        outs = [pl.BlockSpec((c, RET_V_WIDTH), lambda b, t: (row(b, t), 0)),
                pl.BlockSpec((c, RET_V_WIDTH), lambda b, t: (b, 0))]
        return [pl.BlockSpec(memory_space=pltpu.SMEM)] + lat + ctx, outs

    in_specs, out_specs = specs(True, False)
    yb_lat, yb_ctx = pl.pallas_call(
        _ret_bwd_kernel,
        grid=(BATCH, nc + 1),
        in_specs=in_specs,
        out_specs=out_specs,
        out_shape=[jax.ShapeDtypeStruct((N_LAT, RET_V_WIDTH), f32),
                   jax.ShapeDtypeStruct((N_CTX, RET_V_WIDTH), f32)],
        scratch_shapes=_ret_scratch(),
        compiler_params=_cparams(("arbitrary", "arbitrary")),
        name="retention_bwd",
    )(lg_bwd, p, p, p, p, p, p)

    in_specs, out_specs = specs(False, True)
    return pl.pallas_call(
        _ret_fwd_kernel,
        grid=(BATCH, nc + 1),
        in_specs=in_specs,
        out_specs=out_specs,
        out_shape=[jax.ShapeDtypeStruct((N_LAT, RET_V_WIDTH), bf16),
                   jax.ShapeDtypeStruct((N_CTX, RET_V_WIDTH), bf16)],
        scratch_shapes=_ret_scratch(),
        compiler_params=_cparams(("arbitrary", "arbitrary")),
        name="retention_fwd",
    )(lg_fwd, p, p, p, p, yb_lat, p, p, p, p, yb_ctx)


def _merge_out_kernel(ya_ref, yp_ref, yr_ref, ga_ref, gp_ref, gr_ref, wa_ref, wp_ref, wr_ref, wo_ref,
                      x_ref, mod_ref, o_ref, m_ref, *, n_col):
    j = pl.program_id(1)

    @pl.when(j < n_col)
    def _():
        acc = _sigmoid(ga_ref[...].astype(f32)) * _dot(ya_ref[...], wa_ref[...].astype(bf16))
        acc = acc + _sigmoid(gp_ref[...].astype(f32)) * _dot(yp_ref[...], wp_ref[...].astype(bf16))
        acc = acc + _sigmoid(gr_ref[...].astype(f32)) * _dot(yr_ref[...], wr_ref[...].astype(bf16))
        m_ref[j] = acc.astype(m_ref.dtype)

    @pl.when(j >= n_col)
    def _():
        y = None
        for c in range(n_col):
            part = _dot(m_ref[c], wo_ref[c * TN_MERGE:(c + 1) * TN_MERGE, :].astype(bf16))
            y = part if y is None else y + part
        o_ref[...] = x_ref[...] + mod_ref[0, 2:3, :] * y


def _merge_out(ya, yp, yr, p, w_a, w_p, w_r, w_out, x, mod_l, *, layer, tm, row_block0, n_tiles, mod_sel):
    n_col = D_MODEL // TN_MERGE
    gate0 = COL_GATES // TN_MERGE

    def first(j):
        return jnp.minimum(j, n_col - 1)

    def second(j):
        return jnp.maximum(j - n_col, 0)

    def gate_spec(br):
        return pl.BlockSpec((tm, TN_MERGE), lambda i, j: (row_block0 + i, gate0 + br * n_col + first(j)))

    return pl.pallas_call(
        functools.partial(_merge_out_kernel, n_col=n_col),
        grid=(n_tiles, 2 * n_col),
        in_specs=[
            pl.BlockSpec((tm, ATTN_Q_WIDTH), lambda i, j: (i, 0)),
            pl.BlockSpec((tm, POOL_WIDTH), lambda i, j: (i, 0)),
            pl.BlockSpec((tm, RET_V_WIDTH), lambda i, j: (i, 0)),
            gate_spec(0), gate_spec(1), gate_spec(2),
            pl.BlockSpec((None, ATTN_Q_WIDTH, TN_MERGE), lambda i, j: (layer, 0, first(j))),
            pl.BlockSpec((None, POOL_WIDTH, TN_MERGE), lambda i, j: (layer, 0, first(j))),
            pl.BlockSpec((None, RET_V_WIDTH, TN_MERGE), lambda i, j: (layer, 0, first(j))),
            pl.BlockSpec((None, D_MODEL, TN_MERGE), lambda i, j: (layer, 0, second(j))),
            pl.BlockSpec((tm, TN_MERGE), lambda i, j: (row_block0 + i, second(j))),
            pl.BlockSpec((1, N_MOD, TN_MERGE), lambda i, j: (mod_sel(i), 0, second(j))),
        ],
        out_specs=pl.BlockSpec((tm, TN_MERGE), lambda i, j: (row_block0 + i, second(j))),
        out_shape=jax.ShapeDtypeStruct(x.shape, f32),
        scratch_shapes=[pltpu.VMEM((n_col, tm, TN_MERGE), bf16)],
        input_output_aliases={10: 0},
        compiler_params=_cparams(("parallel", "arbitrary")),
        name="merge_out",
    )(ya, yp, yr, p, p, p, w_a, w_p, w_r, w_out, x, mod_l)


ROUTE_LANES = V7X_LANES
LANE_E1, LANE_E2, LANE_R1, LANE_R2, LANE_W1, LANE_W2 = 0, 1, 2, 3, 4, 5
PLAN_ROWS = 8
HI16 = 0xFFFF0000


def _pack_bf16_pairs(v):
    n = v.shape[1] // 2
    bits = lax.bitcast_convert_type(v.astype(bf16).astype(f32), u32)
    return (bits[:, :n] >> 16) | (bits[:, n:] & jnp.uint32(HI16))


def _unpack_bf16_pairs(words):
    return (lax.bitcast_convert_type(words << 16, f32),
            lax.bitcast_convert_type(words & jnp.uint32(HI16), f32))


def _route_kernel(x_ref, mod_ref, g_ref, w_ref, b_ref, h_ref, info_ref, plan_ref, cnt_ref,
                  hf_ref, tri_ref, run_ref):
    tm = TM_ROUTE
    step = pl.program_id(0)

    @pl.when(step == 0)
    def _():
        run_ref[...] = jnp.zeros_like(run_ref)
        r = lax.broadcasted_iota(i32, (tm, tm), 0)
        c = lax.broadcasted_iota(i32, (tm, tm), 1)
        tri_ref[...] = jnp.where(c < r, 1.0, 0.0).astype(bf16)

    g = g_ref[...]
    shift = mod_ref[0, 3:4, :]
    scale = mod_ref[0, 4:5, :]

    def body(r, carry):
        rows = pl.ds(pl.multiple_of(r * ROW_CHUNK, ROW_CHUNK), ROW_CHUNK)
        hf_ref[rows, :] = _norm_mod_rows(x_ref[rows, :], g, shift, scale).astype(bf16)
        return carry

    lax.fori_loop(0, tm // ROW_CHUNK, body, 0)

    for c in range(tm // ROW_CHUNK):
        words = _pack_bf16_pairs(hf_ref[c * ROW_CHUNK:(c + 1) * ROW_CHUNK, :])
        for s in range(XS_SUB):
            h_ref[pl.ds(c * ROW_CHUNK * XS_SUB + s, ROW_CHUNK, stride=XS_SUB), :] = (
                words[:, s * V7X_LANES:(s + 1) * V7X_LANES])

    logits = _dot(hf_ref[...], w_ref[...].astype(bf16)) + b_ref[...]
    lane = lax.broadcasted_iota(i32, (tm, ROUTE_LANES), 1)
    neg = jnp.float32(-jnp.inf)
    lane_f = lane.astype(f32)

    def first_argmax(vals):
        top = jnp.max(vals, axis=-1, keepdims=True)
        idx = jnp.min(jnp.where(vals == top, lane_f, float(ROUTE_LANES)), axis=-1, keepdims=True)
        return top, idx.astype(i32)

    g_logits = jnp.where(lane < MOE_GROUPS, logits, neg)
    g_top, grp = first_argmax(g_logits)
    p_grp = 1.0 / jnp.sum(jnp.exp(g_logits - g_top), axis=-1, keepdims=True)

    e_lane = lane - MOE_GROUPS
    in_grp = (e_lane >= 0) & (e_lane < MOE_EXPERTS) & (lax.shift_right_arithmetic(e_lane, 3) == grp)
    e_logits = jnp.where(in_grp, logits, neg)
    top1, idx1 = first_argmax(e_logits)
    e_logits2 = jnp.where(lane == idx1, neg, e_logits)
    top2, idx2 = first_argmax(e_logits2)
    a = jnp.exp(top2 - top1)
    w1 = (1.0 / (1.0 + a)) * p_grp
    w2 = (a / (1.0 + a)) * p_grp
    e1 = idx1 - MOE_GROUPS
    e2 = idx2 - MOE_GROUPS

    hot1 = lane == e1
    hot2 = lane == e2
    o1 = jnp.where(hot1, 1.0, 0.0)
    o2 = jnp.where(hot2, 1.0, 0.0)
    tri = tri_ref[...]
    before1 = _dot(tri, o1.astype(bf16))
    before2 = _dot(tri, o2.astype(bf16))
    c1 = jnp.sum(o1, axis=0, keepdims=True)
    c2 = jnp.sum(o2, axis=0, keepdims=True)
    run = run_ref[...]
    rank1 = jnp.sum(jnp.where(hot1, before1 + run, 0.0), axis=-1, keepdims=True)
    rank2 = jnp.sum(jnp.where(hot2, before2 + run + c1, 0.0), axis=-1, keepdims=True)
    run = run + c1 + c2
    run_ref[...] = run
    cnt_ref[...] = jnp.broadcast_to(run, cnt_ref.shape)

    info = jnp.where(lane == LANE_E1, e1.astype(f32), 0.0)
    info = jnp.where(lane == LANE_E2, e2.astype(f32), info)
    info = jnp.where(lane == LANE_W1, w1, info)
    info = jnp.where(lane == LANE_W2, w2, info)
    info = jnp.where(lane == LANE_R1, rank1, info)
    info = jnp.where(lane == LANE_R2, rank2, info)
    info_ref[...] = info
    plan_ref[0] = jnp.transpose(info)[0:PLAN_ROWS, :].astype(i32)


def _route(x, mod_l, g2, w_route, b_route, *, n_tiles, mod_sel):
    tm = TM_ROUTE
    return pl.pallas_call(
        _route_kernel,
        grid=(n_tiles,),
        in_specs=[
            pl.BlockSpec((tm, D_MODEL), lambda i: (i, 0)),
            pl.BlockSpec((1, N_MOD, D_MODEL), lambda i: (mod_sel(i), 0, 0)),
            pl.BlockSpec((1, D_MODEL), lambda i: (0, 0)),
            pl.BlockSpec((D_MODEL, ROUTE_LANES), lambda i: (0, 0)),
            pl.BlockSpec((1, ROUTE_LANES), lambda i: (0, 0)),
        ],
        out_specs=[
            pl.BlockSpec((tm * XS_SUB, V7X_LANES), lambda i: (i, 0)),
            pl.BlockSpec((tm, ROUTE_LANES), lambda i: (i, 0)),
            pl.BlockSpec((1, PLAN_ROWS, tm), lambda i: (i, 0, 0)),
            pl.BlockSpec((8, ROUTE_LANES), lambda i: (0, 0)),
        ],
        out_shape=[
            jax.ShapeDtypeStruct((n_tiles * tm * XS_SUB, V7X_LANES), u32),
            jax.ShapeDtypeStruct((n_tiles * tm, ROUTE_LANES), f32),
            jax.ShapeDtypeStruct((n_tiles, PLAN_ROWS, tm), i32),
            jax.ShapeDtypeStruct((8, ROUTE_LANES), f32),
        ],
        scratch_shapes=[
            pltpu.VMEM((tm, D_MODEL), bf16),
            pltpu.VMEM((tm, tm), bf16),
            pltpu.VMEM((1, ROUTE_LANES), f32),
        ],
        compiler_params=_cparams(("arbitrary",)),
        name="moe_route",
    )(x, mod_l, g2, w_route, b_route)


def _dest_kernel(pend_ref, plan_ref, o_ref):
    plan = plan_ref[...]
    start = jnp.zeros_like(plan)
    for e in range(MOE_EXPERTS):
        start = jnp.where(plan == e, pend_ref[e], start)
    o_ref[...] = start + pltpu.roll(plan, plan.shape[0] - MOE_TOP_K, 0)


def _dest(pend0, plan):
    rows = plan.shape[0] * PLAN_ROWS
    return pl.pallas_call(
        _dest_kernel,
        grid_spec=pltpu.PrefetchScalarGridSpec(
            num_scalar_prefetch=1,
            grid=(1,),
            in_specs=[pl.BlockSpec((rows, TM_ROUTE), lambda i, pend: (0, 0))],
            out_specs=pl.BlockSpec((rows, TM_ROUTE), lambda i, pend: (0, 0)),
        ),
        out_shape=jax.ShapeDtypeStruct((rows, TM_ROUTE), i32),
        compiler_params=_cparams(("arbitrary",)),
        name="moe_dest",
    )(pend0, plan.reshape(rows, TM_ROUTE))


def _dispatch_kernel(pend_ref, dest_ref, h_ref, xs_ref, zero_ref, sem):
    tm = TM_ROUTE
    step = pl.program_id(0)

    blk_rows = MOE_BLK * XS_SUB

    def zero_copy(e):
        start = pl.multiple_of((pend_ref[e + 1] - MOE_BLK) * XS_SUB, blk_rows)
        return pltpu.make_async_copy(zero_ref, xs_ref.at[pl.ds(start, blk_rows)], sem)

    def tail_copy(blk):
        start = pl.multiple_of(blk * blk_rows, blk_rows)
        return pltpu.make_async_copy(zero_ref, xs_ref.at[pl.ds(start, blk_rows)], sem)

    @pl.when(step == 0)
    def _():
        zero_ref[...] = jnp.zeros_like(zero_ref)
        first_unused = pend_ref[MOE_EXPERTS] // MOE_BLK
        for phase in range(2):
            def body(e, carry):
                @pl.when(pend_ref[e + 1] > pend_ref[e])
                def _():
                    if phase == 0:
                        zero_copy(e).start()
                    else:
                        zero_copy(e).wait()
                return carry
            lax.fori_loop(0, MOE_EXPERTS, body, 0)

            def tail(blk, carry):
                if phase == 0:
                    tail_copy(blk).start()
                else:
                    tail_copy(blk).wait()
                return carry
            lax.fori_loop(first_unused, MOE_ROWS // MOE_BLK, tail, 0)

    def row_copy(r, k):
        d = dest_ref[k, r]
        src = h_ref.at[pl.ds(pl.multiple_of(r * XS_SUB, XS_SUB), XS_SUB)]
        return pltpu.make_async_copy(src, xs_ref.at[pl.ds(pl.multiple_of(d * XS_SUB, XS_SUB), XS_SUB)], sem)

    def start(r, carry):
        row_copy(r, 0).start()
        row_copy(r, 1).start()
        return carry

    lax.fori_loop(0, tm, start, 0, unroll=4)
    for k in range(MOE_TOP_K):
        pltpu.make_async_copy(h_ref, xs_ref.at[pl.ds(0, tm * XS_SUB)], sem).wait()


def _dispatch(pend0, dest, h_packed, *, n_tiles):
    tm = TM_ROUTE
    return pl.pallas_call(
        _dispatch_kernel,
        grid_spec=pltpu.PrefetchScalarGridSpec(
            num_scalar_prefetch=1,
            grid=(n_tiles,),
            in_specs=[
                pl.BlockSpec((PLAN_ROWS, tm), lambda i, pend: (i, 0), memory_space=pltpu.SMEM),
                pl.BlockSpec((tm * XS_SUB, V7X_LANES), lambda i, pend: (i, 0)),
            ],
            out_specs=pl.BlockSpec(memory_space=pl.ANY),
            scratch_shapes=[pltpu.VMEM((MOE_BLK * XS_SUB, V7X_LANES), u32), pltpu.SemaphoreType.DMA(())],
        ),
        out_shape=jax.ShapeDtypeStruct((MOE_ROWS * XS_SUB, V7X_LANES), u32),
        compiler_params=_cparams(("arbitrary",)),
        name="moe_dispatch",
    )(pend0, dest, h_packed)


WEIGHT_SLOTS = 3


def _expert_kernel(be_ref, nv_ref, first_ref, slot_ref, next_ref, next2_ref, xs_ref, wg_hbm, wu_hbm, wd_hbm,
                   ys_ref, wg_buf, wu_buf, wd_buf, wg16, wu16, wd16, sems, *, layer):
    i = pl.program_id(0)

    def weight_copies(e, s):
        return (pltpu.make_async_copy(wg_hbm.at[layer, e], wg_buf.at[s], sems.at[s, 0]),
                pltpu.make_async_copy(wu_hbm.at[layer, e], wu_buf.at[s], sems.at[s, 1]),
                pltpu.make_async_copy(wd_hbm.at[layer, e], wd_buf.at[s], sems.at[s, 2]))

    def swiglu_block(wg, wu, wd):
        parts = [_unpack_bf16_pairs(xs_ref[pl.ds(t, MOE_BLK, stride=XS_SUB), :]) for t in range(XS_SUB)]
        x = jnp.concatenate([p[0] for p in parts] + [p[1] for p in parts], axis=1).astype(bf16)
        gate = _dot(x, wg)
        up = _dot(x, wu)
        h = gate * _sigmoid(gate) * up
        words = _pack_bf16_pairs(_dot(h.astype(bf16), wd))
        for t in range(XS_SUB):
            ys_ref[pl.ds(t, MOE_BLK, stride=XS_SUB), :] = words[:, t * V7X_LANES:(t + 1) * V7X_LANES]

    @pl.when(i < nv_ref[0])
    def _():
        s = slot_ref[i]

        @pl.when(first_ref[i] == 1)
        def _():
            @pl.when(i == 0)
            def _():
                for cp in weight_copies(be_ref[0], 0):
                    cp.start()

                @pl.when(next_ref[0] >= 0)
                def _():
                    for cp in weight_copies(next_ref[0], 1):
                        cp.start()
            for cp in weight_copies(be_ref[i], s):
                cp.wait()

            @pl.when(next2_ref[i] >= 0)
            def _():
                for cp in weight_copies(next2_ref[i], (s + 2) % WEIGHT_SLOTS):
                    cp.start()

            wg, wu, wd = wg_buf[s].astype(bf16), wu_buf[s].astype(bf16), wd_buf[s].astype(bf16)
            wg16[...] = wg
            wu16[...] = wu
            wd16[...] = wd
            swiglu_block(wg, wu, wd)

        @pl.when(first_ref[i] == 0)
        def _():
            swiglu_block(wg16[...], wu16[...], wd16[...])

    @pl.when(i >= nv_ref[0])
    def _():
        ys_ref[...] = jnp.zeros_like(ys_ref)


def _experts(block_expert, n_valid, first, slot, nxt, nxt2, xs, w_gate, w_up, w_down, *, layer):
    n_blocks = MOE_ROWS // MOE_BLK

    def row(i, be, nv, *_):
        return jnp.minimum(i, nv[0] - 1)

    return pl.pallas_call(
        functools.partial(_expert_kernel, layer=layer),
        grid_spec=pltpu.PrefetchScalarGridSpec(
            num_scalar_prefetch=6,
            grid=(n_blocks,),
            in_specs=[
                pl.BlockSpec((MOE_BLK * XS_SUB, V7X_LANES), lambda i, *pf: (row(i, *pf), 0)),
                pl.BlockSpec(memory_space=pl.ANY),
                pl.BlockSpec(memory_space=pl.ANY),
                pl.BlockSpec(memory_space=pl.ANY),
            ],
            out_specs=pl.BlockSpec((MOE_BLK * XS_SUB, V7X_LANES), lambda i, *pf: (i, 0)),
            scratch_shapes=[
                pltpu.VMEM((WEIGHT_SLOTS, D_MODEL, MOE_D_FF), f32),
                pltpu.VMEM((WEIGHT_SLOTS, D_MODEL, MOE_D_FF), f32),
                pltpu.VMEM((WEIGHT_SLOTS, MOE_D_FF, D_MODEL), f32),
                pltpu.VMEM((D_MODEL, MOE_D_FF), bf16),
                pltpu.VMEM((D_MODEL, MOE_D_FF), bf16),
                pltpu.VMEM((MOE_D_FF, D_MODEL), bf16),
                pltpu.SemaphoreType.DMA((WEIGHT_SLOTS, 3)),
            ],
        ),
        out_shape=jax.ShapeDtypeStruct((MOE_ROWS * XS_SUB, V7X_LANES), u32),
        compiler_params=_cparams(("arbitrary",)),
        name="moe_experts",
    )(block_expert, n_valid, first, slot, nxt, nxt2, xs, w_gate, w_up, w_down)


def _combine_kernel(dest0_ref, destn_ref, x_ref, info_ref, mod_ref, modn_ref, gn_ref, ys_ref,
                    o_ref, hn_ref, buf_ref, sems, *, n_tiles, modulated):
    tm = TM_COMBINE
    i = pl.program_id(0)

    def start_tile(dest_ref, slot):
        def row_copy(r, k):
            d = dest_ref[k, r]
            src = ys_ref.at[pl.ds(pl.multiple_of(d * XS_SUB, XS_SUB), XS_SUB)]
            dst = buf_ref.at[slot, k, pl.ds(pl.multiple_of(r * XS_SUB, XS_SUB), XS_SUB)]
            return pltpu.make_async_copy(src, dst, sems.at[slot])

        def body(r, carry):
            row_copy(r, 0).start()
            row_copy(r, 1).start()
            return carry

        lax.fori_loop(0, tm, body, 0, unroll=4)

    @pl.when(i == 0)
    def _():
        start_tile(dest0_ref, 0)

    slot = i % 2

    @pl.when(i + 1 < n_tiles)
    def _():
        start_tile(destn_ref, 1 - slot)

    for k in range(MOE_TOP_K):
        pltpu.make_async_copy(ys_ref.at[pl.ds(0, tm * XS_SUB)], buf_ref.at[slot, k], sems.at[slot]).wait()

    info = info_ref[...]
    w1 = info[:, LANE_W1:LANE_W1 + 1]
    w2 = info[:, LANE_W2:LANE_W2 + 1]
    for t in range(XS_SUB):
        rows = pl.ds(t, tm, stride=XS_SUB)
        lo1, hi1 = _unpack_bf16_pairs(buf_ref[slot, 0, rows, :])
        lo2, hi2 = _unpack_bf16_pairs(buf_ref[slot, 1, rows, :])
        for half, (y1, y2) in enumerate(((lo1, lo2), (hi1, hi2))):
            cols = slice(half * PACKED + t * V7X_LANES, half * PACKED + (t + 1) * V7X_LANES)
            o_ref[:, cols] = x_ref[:, cols] + mod_ref[0, 5:6, cols] * (y1 * w1 + y2 * w2)

    g = gn_ref[...]
    for c in range(tm // ROW_CHUNK):
        rows = slice(c * ROW_CHUNK, (c + 1) * ROW_CHUNK)
        if modulated:
            y = _norm_mod_rows(o_ref[rows, :], g, modn_ref[0, 0:1, :], modn_ref[0, 1:2, :])
        else:
            xr = o_ref[rows, :]
            y = xr * lax.rsqrt(jnp.mean(xr * xr, axis=-1, keepdims=True) + NORM_EPS) * g
        hn_ref[rows, :] = y.astype(hn_ref.dtype)


def _combine(dest, x, info, mod_l, mod_next, g_next, ys, *, n_tiles, mod_sel):
    tm = TM_COMBINE
    per = TM_ROUTE // tm
    modulated = mod_next is not None
    if not modulated:
        mod_next = mod_l

    def dest_spec(tile):
        return pl.BlockSpec((PLAN_ROWS, tm), lambda i: (tile(i) // per, tile(i) % per), memory_space=pltpu.SMEM)

    return pl.pallas_call(
        functools.partial(_combine_kernel, n_tiles=n_tiles, modulated=modulated),
        grid=(n_tiles,),
        in_specs=[
            dest_spec(lambda i: 0),
            dest_spec(lambda i: jnp.minimum(i + 1, n_tiles - 1)),
            pl.BlockSpec((tm, D_MODEL), lambda i: (i, 0)),
            pl.BlockSpec((tm, ROUTE_LANES), lambda i: (i, 0)),
            pl.BlockSpec((1, N_MOD, D_MODEL), lambda i: (mod_sel(i), 0, 0)),
            pl.BlockSpec((1, N_MOD, D_MODEL), lambda i: (mod_sel(i), 0, 0)),
            pl.BlockSpec((1, D_MODEL), lambda i: (0, 0)),
            pl.BlockSpec(memory_space=pl.ANY),
        ],
        out_specs=[pl.BlockSpec((tm, D_MODEL), lambda i: (i, 0)),
                   pl.BlockSpec((tm, D_MODEL), lambda i: (i, 0))],
        scratch_shapes=[pltpu.VMEM((2, MOE_TOP_K, tm * XS_SUB, V7X_LANES), u32),
                        pltpu.SemaphoreType.DMA((2,))],
        out_shape=[jax.ShapeDtypeStruct(x.shape, f32),
                   jax.ShapeDtypeStruct((n_tiles * tm, D_MODEL), bf16 if modulated else f32)],
        input_output_aliases={2: 0},
        compiler_params=_cparams(("arbitrary",)),
        name="moe_combine",
    )(dest, dest, x, info, mod_l, mod_next, g_next, ys)


def _moe(x, mod_l, g2, w_route, b_route, w_gate, w_up, w_down, mod_next, g_next, *, layer, with_ctx):
    n_tok = N_TOK if with_ctx else N_LAT
    rt = n_tok // TM_ROUTE
    lat_rt = N_LAT // TM_ROUTE
    h_packed, info, plan, cnt = _route(
        x, mod_l, g2, w_route, b_route, n_tiles=rt,
        mod_sel=lambda i: jnp.where(i < lat_rt, i // (SEQ // TM_ROUTE), BATCH))

    counts = cnt[0, :MOE_EXPERTS].astype(i32)
    padded = (counts + MOE_BLK - 1) // MOE_BLK * MOE_BLK
    pend = jnp.cumsum(padded)
    pend0 = jnp.concatenate([jnp.zeros((1,), i32), pend]).astype(i32)
    n_blocks = MOE_ROWS // MOE_BLK
    blk_id = jnp.arange(n_blocks, dtype=i32)

    def expert_at(row):
        return jnp.minimum(jnp.sum((pend[None, :] <= row[:, None]).astype(i32), axis=1), MOE_EXPERTS - 1)

    block_expert = expert_at(blk_id * MOE_BLK)
    n_valid = (pend[-1:] // MOE_BLK).astype(i32)
    valid = blk_id < n_valid[0]
    changed = jnp.concatenate([jnp.ones((1,), bool), block_expert[1:] != block_expert[:-1]])
    first = (changed & valid).astype(i32)
    slot = ((jnp.cumsum(first) - 1) % WEIGHT_SLOTS).astype(i32)

    def following(expert):
        end = jnp.sum(jnp.where(jnp.arange(MOE_EXPERTS, dtype=i32)[None, :] == expert[:, None],
                                pend[None, :], 0), axis=1)
        return jnp.where((expert >= 0) & (end < pend[-1]), expert_at(end), -1).astype(i32)

    nxt = following(block_expert)
    nxt2 = following(nxt)

    dest = _dest(pend0, plan)
    xs = _dispatch(pend0, dest, h_packed, n_tiles=rt)
    ys = _experts(block_expert, n_valid, first, slot, nxt, nxt2, xs, w_gate, w_up, w_down, layer=layer)
    ct = n_tok // TM_COMBINE
    lat_ct = N_LAT // TM_COMBINE
    return _combine(dest, x, info, mod_l, mod_next, g_next, ys, n_tiles=ct,
                    mod_sel=lambda i: jnp.where(i < lat_ct, i // (SEQ // TM_COMBINE), BATCH))


def _rope_tables():
    rows = SEQ // GRID_W
    row = jnp.repeat(jnp.arange(rows, dtype=f32), GRID_W)
    col = jnp.tile(jnp.arange(GRID_W, dtype=f32), rows)
    n_freq = HEAD_DIM // 4
    inv_freq = ROPE_BASE ** (-jnp.arange(n_freq, dtype=f32) / n_freq)
    ang = jnp.concatenate([row[:, None] * inv_freq, col[:, None] * inv_freq], axis=-1)
    cos, sin = jnp.cos(ang), jnp.sin(ang)
    return jnp.concatenate([cos, cos], axis=-1), jnp.concatenate([-sin, sin], axis=-1)


def kernel(x, c, ctx, c_ctx, w_mod, b_mod, norm1_g, norm2_g, w_in, attn_sink, pool_w, pool_scale, ret_decay_fwd, ret_decay_bwd, w_br_attn, w_br_pool, w_br_ret, w_out, w_route_group, b_route_group, w_route_expert, b_route_expert, w_expert_gate, w_expert_up, w_expert_down, final_norm_g):
    assert x.shape == (BATCH, SEQ, D_MODEL) and ctx.shape == (BATCH, CTX_LEN, D_MODEL)
    cos_full, sin_signed = _rope_tables()

    cond8 = jnp.concatenate([c, c_ctx[None, :], jnp.zeros((8 - BATCH - 1, D_MODEL), f32)], axis=0)
    mod = _modulation(cond8, w_mod, b_mod).reshape(DEPTH, 8, N_MOD, D_MODEL)

    lg_fwd = jax.nn.log_sigmoid(ret_decay_fwd.astype(f32))
    lg_bwd = jax.nn.log_sigmoid(ret_decay_bwd.astype(f32))
    pad = ROUTE_LANES - MOE_GROUPS - MOE_EXPERTS
    w_route = jnp.concatenate(
        [w_route_group, w_route_expert, jnp.zeros((DEPTH, D_MODEL, pad), f32)], axis=-1)
    b_route = jnp.concatenate([b_route_group, b_route_expert, jnp.zeros((DEPTH, pad), f32)], axis=-1)

    tok = jnp.concatenate([x.reshape(N_LAT, D_MODEL), ctx.reshape(N_CTX, D_MODEL)], axis=0)
    ctx_block0 = N_LAT // TM_CTX
    sel_ctx = lambda i: BATCH

    def sel_rows(tm):
        return lambda i: jnp.where(i < N_LAT // tm, i // (SEQ // tm), BATCH)

    h1 = _norm(tok, mod[0], norm1_g[0][None, :], n_tiles=N_TOK // TM_NORM, mod_sel=sel_rows(TM_NORM))
    for l in range(DEPTH):
        with_ctx = l < DEPTH - 1
        mod_l = mod[l]
        p = _proj(h1, w_in, layer=l)

        ya_lat = _attn_lat(p, attn_sink[l], cos_full, sin_signed)
        yp_lat = _pool(p, pool_w[l], pool_scale[l][None, :], n_seq=BATCH, seq_len=SEQ, seq0=0)
        yr_lat, yr_ctx = _retention(p, lg_fwd[l], lg_bwd[l])

        tok = _merge_out(ya_lat, yp_lat, yr_lat, p, w_br_attn, w_br_pool, w_br_ret, w_out, tok, mod_l,
                         layer=l, tm=TM_MERGE, row_block0=0, n_tiles=N_LAT // TM_MERGE,
                         mod_sel=sel_rows(TM_MERGE))
        if with_ctx:
            ya_ctx = _attn_ctx(p, attn_sink[l])
            yp_ctx = _pool(p, pool_w[l], pool_scale[l][None, :], n_seq=BATCH, seq_len=CTX_LEN, seq0=CTX_BLK0)
            tok = _merge_out(ya_ctx, yp_ctx, yr_ctx, p, w_br_attn, w_br_pool, w_br_ret, w_out, tok, mod_l,
                             layer=l, tm=TM_CTX, row_block0=ctx_block0, n_tiles=1, mod_sel=sel_ctx)

        mod_next = mod[l + 1] if with_ctx else None
        g_next = norm1_g[l + 1][None, :] if with_ctx else final_norm_g[None, :]
        tok, h1 = _moe(tok, mod_l, norm2_g[l][None, :], w_route[l], b_route[l][None, :],
                       w_expert_gate, w_expert_up, w_expert_down, mod_next, g_next, layer=l, with_ctx=with_ctx)

    return h1.reshape(BATCH, SEQ, D_MODEL)
```

```python
import functools

import jax
import jax.numpy as jnp
from jax import lax
from jax.experimental import pallas as pl
from jax.experimental.pallas import tpu as pltpu

f32 = jnp.float32
bf16 = jnp.bfloat16
i32 = jnp.int32
u32 = jnp.uint32

D_MODEL = 2048
BATCH = 2
SEQ = 4096
DEPTH = 4
GRID_W = 64
CTX_LEN = 256
NORM_EPS = 1e-6
N_MOD = 6
HEAD_DIM = 128
ATTN_HEADS = 8
ATTN_KV_HEADS = 2
ATTN_GROUP = ATTN_HEADS // ATTN_KV_HEADS
ATTN_BLOCK = 128
ROPE_BASE = 10000.0
LOG2_E = 1.4426950408889634
POOL_GROUPS = 4
POOL_GROUP_DIM = 128
POOL_WIDTH = POOL_GROUPS * POOL_GROUP_DIM
POOL_SPANS = (2, 4, 8, 16)
RET_HEADS = 4
RET_QK_DIM = 64
RET_V_DIM = 128
N_BRANCHES = 3
ATTN_Q_WIDTH = ATTN_HEADS * HEAD_DIM
ATTN_KV_WIDTH = ATTN_KV_HEADS * HEAD_DIM
RET_QK_WIDTH = RET_HEADS * RET_QK_DIM
RET_V_WIDTH = RET_HEADS * RET_V_DIM
IN_SIZES = (ATTN_Q_WIDTH, ATTN_KV_WIDTH, ATTN_KV_WIDTH, POOL_WIDTH,
            RET_QK_WIDTH, RET_QK_WIDTH, RET_V_WIDTH, RET_V_WIDTH, N_BRANCHES * D_MODEL)
IN_WIDTH = sum(IN_SIZES)
MOE_GROUPS = 4
MOE_EXPERTS_PER_GROUP = 8
MOE_EXPERTS = MOE_GROUPS * MOE_EXPERTS_PER_GROUP
MOE_TOP_K = 2
MOE_D_FF = 512

COL_QA = 0
COL_KA = COL_QA + ATTN_Q_WIDTH
COL_VA = COL_KA + ATTN_KV_WIDTH
COL_U = COL_VA + ATTN_KV_WIDTH
COL_QR = COL_U + POOL_WIDTH
COL_KR = COL_QR + RET_QK_WIDTH
COL_VR = COL_KR + RET_QK_WIDTH
COL_GR = COL_VR + RET_V_WIDTH
COL_GATES = COL_GR + RET_V_WIDTH

V7X_LANES = 128
V7X_VMEM_LIMIT_BYTES = 56 * 1024 * 1024

N_LAT = BATCH * SEQ
N_CTX = BATCH * CTX_LEN
N_TOK = N_LAT + N_CTX
TM_PROJ = N_TOK // 4
CTX_BLK0 = N_LAT // CTX_LEN
TM_MERGE = 2048
TM_CTX = N_CTX
TM_NORM = 512
TN_PROJ = 512
TN_MERGE = 256
TN_MOD = 2048
ROW_CHUNK = 128
RET_CHUNK = 256
TM_ROUTE = 512
TM_COMBINE = 256
MOE_BLK = 256
MOE_ROWS = (N_TOK * MOE_TOP_K + MOE_EXPERTS * (MOE_BLK - 1)) // MOE_BLK * MOE_BLK
PACKED = D_MODEL // 2
XS_SUB = PACKED // V7X_LANES
YS_SUB = D_MODEL // V7X_LANES

assert CTX_LEN == RET_CHUNK and SEQ % RET_CHUNK == 0
assert IN_WIDTH % TN_PROJ == 0 and COL_GATES % TN_MERGE == 0


def _cparams(sem, vmem=V7X_VMEM_LIMIT_BYTES):
    return pltpu.CompilerParams(dimension_semantics=sem, vmem_limit_bytes=vmem)


def _dot(a, b):
    return jnp.dot(a, b, preferred_element_type=f32)


def _dot_nt(a, b):
    return lax.dot_general(a, b, (((1,), (1,)), ((), ())), preferred_element_type=f32)


def _sigmoid(x):
    return 0.5 * jnp.tanh(0.5 * x) + 0.5


def _dot_tn(a, b):
    return lax.dot_general(a, b, (((0,), (0,)), ((), ())), preferred_element_type=f32)


def _mod_kernel(cond_ref, w_ref, b_ref, o_ref):
    c = cond_ref[...]
    s = (c * jax.nn.sigmoid(c)).astype(bf16)
    o_ref[0] = _dot(s, w_ref[0].astype(bf16)) + b_ref[0]


def _modulation(cond8, w_mod, b_mod):
    n = N_MOD * D_MODEL
    return pl.pallas_call(
        _mod_kernel,
        grid=(DEPTH, n // TN_MOD),
        in_specs=[
            pl.BlockSpec((8, D_MODEL), lambda l, j: (0, 0)),
            pl.BlockSpec((1, D_MODEL, TN_MOD), lambda l, j: (l, 0, j)),
            pl.BlockSpec((1, 1, TN_MOD), lambda l, j: (l, 0, j)),
        ],
        out_specs=pl.BlockSpec((1, 8, TN_MOD), lambda l, j: (l, 0, j)),
        out_shape=jax.ShapeDtypeStruct((DEPTH, 8, n), f32),
        compiler_params=_cparams(("parallel", "parallel")),
        name="modulation",
    )(cond8, w_mod, b_mod.reshape(DEPTH, 1, n))


def _norm_mod_rows(x, g, shift, scale):
    ms = jnp.mean(x * x, axis=-1, keepdims=True)
    y = x * lax.rsqrt(ms + NORM_EPS) * g
    return y * (1.0 + scale) + shift


def _norm_kernel(x_ref, mod_ref, g_ref, o_ref, *, tm):
    g = g_ref[...]
    shift = mod_ref[0, 0:1, :]
    scale = mod_ref[0, 1:2, :]

    def body(r, carry):
        rows = pl.ds(pl.multiple_of(r * ROW_CHUNK, ROW_CHUNK), ROW_CHUNK)
        o_ref[rows, :] = _norm_mod_rows(x_ref[rows, :], g, shift, scale).astype(bf16)
        return carry

    lax.fori_loop(0, tm // ROW_CHUNK, body, 0)


def _norm(x, mod_l, g1, *, n_tiles, mod_sel):
    tm = TM_NORM
    return pl.pallas_call(
        functools.partial(_norm_kernel, tm=tm),
        grid=(n_tiles,),
        in_specs=[
            pl.BlockSpec((tm, D_MODEL), lambda i: (i, 0)),
            pl.BlockSpec((1, N_MOD, D_MODEL), lambda i: (mod_sel(i), 0, 0)),
            pl.BlockSpec((1, D_MODEL), lambda i: (0, 0)),
        ],
        out_specs=pl.BlockSpec((tm, D_MODEL), lambda i: (i, 0)),
        out_shape=jax.ShapeDtypeStruct((n_tiles * tm, D_MODEL), bf16),
        compiler_params=_cparams(("parallel",)),
        name="norm1",
    )(x, mod_l, g1)


def _proj_kernel(h_ref, w_ref, o_ref):
    o_ref[...] = _dot(h_ref[...], w_ref[...].astype(bf16)).astype(o_ref.dtype)


def _proj(h, w_in, *, layer):
    tm = TM_PROJ
    return pl.pallas_call(
        _proj_kernel,
        grid=(N_TOK // tm, IN_WIDTH // TN_PROJ),
        in_specs=[
            pl.BlockSpec((tm, D_MODEL), lambda i, j: (i, 0)),
            pl.BlockSpec((None, D_MODEL, TN_PROJ), lambda i, j: (layer, 0, j)),
        ],
        out_specs=pl.BlockSpec((tm, TN_PROJ), lambda i, j: (i, j)),
        out_shape=jax.ShapeDtypeStruct((N_TOK, IN_WIDTH), bf16),
        compiler_params=_cparams(("parallel", "arbitrary")),
        name="in_proj",
    )(h, w_in)


def _rope(t, cos_full, sin_signed):
    return t * cos_full + pltpu.roll(t, HEAD_DIM // 2, 1) * sin_signed


def _sink_column(sink_ref, g, rows_per_head):
    n = ATTN_GROUP * rows_per_head
    row = lax.broadcasted_iota(i32, (n, 1), 0)
    col = jnp.full((n, 1), sink_ref[g * ATTN_GROUP + ATTN_GROUP - 1], f32)
    for j in range(ATTN_GROUP - 2, -1, -1):
        col = jnp.where(row < (j + 1) * rows_per_head, sink_ref[g * ATTN_GROUP + j], col)
    return col


def _softmax_pv(scores, values, sink_col):
    def lane_chunks(arrs):
        return [a[:, c * V7X_LANES:(c + 1) * V7X_LANES] for a in arrs for c in range(a.shape[1] // V7X_LANES)]

    folded = functools.reduce(jnp.maximum, lane_chunks(scores))
    m = jnp.maximum(sink_col, jnp.max(folded, axis=-1, keepdims=True))
    es = [jnp.exp2(s - m) for s in scores]
    total = functools.reduce(lambda a, b: a + b, lane_chunks(es))
    denom = jnp.exp2(sink_col - m) + jnp.sum(total, axis=-1, keepdims=True)
    out = None
    for e, v in zip(es, values):
        o = _dot(e.astype(bf16), v)
        out = o if out is None else out + o
    return out * (1.0 / denom)


def _attn_lat_kernel(sink_ref, q_ref, kp_ref, km_ref, kn_ref, vp_ref, vm_ref, vn_ref, kx_ref, vx_ref,
                     cm_ref, sm_ref, cp_ref, sp_ref, cn_ref, sn_ref, o_ref, *, n_steps):
    step = pl.program_id(1)
    blk = ATTN_BLOCK
    scale = HEAD_DIM ** -0.5 * LOG2_E
    cm, sm = cm_ref[...], sm_ref[...]
    rows = ATTN_GROUP * blk
    qi = lax.broadcasted_iota(i32, (rows, blk), 0) % blk
    kj = lax.broadcasted_iota(i32, (rows, blk), 1)
    neg = jnp.float32(-jnp.inf)
    keep_prev = [(kj >= qi) & (step > 0), kj >= qi]
    keep_next = [kj <= qi, (kj <= qi) & (step < n_steps - 1)]

    for g in range(ATTN_KV_HEADS):
        kv = slice(g * HEAD_DIM, (g + 1) * HEAD_DIM)
        k_mid = _rope(km_ref[:, kv].astype(f32), cm, sm).astype(bf16)
        k_blocks = [_rope(kp_ref[:, kv].astype(f32), cp_ref[...], sp_ref[...]).astype(bf16),
                    k_mid[:blk], k_mid[blk:],
                    _rope(kn_ref[:, kv].astype(f32), cn_ref[...], sn_ref[...]).astype(bf16)]
        v_blocks = [vp_ref[:, kv], vm_ref[:blk, kv], vm_ref[blk:, kv], vn_ref[:, kv]]
        k_ctx, v_ctx = kx_ref[:, kv], vx_ref[:, kv]
        sink_col = _sink_column(sink_ref, g, blk) * LOG2_E
        for half in range(2):
            qrows = slice(half * blk, (half + 1) * blk)
            cq, sq = cm[qrows], sm[qrows]
            qs = jnp.concatenate(
                [(_rope(q_ref[qrows, (g * ATTN_GROUP + j) * HEAD_DIM:(g * ATTN_GROUP + j + 1) * HEAD_DIM]
                        .astype(f32), cq, sq) * scale).astype(bf16) for j in range(ATTN_GROUP)],
                axis=0)
            s_prev = jnp.where(keep_prev[half], _dot_nt(qs, k_blocks[half]), neg)
            s_cur = _dot_nt(qs, k_blocks[half + 1])
            s_next = jnp.where(keep_next[half], _dot_nt(qs, k_blocks[half + 2]), neg)
            s_ctx = _dot_nt(qs, k_ctx)
            out = _softmax_pv([s_prev, s_cur, s_next, s_ctx],
                              [v_blocks[half], v_blocks[half + 1], v_blocks[half + 2], v_ctx], sink_col)
            for j in range(ATTN_GROUP):
                h = g * ATTN_GROUP + j
                o_ref[qrows, h * HEAD_DIM:(h + 1) * HEAD_DIM] = out[j * blk:(j + 1) * blk, :].astype(o_ref.dtype)


def _attn_lat(p, sink, cos_full, sin_signed):
    nb = SEQ // ATTN_BLOCK
    blk = ATTN_BLOCK
    pair = 2 * blk
    n_steps = nb // 2
    kcol = COL_KA // ATTN_KV_WIDTH
    vcol = COL_VA // ATTN_KV_WIDTH

    def edge_block(s, shift):
        return jnp.clip(2 * s + shift, 0, nb - 1)

    def edge_spec(col, shift):
        return pl.BlockSpec((blk, ATTN_KV_WIDTH), lambda b, s: (b * nb + edge_block(s, shift), col))

    def mid_spec(col):
        return pl.BlockSpec((pair, ATTN_KV_WIDTH), lambda b, s: (b * n_steps + s, col))

    def edge_tab(shift):
        return pl.BlockSpec((blk, HEAD_DIM), lambda b, s: (edge_block(s, shift), 0))

    mid_tab = pl.BlockSpec((pair, HEAD_DIM), lambda b, s: (s, 0))
    return pl.pallas_call(
        functools.partial(_attn_lat_kernel, n_steps=n_steps),
        grid=(BATCH, n_steps),
        in_specs=[
            pl.BlockSpec(memory_space=pltpu.SMEM),
            pl.BlockSpec((pair, ATTN_Q_WIDTH), lambda b, s: (b * n_steps + s, 0)),
            edge_spec(kcol, -1), mid_spec(kcol), edge_spec(kcol, 2),
            edge_spec(vcol, -1), mid_spec(vcol), edge_spec(vcol, 2),
            pl.BlockSpec((CTX_LEN, ATTN_KV_WIDTH), lambda b, s: (CTX_BLK0 + b, kcol)),
            pl.BlockSpec((CTX_LEN, ATTN_KV_WIDTH), lambda b, s: (CTX_BLK0 + b, vcol)),
            mid_tab, mid_tab, edge_tab(-1), edge_tab(-1), edge_tab(2), edge_tab(2),
        ],
        out_specs=pl.BlockSpec((pair, ATTN_Q_WIDTH), lambda b, s: (b * n_steps + s, 0)),
        out_shape=jax.ShapeDtypeStruct((N_LAT, ATTN_Q_WIDTH), bf16),
        compiler_params=_cparams(("parallel", "arbitrary")),
        name="attn_lat",
    )(sink, p, p, p, p, p, p, p, p, p,
      cos_full, sin_signed, cos_full, sin_signed, cos_full, sin_signed)


def _attn_ctx_kernel(sink_ref, q_ref, kx_ref, vx_ref, o_ref):
    g = pl.program_id(1)
    scale = HEAD_DIM ** -0.5 * LOG2_E
    q = q_ref[...].astype(f32)
    qs = jnp.concatenate(
        [(q[:, j * HEAD_DIM:(j + 1) * HEAD_DIM] * scale).astype(bf16) for j in range(ATTN_GROUP)], axis=0)
    s_ctx = _dot_nt(qs, kx_ref[...])
    out = _softmax_pv([s_ctx], [vx_ref[...]], _sink_column(sink_ref, g, CTX_LEN) * LOG2_E)
    for j in range(ATTN_GROUP):
        o_ref[:, j * HEAD_DIM:(j + 1) * HEAD_DIM] = out[j * CTX_LEN:(j + 1) * CTX_LEN, :].astype(o_ref.dtype)


def _attn_ctx(p, sink):
    gw = ATTN_GROUP * HEAD_DIM
    kcol = COL_KA // HEAD_DIM
    vcol = COL_VA // HEAD_DIM
    return pl.pallas_call(
        _attn_ctx_kernel,
        grid=(BATCH, ATTN_KV_HEADS),
        in_specs=[
            pl.BlockSpec(memory_space=pltpu.SMEM),
            pl.BlockSpec((CTX_LEN, gw), lambda b, g: (CTX_BLK0 + b, g)),
            pl.BlockSpec((CTX_LEN, HEAD_DIM), lambda b, g: (CTX_BLK0 + b, kcol + g)),
            pl.BlockSpec((CTX_LEN, HEAD_DIM), lambda b, g: (CTX_BLK0 + b, vcol + g)),
        ],
        out_specs=pl.BlockSpec((CTX_LEN, gw), lambda b, g: (b, g)),
        out_shape=jax.ShapeDtypeStruct((N_CTX, ATTN_Q_WIDTH), bf16),
        compiler_params=_cparams(("parallel", "parallel")),
        name="attn_ctx",
    )(sink, p, p, p)


POOL_HALO = 8
POOL_ROWS = 256


def _pool_group(u_ref, w_ref, s_ref, o_ref, pad_ref, *, seq_len, radius):
    zeros = jnp.zeros((POOL_HALO, POOL_GROUP_DIM), f32)
    pad_ref[0:POOL_HALO, :] = zeros
    pad_ref[POOL_HALO + seq_len:POOL_HALO + seq_len + POOL_HALO, :] = zeros
    pad_ref[POOL_HALO:POOL_HALO + seq_len, :] = u_ref[...].astype(f32)
    w = w_ref[0].astype(bf16)
    sc = s_ref[...]
    for c in range(seq_len // POOL_ROWS):
        base = c * POOL_ROWS
        acc = pad_ref[POOL_HALO + base:POOL_HALO + base + POOL_ROWS, :]
        center = acc
        for d in range(1, radius + 1):
            acc = acc + pad_ref[POOL_HALO + base - d:POOL_HALO + base - d + POOL_ROWS, :]
            acc = acc + pad_ref[POOL_HALO + base + d:POOL_HALO + base + d + POOL_ROWS, :]
        t = lax.broadcasted_iota(i32, (POOL_ROWS, 1), 0) + base
        lo = jnp.maximum(t - radius, 0)
        hi = jnp.minimum(t + radius + 1, seq_len)
        count = (hi - lo).astype(f32)
        pooled = (acc / count - center).astype(bf16)
        o_ref[base:base + POOL_ROWS, :] = (_dot(pooled, w) * sc).astype(o_ref.dtype)


def _pool_kernel(u_ref, w_ref, s_ref, o_ref, pad_ref, *, seq_len):
    for gi in range(POOL_GROUPS):
        @pl.when(pl.program_id(1) == gi)
        def _():
            _pool_group(u_ref, w_ref, s_ref, o_ref, pad_ref, seq_len=seq_len, radius=POOL_SPANS[gi] // 2)


def _pool(p, pool_w, pool_scale, *, n_seq, seq_len, seq0):
    ucol = COL_U // POOL_GROUP_DIM
    return pl.pallas_call(
        functools.partial(_pool_kernel, seq_len=seq_len),
        grid=(n_seq, POOL_GROUPS),
        in_specs=[
            pl.BlockSpec((seq_len, POOL_GROUP_DIM), lambda s, g: (seq0 + s, ucol + g)),
            pl.BlockSpec((1, POOL_GROUP_DIM, POOL_GROUP_DIM), lambda s, g: (g, 0, 0)),
            pl.BlockSpec((1, POOL_GROUP_DIM), lambda s, g: (0, g)),
        ],
        out_specs=pl.BlockSpec((seq_len, POOL_GROUP_DIM), lambda s, g: (s, g)),
        out_shape=jax.ShapeDtypeStruct((n_seq * seq_len, POOL_WIDTH), bf16),
        scratch_shapes=[pltpu.VMEM((seq_len + 2 * POOL_HALO, POOL_GROUP_DIM), f32)],
        compiler_params=_cparams(("parallel", "arbitrary")),
        name="pool",
    )(p, pool_w, pool_scale)


def _ret_tables(lg_ref, intra_ref, qdec_ref, kdec_ref, *, reverse):
    c = RET_CHUNK
    i = lax.broadcasted_iota(i32, (c, c), 0).astype(f32)
    j = lax.broadcasted_iota(i32, (c, c), 1).astype(f32)
    pos = lax.broadcasted_iota(i32, (c, V7X_LANES), 0).astype(f32)
    for h in range(RET_HEADS):
        lg = lg_ref[h]
        if reverse:
            diff = j - i
            qd = c - pos
            kd = pos
        else:
            diff = i - j
            qd = pos + 1.0
            kd = c - 1.0 - pos
        intra_ref[h] = jnp.where(diff >= 0, jnp.exp(lg * jnp.maximum(diff, 0.0)), 0.0)
        qdec_ref[h] = jnp.exp(lg * qd)
        kdec_ref[h] = jnp.exp(lg * kd)


def _ret_chunk(q_ref, k_ref, v_ref, lg_ref, intra_ref, qdec_ref, kdec_ref, state_ref):
    outs = []
    k_scale = RET_QK_DIM ** -0.5
    for h in range(RET_HEADS):
        q = q_ref[:, h * RET_QK_DIM:(h + 1) * RET_QK_DIM].astype(bf16)
        k = k_ref[:, h * RET_QK_DIM:(h + 1) * RET_QK_DIM].astype(f32) * k_scale
        v = v_ref[:, h * RET_V_DIM:(h + 1) * RET_V_DIM].astype(bf16)
        scores = _dot_nt(q, k.astype(bf16)) * intra_ref[h]
        state = state_ref[h]
        out = _dot(scores.astype(bf16), v) + _dot(q, state.astype(bf16)) * qdec_ref[h]
        kd = (k * kdec_ref[h][:, :RET_QK_DIM]).astype(bf16)
        chunk_decay = jnp.exp(jnp.full((1, RET_V_DIM), lg_ref[h] * RET_CHUNK, f32))
        state_ref[h] = state * chunk_decay + _dot_tn(kd, v)
        outs.append(out)
    return outs


def _ret_bwd_kernel(lg_ref, ql_ref, kl_ref, vl_ref, qc_ref, kc_ref, vc_ref, ol_ref, oc_ref,
                    intra_ref, qdec_ref, kdec_ref, state_ref):
    t = pl.program_id(1)

    @pl.when((pl.program_id(0) == 0) & (t == 0))
    def _():
        _ret_tables(lg_ref, intra_ref, qdec_ref, kdec_ref, reverse=True)

    @pl.when(t == 0)
    def _():
        state_ref[...] = jnp.zeros_like(state_ref)
        outs = _ret_chunk(qc_ref, kc_ref, vc_ref, lg_ref, intra_ref, qdec_ref, kdec_ref, state_ref)
        for h in range(RET_HEADS):
            oc_ref[:, h * RET_V_DIM:(h + 1) * RET_V_DIM] = outs[h]

    @pl.when(t > 0)
    def _():
        outs = _ret_chunk(ql_ref, kl_ref, vl_ref, lg_ref, intra_ref, qdec_ref, kdec_ref, state_ref)
        for h in range(RET_HEADS):
            ol_ref[:, h * RET_V_DIM:(h + 1) * RET_V_DIM] = outs[h]


def _ret_finish(outs, yb_ref, g_ref, o_ref):
    for h in range(RET_HEADS):
        cols = slice(h * RET_V_DIM, (h + 1) * RET_V_DIM)
        y = outs[h] + yb_ref[:, cols]
        mu = jnp.mean(y, axis=-1, keepdims=True)
        yc = y - mu
        var = jnp.mean(yc * yc, axis=-1, keepdims=True)
        yn = yc * lax.rsqrt(var + NORM_EPS)
        g = g_ref[:, cols].astype(f32)
        o_ref[:, cols] = (g * jax.nn.sigmoid(g) * yn).astype(o_ref.dtype)


def _ret_fwd_kernel(lg_ref, ql_ref, kl_ref, vl_ref, gl_ref, ybl_ref, qc_ref, kc_ref, vc_ref, gc_ref, ybc_ref,
                    ol_ref, oc_ref, intra_ref, qdec_ref, kdec_ref, state_ref):
    t = pl.program_id(1)

    @pl.when((pl.program_id(0) == 0) & (t == 0))
    def _():
        _ret_tables(lg_ref, intra_ref, qdec_ref, kdec_ref, reverse=False)

    @pl.when(t == 0)
    def _():
        state_ref[...] = jnp.zeros_like(state_ref)
        outs = _ret_chunk(qc_ref, kc_ref, vc_ref, lg_ref, intra_ref, qdec_ref, kdec_ref, state_ref)
        _ret_finish(outs, ybc_ref, gc_ref, oc_ref)

    @pl.when(t > 0)
    def _():
        outs = _ret_chunk(ql_ref, kl_ref, vl_ref, lg_ref, intra_ref, qdec_ref, kdec_ref, state_ref)
        _ret_finish(outs, ybl_ref, gl_ref, ol_ref)


def _ret_scratch():
    c = RET_CHUNK
    return [
        pltpu.VMEM((RET_HEADS, c, c), f32),
        pltpu.VMEM((RET_HEADS, c, V7X_LANES), f32),
        pltpu.VMEM((RET_HEADS, c, V7X_LANES), f32),
        pltpu.VMEM((RET_HEADS, RET_QK_DIM, RET_V_DIM), f32),
    ]


def _retention(p, lg_fwd, lg_bwd):
    c = RET_CHUNK
    nc = SEQ // c
    qcol, kcol = COL_QR // RET_QK_WIDTH, COL_KR // RET_QK_WIDTH
    vcol, gcol = COL_VR // RET_V_WIDTH, COL_GR // RET_V_WIDTH

    def lat_row(reverse):
        def f(b, t):
            s = jnp.maximum(t, 1)
            return b * nc + ((nc - s) if reverse else (s - 1))
        return f

    def specs(reverse, with_gate):
        row = lat_row(reverse)
        lat = [pl.BlockSpec((c, RET_QK_WIDTH), lambda b, t: (row(b, t), qcol)),
               pl.BlockSpec((c, RET_QK_WIDTH), lambda b, t: (row(b, t), kcol)),
               pl.BlockSpec((c, RET_V_WIDTH), lambda b, t: (row(b, t), vcol))]
        ctx = [pl.BlockSpec((c, RET_QK_WIDTH), lambda b, t: (CTX_BLK0 + b, qcol)),
               pl.BlockSpec((c, RET_QK_WIDTH), lambda b, t: (CTX_BLK0 + b, kcol)),
               pl.BlockSpec((c, RET_V_WIDTH), lambda b, t: (CTX_BLK0 + b, vcol))]
        if with_gate:
            lat += [pl.BlockSpec((c, RET_V_WIDTH), lambda b, t: (row(b, t), gcol)),
                    pl.BlockSpec((c, RET_V_WIDTH), lambda b, t: (row(b, t), 0))]
            ctx += [pl.BlockSpec((c, RET_V_WIDTH), lambda b, t: (CTX_BLK0 + b, gcol)),
                    pl.BlockSpec((c, RET_V_WIDTH), lambda b, t: (b, 0))]
        outs = [pl.BlockSpec((c, RET_V_WIDTH), lambda b, t: (row(b, t), 0)),
                pl.BlockSpec((c, RET_V_WIDTH), lambda b, t: (b, 0))]
        return [pl.BlockSpec(memory_space=pltpu.SMEM)] + lat + ctx, outs

    in_specs, out_specs = specs(True, False)
    yb_lat, yb_ctx = pl.pallas_call(
        _ret_bwd_kernel,
        grid=(BATCH, nc + 1),
        in_specs=in_specs,
        out_specs=out_specs,
        out_shape=[jax.ShapeDtypeStruct((N_LAT, RET_V_WIDTH), f32),
                   jax.ShapeDtypeStruct((N_CTX, RET_V_WIDTH), f32)],
        scratch_shapes=_ret_scratch(),
        compiler_params=_cparams(("arbitrary", "arbitrary")),
        name="retention_bwd",
    )(lg_bwd, p, p, p, p, p, p)

    in_specs, out_specs = specs(False, True)
    return pl.pallas_call(
        _ret_fwd_kernel,
        grid=(BATCH, nc + 1),
        in_specs=in_specs,
        out_specs=out_specs,
        out_shape=[jax.ShapeDtypeStruct((N_LAT, RET_V_WIDTH), bf16),
                   jax.ShapeDtypeStruct((N_CTX, RET_V_WIDTH), bf16)],
        scratch_shapes=_ret_scratch(),
        compiler_params=_cparams(("arbitrary", "arbitrary")),
        name="retention_fwd",
    )(lg_fwd, p, p, p, p, yb_lat, p, p, p, p, yb_ctx)


def _merge_out_kernel(ya_ref, yp_ref, yr_ref, ga_ref, gp_ref, gr_ref, wa_ref, wp_ref, wr_ref, wo_ref,
                      x_ref, mod_ref, o_ref, m_ref, *, n_col):
    j = pl.program_id(1)

    @pl.when(j < n_col)
    def _():
        acc = _sigmoid(ga_ref[...].astype(f32)) * _dot(ya_ref[...], wa_ref[...].astype(bf16))
        acc = acc + _sigmoid(gp_ref[...].astype(f32)) * _dot(yp_ref[...], wp_ref[...].astype(bf16))
        acc = acc + _sigmoid(gr_ref[...].astype(f32)) * _dot(yr_ref[...], wr_ref[...].astype(bf16))
        m_ref[j] = acc.astype(m_ref.dtype)

    @pl.when(j >= n_col)
    def _():
        y = None
        for c in range(n_col):
            part = _dot(m_ref[c], wo_ref[c * TN_MERGE:(c + 1) * TN_MERGE, :].astype(bf16))
            y = part if y is None else y + part
        o_ref[...] = x_ref[...] + mod_ref[0, 2:3, :] * y


def _merge_out(ya, yp, yr, p, w_a, w_p, w_r, w_out, x, mod_l, *, layer, tm, row_block0, n_tiles, mod_sel):
    n_col = D_MODEL // TN_MERGE
    gate0 = COL_GATES // TN_MERGE

    def first(j):
        return jnp.minimum(j, n_col - 1)

    def second(j):
        return jnp.maximum(j - n_col, 0)

    def gate_spec(br):
        return pl.BlockSpec((tm, TN_MERGE), lambda i, j: (row_block0 + i, gate0 + br * n_col + first(j)))

    return pl.pallas_call(
        functools.partial(_merge_out_kernel, n_col=n_col),
        grid=(n_tiles, 2 * n_col),
        in_specs=[
            pl.BlockSpec((tm, ATTN_Q_WIDTH), lambda i, j: (i, 0)),
            pl.BlockSpec((tm, POOL_WIDTH), lambda i, j: (i, 0)),
            pl.BlockSpec((tm, RET_V_WIDTH), lambda i, j: (i, 0)),
            gate_spec(0), gate_spec(1), gate_spec(2),
            pl.BlockSpec((None, ATTN_Q_WIDTH, TN_MERGE), lambda i, j: (layer, 0, first(j))),
            pl.BlockSpec((None, POOL_WIDTH, TN_MERGE), lambda i, j: (layer, 0, first(j))),
            pl.BlockSpec((None, RET_V_WIDTH, TN_MERGE), lambda i, j: (layer, 0, first(j))),
            pl.BlockSpec((None, D_MODEL, TN_MERGE), lambda i, j: (layer, 0, second(j))),
            pl.BlockSpec((tm, TN_MERGE), lambda i, j: (row_block0 + i, second(j))),
            pl.BlockSpec((1, N_MOD, TN_MERGE), lambda i, j: (mod_sel(i), 0, second(j))),
        ],
        out_specs=pl.BlockSpec((tm, TN_MERGE), lambda i, j: (row_block0 + i, second(j))),
        out_shape=jax.ShapeDtypeStruct(x.shape, f32),
        scratch_shapes=[pltpu.VMEM((n_col, tm, TN_MERGE), bf16)],
        input_output_aliases={10: 0},
        compiler_params=_cparams(("parallel", "arbitrary")),
        name="merge_out",
    )(ya, yp, yr, p, p, p, w_a, w_p, w_r, w_out, x, mod_l)


ROUTE_LANES = V7X_LANES
LANE_E1, LANE_E2, LANE_R1, LANE_R2, LANE_W1, LANE_W2 = 0, 1, 2, 3, 4, 5
PLAN_ROWS = 8
HI16 = 0xFFFF0000


def _pack_bf16_pairs(v):
    n = v.shape[1] // 2
    bits = lax.bitcast_convert_type(v.astype(bf16).astype(f32), u32)
    return (bits[:, :n] >> 16) | (bits[:, n:] & jnp.uint32(HI16))


def _unpack_bf16_pairs(words):
    return (lax.bitcast_convert_type(words << 16, f32),
            lax.bitcast_convert_type(words & jnp.uint32(HI16), f32))


def _route_kernel(x_ref, mod_ref, g_ref, w_ref, b_ref, h_ref, info_ref, plan_ref, cnt_ref,
                  hf_ref, tri_ref, run_ref):
    tm = TM_ROUTE
    step = pl.program_id(0)

    @pl.when(step == 0)
    def _():
        run_ref[...] = jnp.zeros_like(run_ref)
        r = lax.broadcasted_iota(i32, (tm, tm), 0)
        c = lax.broadcasted_iota(i32, (tm, tm), 1)
        tri_ref[...] = jnp.where(c < r, 1.0, 0.0).astype(bf16)

    g = g_ref[...]
    shift = mod_ref[0, 3:4, :]
    scale = mod_ref[0, 4:5, :]

    def body(r, carry):
        rows = pl.ds(pl.multiple_of(r * ROW_CHUNK, ROW_CHUNK), ROW_CHUNK)
        hf_ref[rows, :] = _norm_mod_rows(x_ref[rows, :], g, shift, scale).astype(bf16)
        return carry

    lax.fori_loop(0, tm // ROW_CHUNK, body, 0)

    for c in range(tm // ROW_CHUNK):
        words = _pack_bf16_pairs(hf_ref[c * ROW_CHUNK:(c + 1) * ROW_CHUNK, :])
        for s in range(XS_SUB):
            h_ref[pl.ds(c * ROW_CHUNK * XS_SUB + s, ROW_CHUNK, stride=XS_SUB), :] = (
                words[:, s * V7X_LANES:(s + 1) * V7X_LANES])

    logits = _dot(hf_ref[...], w_ref[...].astype(bf16)) + b_ref[...]
    lane = lax.broadcasted_iota(i32, (tm, ROUTE_LANES), 1)
    neg = jnp.float32(-jnp.inf)
    lane_f = lane.astype(f32)

    def first_argmax(vals):
        top = jnp.max(vals, axis=-1, keepdims=True)
        idx = jnp.min(jnp.where(vals == top, lane_f, float(ROUTE_LANES)), axis=-1, keepdims=True)
        return top, idx.astype(i32)

    g_logits = jnp.where(lane < MOE_GROUPS, logits, neg)
    g_top, grp = first_argmax(g_logits)
    p_grp = 1.0 / jnp.sum(jnp.exp(g_logits - g_top), axis=-1, keepdims=True)

    e_lane = lane - MOE_GROUPS
    in_grp = (e_lane >= 0) & (e_lane < MOE_EXPERTS) & (lax.shift_right_arithmetic(e_lane, 3) == grp)
    e_logits = jnp.where(in_grp, logits, neg)
    top1, idx1 = first_argmax(e_logits)
    e_logits2 = jnp.where(lane == idx1, neg, e_logits)
    top2, idx2 = first_argmax(e_logits2)
    a = jnp.exp(top2 - top1)
    w1 = (1.0 / (1.0 + a)) * p_grp
    w2 = (a / (1.0 + a)) * p_grp
    e1 = idx1 - MOE_GROUPS
    e2 = idx2 - MOE_GROUPS

    hot1 = lane == e1
    hot2 = lane == e2
    o1 = jnp.where(hot1, 1.0, 0.0)
    o2 = jnp.where(hot2, 1.0, 0.0)
    tri = tri_ref[...]
    before1 = _dot(tri, o1.astype(bf16))
    before2 = _dot(tri, o2.astype(bf16))
    c1 = jnp.sum(o1, axis=0, keepdims=True)
    c2 = jnp.sum(o2, axis=0, keepdims=True)
    run = run_ref[...]
    rank1 = jnp.sum(jnp.where(hot1, before1 + run, 0.0), axis=-1, keepdims=True)
    rank2 = jnp.sum(jnp.where(hot2, before2 + run + c1, 0.0), axis=-1, keepdims=True)
    run = run + c1 + c2
    run_ref[...] = run
    cnt_ref[...] = jnp.broadcast_to(run, cnt_ref.shape)

    info = jnp.where(lane == LANE_E1, e1.astype(f32), 0.0)
    info = jnp.where(lane == LANE_E2, e2.astype(f32), info)
    info = jnp.where(lane == LANE_W1, w1, info)
    info = jnp.where(lane == LANE_W2, w2, info)
    info = jnp.where(lane == LANE_R1, rank1, info)
    info = jnp.where(lane == LANE_R2, rank2, info)
    info_ref[...] = info
    plan_ref[0] = jnp.transpose(info)[0:PLAN_ROWS, :].astype(i32)


def _route(x, mod_l, g2, w_route, b_route, *, n_tiles, mod_sel):
    tm = TM_ROUTE
    return pl.pallas_call(
        _route_kernel,
        grid=(n_tiles,),
        in_specs=[
            pl.BlockSpec((tm, D_MODEL), lambda i: (i, 0)),
            pl.BlockSpec((1, N_MOD, D_MODEL), lambda i: (mod_sel(i), 0, 0)),
            pl.BlockSpec((1, D_MODEL), lambda i: (0, 0)),
            pl.BlockSpec((D_MODEL, ROUTE_LANES), lambda i: (0, 0)),
            pl.BlockSpec((1, ROUTE_LANES), lambda i: (0, 0)),
        ],
        out_specs=[
            pl.BlockSpec((tm * XS_SUB, V7X_LANES), lambda i: (i, 0)),
            pl.BlockSpec((tm, ROUTE_LANES), lambda i: (i, 0)),
            pl.BlockSpec((1, PLAN_ROWS, tm), lambda i: (i, 0, 0)),
            pl.BlockSpec((8, ROUTE_LANES), lambda i: (0, 0)),
        ],
        out_shape=[
            jax.ShapeDtypeStruct((n_tiles * tm * XS_SUB, V7X_LANES), u32),
            jax.ShapeDtypeStruct((n_tiles * tm, ROUTE_LANES), f32),
            jax.ShapeDtypeStruct((n_tiles, PLAN_ROWS, tm), i32),
            jax.ShapeDtypeStruct((8, ROUTE_LANES), f32),
        ],
        scratch_shapes=[
            pltpu.VMEM((tm, D_MODEL), bf16),
            pltpu.VMEM((tm, tm), bf16),
            pltpu.VMEM((1, ROUTE_LANES), f32),
        ],
        compiler_params=_cparams(("arbitrary",)),
        name="moe_route",
    )(x, mod_l, g2, w_route, b_route)


def _dest_kernel(pend_ref, plan_ref, o_ref):
    plan = plan_ref[...]
    start = jnp.zeros_like(plan)
    for e in range(MOE_EXPERTS):
        start = jnp.where(plan == e, pend_ref[e], start)
    o_ref[...] = start + pltpu.roll(plan, plan.shape[0] - MOE_TOP_K, 0)


def _dest(pend0, plan):
    rows = plan.shape[0] * PLAN_ROWS
    return pl.pallas_call(
        _dest_kernel,
        grid_spec=pltpu.PrefetchScalarGridSpec(
            num_scalar_prefetch=1,
            grid=(1,),
            in_specs=[pl.BlockSpec((rows, TM_ROUTE), lambda i, pend: (0, 0))],
            out_specs=pl.BlockSpec((rows, TM_ROUTE), lambda i, pend: (0, 0)),
        ),
        out_shape=jax.ShapeDtypeStruct((rows, TM_ROUTE), i32),
        compiler_params=_cparams(("arbitrary",)),
        name="moe_dest",
    )(pend0, plan.reshape(rows, TM_ROUTE))


def _dispatch_kernel(pend_ref, dest_ref, h_ref, xs_ref, zero_ref, sem):
    tm = TM_ROUTE
    step = pl.program_id(0)

    blk_rows = MOE_BLK * XS_SUB

    def zero_copy(e):
        start = pl.multiple_of((pend_ref[e + 1] - MOE_BLK) * XS_SUB, blk_rows)
        return pltpu.make_async_copy(zero_ref, xs_ref.at[pl.ds(start, blk_rows)], sem)

    def tail_copy(blk):
        start = pl.multiple_of(blk * blk_rows, blk_rows)
        return pltpu.make_async_copy(zero_ref, xs_ref.at[pl.ds(start, blk_rows)], sem)

    @pl.when(step == 0)
    def _():
        zero_ref[...] = jnp.zeros_like(zero_ref)
        first_unused = pend_ref[MOE_EXPERTS] // MOE_BLK
        for phase in range(2):
            def body(e, carry):
                @pl.when(pend_ref[e + 1] > pend_ref[e])
                def _():
                    if phase == 0:
                        zero_copy(e).start()
                    else:
                        zero_copy(e).wait()
                return carry
            lax.fori_loop(0, MOE_EXPERTS, body, 0)

            def tail(blk, carry):
                if phase == 0:
                    tail_copy(blk).start()
                else:
                    tail_copy(blk).wait()
                return carry
            lax.fori_loop(first_unused, MOE_ROWS // MOE_BLK, tail, 0)

    def row_copy(r, k):
        d = dest_ref[k, r]
        src = h_ref.at[pl.ds(pl.multiple_of(r * XS_SUB, XS_SUB), XS_SUB)]
        return pltpu.make_async_copy(src, xs_ref.at[pl.ds(pl.multiple_of(d * XS_SUB, XS_SUB), XS_SUB)], sem)

    def start(r, carry):
        row_copy(r, 0).start(priority=0)
        row_copy(r, 1).start(priority=1)
        return carry

    lax.fori_loop(0, tm, start, 0, unroll=4)
    for k in range(MOE_TOP_K):
        pltpu.make_async_copy(h_ref, xs_ref.at[pl.ds(0, tm * XS_SUB)], sem).wait()


def _dispatch(pend0, dest, h_packed, *, n_tiles):
    tm = TM_ROUTE
    return pl.pallas_call(
        _dispatch_kernel,
        grid_spec=pltpu.PrefetchScalarGridSpec(
            num_scalar_prefetch=1,
            grid=(n_tiles,),
            in_specs=[
                pl.BlockSpec((PLAN_ROWS, tm), lambda i, pend: (i, 0), memory_space=pltpu.SMEM),
                pl.BlockSpec((tm * XS_SUB, V7X_LANES), lambda i, pend: (i, 0)),
            ],
            out_specs=pl.BlockSpec(memory_space=pl.ANY),
            scratch_shapes=[pltpu.VMEM((MOE_BLK * XS_SUB, V7X_LANES), u32), pltpu.SemaphoreType.DMA(())],
        ),
        out_shape=jax.ShapeDtypeStruct((MOE_ROWS * XS_SUB, V7X_LANES), u32),
        compiler_params=_cparams(("arbitrary",)),
        name="moe_dispatch",
    )(pend0, dest, h_packed)


WEIGHT_SLOTS = 3


def _expert_kernel(be_ref, nv_ref, first_ref, slot_ref, next_ref, next2_ref, xs_ref, wg_hbm, wu_hbm, wd_hbm,
                   ys_ref, wg_buf, wu_buf, wd_buf, wg16, wu16, wd16, sems, *, layer):
    i = pl.program_id(0)

    def weight_copies(e, s):
        return (pltpu.make_async_copy(wg_hbm.at[layer, e], wg_buf.at[s], sems.at[s, 0]),
                pltpu.make_async_copy(wu_hbm.at[layer, e], wu_buf.at[s], sems.at[s, 1]),
                pltpu.make_async_copy(wd_hbm.at[layer, e], wd_buf.at[s], sems.at[s, 2]))

    def swiglu_block(wg, wu, wd):
        parts = [_unpack_bf16_pairs(xs_ref[pl.ds(t, MOE_BLK, stride=XS_SUB), :]) for t in range(XS_SUB)]
        x = jnp.concatenate([p[0] for p in parts] + [p[1] for p in parts], axis=1).astype(bf16)
        gate = _dot(x, wg)
        up = _dot(x, wu)
        h = gate * _sigmoid(gate) * up
        words = _pack_bf16_pairs(_dot(h.astype(bf16), wd))
        for t in range(XS_SUB):
            ys_ref[pl.ds(t, MOE_BLK, stride=XS_SUB), :] = words[:, t * V7X_LANES:(t + 1) * V7X_LANES]

    @pl.when(i < nv_ref[0])
    def _():
        s = slot_ref[i]

        @pl.when(first_ref[i] == 1)
        def _():
            @pl.when(i == 0)
            def _():
                for cp in weight_copies(be_ref[0], 0):
                    cp.start()

                @pl.when(next_ref[0] >= 0)
                def _():
                    for cp in weight_copies(next_ref[0], 1):
                        cp.start()
            for cp in weight_copies(be_ref[i], s):
                cp.wait()

            @pl.when(next2_ref[i] >= 0)
            def _():
                for cp in weight_copies(next2_ref[i], (s + 2) % WEIGHT_SLOTS):
                    cp.start()

            wg, wu, wd = wg_buf[s].astype(bf16), wu_buf[s].astype(bf16), wd_buf[s].astype(bf16)
            wg16[...] = wg
            wu16[...] = wu
            wd16[...] = wd
            swiglu_block(wg, wu, wd)

        @pl.when(first_ref[i] == 0)
        def _():
            swiglu_block(wg16[...], wu16[...], wd16[...])

    @pl.when(i >= nv_ref[0])
    def _():
        ys_ref[...] = jnp.zeros_like(ys_ref)


def _experts(block_expert, n_valid, first, slot, nxt, nxt2, xs, w_gate, w_up, w_down, *, layer):
    n_blocks = MOE_ROWS // MOE_BLK

    def row(i, be, nv, *_):
        return jnp.minimum(i, nv[0] - 1)

    return pl.pallas_call(
        functools.partial(_expert_kernel, layer=layer),
        grid_spec=pltpu.PrefetchScalarGridSpec(
            num_scalar_prefetch=6,
            grid=(n_blocks,),
            in_specs=[
                pl.BlockSpec((MOE_BLK * XS_SUB, V7X_LANES), lambda i, *pf: (row(i, *pf), 0)),
                pl.BlockSpec(memory_space=pl.ANY),
                pl.BlockSpec(memory_space=pl.ANY),
                pl.BlockSpec(memory_space=pl.ANY),
            ],
            out_specs=pl.BlockSpec((MOE_BLK * XS_SUB, V7X_LANES), lambda i, *pf: (i, 0)),
            scratch_shapes=[
                pltpu.VMEM((WEIGHT_SLOTS, D_MODEL, MOE_D_FF), f32),
                pltpu.VMEM((WEIGHT_SLOTS, D_MODEL, MOE_D_FF), f32),
                pltpu.VMEM((WEIGHT_SLOTS, MOE_D_FF, D_MODEL), f32),
                pltpu.VMEM((D_MODEL, MOE_D_FF), bf16),
                pltpu.VMEM((D_MODEL, MOE_D_FF), bf16),
                pltpu.VMEM((MOE_D_FF, D_MODEL), bf16),
                pltpu.SemaphoreType.DMA((WEIGHT_SLOTS, 3)),
            ],
        ),
        out_shape=jax.ShapeDtypeStruct((MOE_ROWS * XS_SUB, V7X_LANES), u32),
        compiler_params=_cparams(("arbitrary",)),
        name="moe_experts",
    )(block_expert, n_valid, first, slot, nxt, nxt2, xs, w_gate, w_up, w_down)


def _combine_kernel(dest0_ref, destn_ref, x_ref, info_ref, mod_ref, modn_ref, gn_ref, ys_ref,
                    o_ref, hn_ref, buf_ref, sems, *, n_tiles, modulated):
    tm = TM_COMBINE
    i = pl.program_id(0)

    def start_tile(dest_ref, slot):
        def row_copy(r, k):
            d = dest_ref[k, r]
            src = ys_ref.at[pl.ds(pl.multiple_of(d * XS_SUB, XS_SUB), XS_SUB)]
            dst = buf_ref.at[slot, k, pl.ds(pl.multiple_of(r * XS_SUB, XS_SUB), XS_SUB)]
            return pltpu.make_async_copy(src, dst, sems.at[slot])

        def body(r, carry):
            row_copy(r, 0).start(priority=0)
            row_copy(r, 1).start(priority=1)
            return carry

        lax.fori_loop(0, tm, body, 0, unroll=4)

    @pl.when(i == 0)
    def _():
        start_tile(dest0_ref, 0)

    slot = i % 2

    @pl.when(i + 1 < n_tiles)
    def _():
        start_tile(destn_ref, 1 - slot)

    for k in range(MOE_TOP_K):
        pltpu.make_async_copy(ys_ref.at[pl.ds(0, tm * XS_SUB)], buf_ref.at[slot, k], sems.at[slot]).wait()

    info = info_ref[...]
    w1 = info[:, LANE_W1:LANE_W1 + 1]
    w2 = info[:, LANE_W2:LANE_W2 + 1]
    for t in range(XS_SUB):
        rows = pl.ds(t, tm, stride=XS_SUB)
        lo1, hi1 = _unpack_bf16_pairs(buf_ref[slot, 0, rows, :])
        lo2, hi2 = _unpack_bf16_pairs(buf_ref[slot, 1, rows, :])
        for half, (y1, y2) in enumerate(((lo1, lo2), (hi1, hi2))):
            cols = slice(half * PACKED + t * V7X_LANES, half * PACKED + (t + 1) * V7X_LANES)
            o_ref[:, cols] = x_ref[:, cols] + mod_ref[0, 5:6, cols] * (y1 * w1 + y2 * w2)

    g = gn_ref[...]
    for c in range(tm // ROW_CHUNK):
        rows = slice(c * ROW_CHUNK, (c + 1) * ROW_CHUNK)
        if modulated:
            y = _norm_mod_rows(o_ref[rows, :], g, modn_ref[0, 0:1, :], modn_ref[0, 1:2, :])
        else:
            xr = o_ref[rows, :]
            y = xr * lax.rsqrt(jnp.mean(xr * xr, axis=-1, keepdims=True) + NORM_EPS) * g
        hn_ref[rows, :] = y.astype(hn_ref.dtype)


def _combine(dest, x, info, mod_l, mod_next, g_next, ys, *, n_tiles, mod_sel):
    tm = TM_COMBINE
    per = TM_ROUTE // tm
    modulated = mod_next is not None
    if not modulated:
        mod_next = mod_l

    def dest_spec(tile):
        return pl.BlockSpec((PLAN_ROWS, tm), lambda i: (tile(i) // per, tile(i) % per), memory_space=pltpu.SMEM)

    return pl.pallas_call(
        functools.partial(_combine_kernel, n_tiles=n_tiles, modulated=modulated),
        grid=(n_tiles,),
        in_specs=[
            dest_spec(lambda i: 0),
            dest_spec(lambda i: jnp.minimum(i + 1, n_tiles - 1)),
            pl.BlockSpec((tm, D_MODEL), lambda i: (i, 0)),
            pl.BlockSpec((tm, ROUTE_LANES), lambda i: (i, 0)),
            pl.BlockSpec((1, N_MOD, D_MODEL), lambda i: (mod_sel(i), 0, 0)),
            pl.BlockSpec((1, N_MOD, D_MODEL), lambda i: (mod_sel(i), 0, 0)),
            pl.BlockSpec((1, D_MODEL), lambda i: (0, 0)),
            pl.BlockSpec(memory_space=pl.ANY),
        ],
        out_specs=[pl.BlockSpec((tm, D_MODEL), lambda i: (i, 0)),
                   pl.BlockSpec((tm, D_MODEL), lambda i: (i, 0))],
        scratch_shapes=[pltpu.VMEM((2, MOE_TOP_K, tm * XS_SUB, V7X_LANES), u32),
                        pltpu.SemaphoreType.DMA((2,))],
        out_shape=[jax.ShapeDtypeStruct(x.shape, f32),
                   jax.ShapeDtypeStruct((n_tiles * tm, D_MODEL), bf16 if modulated else f32)],
        input_output_aliases={2: 0},
        compiler_params=_cparams(("arbitrary",)),
        name="moe_combine",
    )(dest, dest, x, info, mod_l, mod_next, g_next, ys)


def _moe(x, mod_l, g2, w_route, b_route, w_gate, w_up, w_down, mod_next, g_next, *, layer, with_ctx):
    n_tok = N_TOK if with_ctx else N_LAT
    rt = n_tok // TM_ROUTE
    lat_rt = N_LAT // TM_ROUTE
    h_packed, info, plan, cnt = _route(
        x, mod_l, g2, w_route, b_route, n_tiles=rt,
        mod_sel=lambda i: jnp.where(i < lat_rt, i // (SEQ // TM_ROUTE), BATCH))

    counts = cnt[0, :MOE_EXPERTS].astype(i32)
    padded = (counts + MOE_BLK - 1) // MOE_BLK * MOE_BLK
    pend = jnp.cumsum(padded)
    pend0 = jnp.concatenate([jnp.zeros((1,), i32), pend]).astype(i32)
    n_blocks = MOE_ROWS // MOE_BLK
    blk_id = jnp.arange(n_blocks, dtype=i32)

    def expert_at(row):
        return jnp.minimum(jnp.sum((pend[None, :] <= row[:, None]).astype(i32), axis=1), MOE_EXPERTS - 1)

    block_expert = expert_at(blk_id * MOE_BLK)
    n_valid = (pend[-1:] // MOE_BLK).astype(i32)
    valid = blk_id < n_valid[0]
    changed = jnp.concatenate([jnp.ones((1,), bool), block_expert[1:] != block_expert[:-1]])
    first = (changed & valid).astype(i32)
    slot = ((jnp.cumsum(first) - 1) % WEIGHT_SLOTS).astype(i32)

    def following(expert):
        end = jnp.sum(jnp.where(jnp.arange(MOE_EXPERTS, dtype=i32)[None, :] == expert[:, None],
                                pend[None, :], 0), axis=1)
        return jnp.where((expert >= 0) & (end < pend[-1]), expert_at(end), -1).astype(i32)

    nxt = following(block_expert)
    nxt2 = following(nxt)

    dest = _dest(pend0, plan)
    xs = _dispatch(pend0, dest, h_packed, n_tiles=rt)
    ys = _experts(block_expert, n_valid, first, slot, nxt, nxt2, xs, w_gate, w_up, w_down, layer=layer)
    ct = n_tok // TM_COMBINE
    lat_ct = N_LAT // TM_COMBINE
    return _combine(dest, x, info, mod_l, mod_next, g_next, ys, n_tiles=ct,
                    mod_sel=lambda i: jnp.where(i < lat_ct, i // (SEQ // TM_COMBINE), BATCH))


def _rope_tables():
    rows = SEQ // GRID_W
    row = jnp.repeat(jnp.arange(rows, dtype=f32), GRID_W)
    col = jnp.tile(jnp.arange(GRID_W, dtype=f32), rows)
    n_freq = HEAD_DIM // 4
    inv_freq = ROPE_BASE ** (-jnp.arange(n_freq, dtype=f32) / n_freq)
    ang = jnp.concatenate([row[:, None] * inv_freq, col[:, None] * inv_freq], axis=-1)
    cos, sin = jnp.cos(ang), jnp.sin(ang)
    return jnp.concatenate([cos, cos], axis=-1), jnp.concatenate([-sin, sin], axis=-1)


def kernel(x, c, ctx, c_ctx, w_mod, b_mod, norm1_g, norm2_g, w_in, attn_sink, pool_w, pool_scale, ret_decay_fwd, ret_decay_bwd, w_br_attn, w_br_pool, w_br_ret, w_out, w_route_group, b_route_group, w_route_expert, b_route_expert, w_expert_gate, w_expert_up, w_expert_down, final_norm_g):
    assert x.shape == (BATCH, SEQ, D_MODEL) and ctx.shape == (BATCH, CTX_LEN, D_MODEL)
    cos_full, sin_signed = _rope_tables()

    cond8 = jnp.concatenate([c, c_ctx[None, :], jnp.zeros((8 - BATCH - 1, D_MODEL), f32)], axis=0)
    mod = _modulation(cond8, w_mod, b_mod).reshape(DEPTH, 8, N_MOD, D_MODEL)

    lg_fwd = jax.nn.log_sigmoid(ret_decay_fwd.astype(f32))
    lg_bwd = jax.nn.log_sigmoid(ret_decay_bwd.astype(f32))
    pad = ROUTE_LANES - MOE_GROUPS - MOE_EXPERTS
    w_route = jnp.concatenate(
        [w_route_group, w_route_expert, jnp.zeros((DEPTH, D_MODEL, pad), f32)], axis=-1)
    b_route = jnp.concatenate([b_route_group, b_route_expert, jnp.zeros((DEPTH, pad), f32)], axis=-1)

    tok = jnp.concatenate([x.reshape(N_LAT, D_MODEL), ctx.reshape(N_CTX, D_MODEL)], axis=0)
    ctx_block0 = N_LAT // TM_CTX
    sel_ctx = lambda i: BATCH

    def sel_rows(tm):
        return lambda i: jnp.where(i < N_LAT // tm, i // (SEQ // tm), BATCH)

    h1 = _norm(tok, mod[0], norm1_g[0][None, :], n_tiles=N_TOK // TM_NORM, mod_sel=sel_rows(TM_NORM))
    for l in range(DEPTH):
        with_ctx = l < DEPTH - 1
        mod_l = mod[l]
        p = _proj(h1, w_in, layer=l)

        ya_lat = _attn_lat(p, attn_sink[l], cos_full, sin_signed)
        yp_lat = _pool(p, pool_w[l], pool_scale[l][None, :], n_seq=BATCH, seq_len=SEQ, seq0=0)
        yr_lat, yr_ctx = _retention(p, lg_fwd[l], lg_bwd[l])

        tok = _merge_out(ya_lat, yp_lat, yr_lat, p, w_br_attn, w_br_pool, w_br_ret, w_out, tok, mod_l,
                         layer=l, tm=TM_MERGE, row_block0=0, n_tiles=N_LAT // TM_MERGE,
                         mod_sel=sel_rows(TM_MERGE))
        if with_ctx:
            ya_ctx = _attn_ctx(p, attn_sink[l])
            yp_ctx = _pool(p, pool_w[l], pool_scale[l][None, :], n_seq=BATCH, seq_len=CTX_LEN, seq0=CTX_BLK0)
            tok = _merge_out(ya_ctx, yp_ctx, yr_ctx, p, w_br_attn, w_br_pool, w_br_ret, w_out, tok, mod_l,
                             layer=l, tm=TM_CTX, row_block0=ctx_block0, n_tiles=1, mod_sel=sel_ctx)

        mod_next = mod[l + 1] if with_ctx else None
        g_next = norm1_g[l + 1][None, :] if with_ctx else final_norm_g[None, :]
        tok, h1 = _moe(tok, mod_l, norm2_g[l][None, :], w_route[l], b_route[l][None, :],
                       w_expert_gate, w_expert_up, w_expert_down, mod_next, g_next, layer=l, with_ctx=with_ctx)

    return h1.reshape(BATCH, SEQ, D_MODEL)
```

```python
import functools

import jax
import jax.numpy as jnp
from jax import lax
from jax.experimental import pallas as pl
from jax.experimental.pallas import tpu as pltpu

f32 = jnp.float32
bf16 = jnp.bfloat16
i32 = jnp.int32
u32 = jnp.uint32

D_MODEL = 2048
BATCH = 2
SEQ = 4096
DEPTH = 4
GRID_W = 64
CTX_LEN = 256
NORM_EPS = 1e-6
N_MOD = 6
HEAD_DIM = 128
ATTN_HEADS = 8
ATTN_KV_HEADS = 2
ATTN_GROUP = ATTN_HEADS // ATTN_KV_HEADS
ATTN_BLOCK = 128
ROPE_BASE = 10000.0
LOG2_E = 1.4426950408889634
POOL_GROUPS = 4
POOL_GROUP_DIM = 128
POOL_WIDTH = POOL_GROUPS * POOL_GROUP_DIM
POOL_SPANS = (2, 4, 8, 16)
RET_HEADS = 4
RET_QK_DIM = 64
RET_V_DIM = 128
N_BRANCHES = 3
ATTN_Q_WIDTH = ATTN_HEADS * HEAD_DIM
ATTN_KV_WIDTH = ATTN_KV_HEADS * HEAD_DIM
RET_QK_WIDTH = RET_HEADS * RET_QK_DIM
RET_V_WIDTH = RET_HEADS * RET_V_DIM
IN_SIZES = (ATTN_Q_WIDTH, ATTN_KV_WIDTH, ATTN_KV_WIDTH, POOL_WIDTH,
            RET_QK_WIDTH, RET_QK_WIDTH, RET_V_WIDTH, RET_V_WIDTH, N_BRANCHES * D_MODEL)
IN_WIDTH = sum(IN_SIZES)
MOE_GROUPS = 4
MOE_EXPERTS_PER_GROUP = 8
MOE_EXPERTS = MOE_GROUPS * MOE_EXPERTS_PER_GROUP
MOE_TOP_K = 2
MOE_D_FF = 512

COL_QA = 0
COL_KA = COL_QA + ATTN_Q_WIDTH
COL_VA = COL_KA + ATTN_KV_WIDTH
COL_U = COL_VA + ATTN_KV_WIDTH
COL_QR = COL_U + POOL_WIDTH
COL_KR = COL_QR + RET_QK_WIDTH
COL_VR = COL_KR + RET_QK_WIDTH
COL_GR = COL_VR + RET_V_WIDTH
COL_GATES = COL_GR + RET_V_WIDTH

V7X_LANES = 128
V7X_VMEM_LIMIT_BYTES = 56 * 1024 * 1024

N_LAT = BATCH * SEQ
N_CTX = BATCH * CTX_LEN
N_TOK = N_LAT + N_CTX
TM_PROJ = N_TOK // 4
CTX_BLK0 = N_LAT // CTX_LEN
TM_MERGE = 2048
TM_CTX = N_CTX
TM_NORM = 512
TN_PROJ = 512
TN_MERGE = 256
TN_MOD = 2048
ROW_CHUNK = 128
RET_CHUNK = 256
TM_ROUTE = 512
TM_COMBINE = 512
MOE_BLK = 256
MOE_ROWS = (N_TOK * MOE_TOP_K + MOE_EXPERTS * (MOE_BLK - 1)) // MOE_BLK * MOE_BLK
PACKED = D_MODEL // 2
XS_SUB = PACKED // V7X_LANES
YS_SUB = D_MODEL // V7X_LANES

assert CTX_LEN == RET_CHUNK and SEQ % RET_CHUNK == 0
assert IN_WIDTH % TN_PROJ == 0 and COL_GATES % TN_MERGE == 0


def _cparams(sem, vmem=V7X_VMEM_LIMIT_BYTES):
    return pltpu.CompilerParams(dimension_semantics=sem, vmem_limit_bytes=vmem)


def _dot(a, b):
    return jnp.dot(a, b, preferred_element_type=f32)


def _dot_nt(a, b):
    return lax.dot_general(a, b, (((1,), (1,)), ((), ())), preferred_element_type=f32)


def _sigmoid(x):
    return 0.5 * jnp.tanh(0.5 * x) + 0.5


def _dot_tn(a, b):
    return lax.dot_general(a, b, (((0,), (0,)), ((), ())), preferred_element_type=f32)


def _mod_kernel(cond_ref, w_ref, b_ref, o_ref):
    c = cond_ref[...]
    s = (c * jax.nn.sigmoid(c)).astype(bf16)
    o_ref[0] = _dot(s, w_ref[0].astype(bf16)) + b_ref[0]


def _modulation(cond8, w_mod, b_mod):
    n = N_MOD * D_MODEL
    return pl.pallas_call(
        _mod_kernel,
        grid=(DEPTH, n // TN_MOD),
        in_specs=[
            pl.BlockSpec((8, D_MODEL), lambda l, j: (0, 0)),
            pl.BlockSpec((1, D_MODEL, TN_MOD), lambda l, j: (l, 0, j)),
            pl.BlockSpec((1, 1, TN_MOD), lambda l, j: (l, 0, j)),
        ],
        out_specs=pl.BlockSpec((1, 8, TN_MOD), lambda l, j: (l, 0, j)),
        out_shape=jax.ShapeDtypeStruct((DEPTH, 8, n), f32),
        compiler_params=_cparams(("parallel", "parallel")),
        name="modulation",
    )(cond8, w_mod, b_mod.reshape(DEPTH, 1, n))


def _norm_mod_rows(x, g, shift, scale):
    ms = jnp.mean(x * x, axis=-1, keepdims=True)
    y = x * lax.rsqrt(ms + NORM_EPS) * g
    return y * (1.0 + scale) + shift


def _norm_kernel(x_ref, mod_ref, g_ref, o_ref, *, tm):
    g = g_ref[...]
    shift = mod_ref[0, 0:1, :]
    scale = mod_ref[0, 1:2, :]

    def body(r, carry):
        rows = pl.ds(pl.multiple_of(r * ROW_CHUNK, ROW_CHUNK), ROW_CHUNK)
        o_ref[rows, :] = _norm_mod_rows(x_ref[rows, :], g, shift, scale).astype(bf16)
        return carry

    lax.fori_loop(0, tm // ROW_CHUNK, body, 0)


def _norm(x, mod_l, g1, *, n_tiles, mod_sel):
    tm = TM_NORM
    return pl.pallas_call(
        functools.partial(_norm_kernel, tm=tm),
        grid=(n_tiles,),
        in_specs=[
            pl.BlockSpec((tm, D_MODEL), lambda i: (i, 0)),
            pl.BlockSpec((1, N_MOD, D_MODEL), lambda i: (mod_sel(i), 0, 0)),
            pl.BlockSpec((1, D_MODEL), lambda i: (0, 0)),
        ],
        out_specs=pl.BlockSpec((tm, D_MODEL), lambda i: (i, 0)),
        out_shape=jax.ShapeDtypeStruct((n_tiles * tm, D_MODEL), bf16),
        compiler_params=_cparams(("parallel",)),
        name="norm1",
    )(x, mod_l, g1)


def _proj_kernel(h_ref, w_ref, o_ref):
    o_ref[...] = _dot(h_ref[...], w_ref[...].astype(bf16)).astype(o_ref.dtype)


def _proj(h, w_in, *, layer):
    tm = TM_PROJ
    return pl.pallas_call(
        _proj_kernel,
        grid=(N_TOK // tm, IN_WIDTH // TN_PROJ),
        in_specs=[
            pl.BlockSpec((tm, D_MODEL), lambda i, j: (i, 0)),
            pl.BlockSpec((None, D_MODEL, TN_PROJ), lambda i, j: (layer, 0, j)),
        ],
        out_specs=pl.BlockSpec((tm, TN_PROJ), lambda i, j: (i, j)),
        out_shape=jax.ShapeDtypeStruct((N_TOK, IN_WIDTH), bf16),
        compiler_params=_cparams(("parallel", "arbitrary")),
        name="in_proj",
    )(h, w_in)


def _rope(t, cos_full, sin_signed):
    return t * cos_full + pltpu.roll(t, HEAD_DIM // 2, 1) * sin_signed


def _sink_column(sink_ref, g, rows_per_head):
    n = ATTN_GROUP * rows_per_head
    row = lax.broadcasted_iota(i32, (n, 1), 0)
    col = jnp.full((n, 1), sink_ref[g * ATTN_GROUP + ATTN_GROUP - 1], f32)
    for j in range(ATTN_GROUP - 2, -1, -1):
        col = jnp.where(row < (j + 1) * rows_per_head, sink_ref[g * ATTN_GROUP + j], col)
    return col


def _softmax_pv(scores, values, sink_col):
    def lane_chunks(arrs):
        return [a[:, c * V7X_LANES:(c + 1) * V7X_LANES] for a in arrs for c in range(a.shape[1] // V7X_LANES)]

    folded = functools.reduce(jnp.maximum, lane_chunks(scores))
    m = jnp.maximum(sink_col, jnp.max(folded, axis=-1, keepdims=True))
    es = [jnp.exp2(s - m) for s in scores]
    total = functools.reduce(lambda a, b: a + b, lane_chunks(es))
    denom = jnp.exp2(sink_col - m) + jnp.sum(total, axis=-1, keepdims=True)
    out = None
    for e, v in zip(es, values):
        o = _dot(e.astype(bf16), v)
        out = o if out is None else out + o
    return out * (1.0 / denom)


def _attn_lat_kernel(sink_ref, q_ref, kp_ref, km_ref, kn_ref, vp_ref, vm_ref, vn_ref, kx_ref, vx_ref,
                     cm_ref, sm_ref, cp_ref, sp_ref, cn_ref, sn_ref, o_ref, *, n_steps):
    step = pl.program_id(1)
    blk = ATTN_BLOCK
    scale = HEAD_DIM ** -0.5 * LOG2_E
    cm, sm = cm_ref[...], sm_ref[...]
    rows = ATTN_GROUP * blk
    qi = lax.broadcasted_iota(i32, (rows, blk), 0) % blk
    kj = lax.broadcasted_iota(i32, (rows, blk), 1)
    neg = jnp.float32(-jnp.inf)
    keep_prev = [(kj >= qi) & (step > 0), kj >= qi]
    keep_next = [kj <= qi, (kj <= qi) & (step < n_steps - 1)]

    for g in range(ATTN_KV_HEADS):
        kv = slice(g * HEAD_DIM, (g + 1) * HEAD_DIM)
        k_mid = _rope(km_ref[:, kv].astype(f32), cm, sm).astype(bf16)
        k_blocks = [_rope(kp_ref[:, kv].astype(f32), cp_ref[...], sp_ref[...]).astype(bf16),
                    k_mid[:blk], k_mid[blk:],
                    _rope(kn_ref[:, kv].astype(f32), cn_ref[...], sn_ref[...]).astype(bf16)]
        v_blocks = [vp_ref[:, kv], vm_ref[:blk, kv], vm_ref[blk:, kv], vn_ref[:, kv]]
        k_ctx, v_ctx = kx_ref[:, kv], vx_ref[:, kv]
        sink_col = _sink_column(sink_ref, g, blk) * LOG2_E
        for half in range(2):
            qrows = slice(half * blk, (half + 1) * blk)
            cq, sq = cm[qrows], sm[qrows]
            qs = jnp.concatenate(
                [(_rope(q_ref[qrows, (g * ATTN_GROUP + j) * HEAD_DIM:(g * ATTN_GROUP + j + 1) * HEAD_DIM]
                        .astype(f32), cq, sq) * scale).astype(bf16) for j in range(ATTN_GROUP)],
                axis=0)
            s_prev = jnp.where(keep_prev[half], _dot_nt(qs, k_blocks[half]), neg)
            s_cur = _dot_nt(qs, k_blocks[half + 1])
            s_next = jnp.where(keep_next[half], _dot_nt(qs, k_blocks[half + 2]), neg)
            s_ctx = _dot_nt(qs, k_ctx)
            out = _softmax_pv([s_prev, s_cur, s_next, s_ctx],
                              [v_blocks[half], v_blocks[half + 1], v_blocks[half + 2], v_ctx], sink_col)
            for j in range(ATTN_GROUP):
                h = g * ATTN_GROUP + j
                o_ref[qrows, h * HEAD_DIM:(h + 1) * HEAD_DIM] = out[j * blk:(j + 1) * blk, :].astype(o_ref.dtype)


def _attn_lat(p, sink, cos_full, sin_signed):
    nb = SEQ // ATTN_BLOCK
    blk = ATTN_BLOCK
    pair = 2 * blk
    n_steps = nb // 2
    kcol = COL_KA // ATTN_KV_WIDTH
    vcol = COL_VA // ATTN_KV_WIDTH

    def edge_block(s, shift):
        return jnp.clip(2 * s + shift, 0, nb - 1)

    def edge_spec(col, shift):
        return pl.BlockSpec((blk, ATTN_KV_WIDTH), lambda b, s: (b * nb + edge_block(s, shift), col))

    def mid_spec(col):
        return pl.BlockSpec((pair, ATTN_KV_WIDTH), lambda b, s: (b * n_steps + s, col))

    def edge_tab(shift):
        return pl.BlockSpec((blk, HEAD_DIM), lambda b, s: (edge_block(s, shift), 0))

    mid_tab = pl.BlockSpec((pair, HEAD_DIM), lambda b, s: (s, 0))
    return pl.pallas_call(
        functools.partial(_attn_lat_kernel, n_steps=n_steps),
        grid=(BATCH, n_steps),
        in_specs=[
            pl.BlockSpec(memory_space=pltpu.SMEM),
            pl.BlockSpec((pair, ATTN_Q_WIDTH), lambda b, s: (b * n_steps + s, 0)),
            edge_spec(kcol, -1), mid_spec(kcol), edge_spec(kcol, 2),
            edge_spec(vcol, -1), mid_spec(vcol), edge_spec(vcol, 2),
            pl.BlockSpec((CTX_LEN, ATTN_KV_WIDTH), lambda b, s: (CTX_BLK0 + b, kcol)),
            pl.BlockSpec((CTX_LEN, ATTN_KV_WIDTH), lambda b, s: (CTX_BLK0 + b, vcol)),
            mid_tab, mid_tab, edge_tab(-1), edge_tab(-1), edge_tab(2), edge_tab(2),
        ],
        out_specs=pl.BlockSpec((pair, ATTN_Q_WIDTH), lambda b, s: (b * n_steps + s, 0)),
        out_shape=jax.ShapeDtypeStruct((N_LAT, ATTN_Q_WIDTH), bf16),
        compiler_params=_cparams(("parallel", "arbitrary")),
        name="attn_lat",
    )(sink, p, p, p, p, p, p, p, p, p,
      cos_full, sin_signed, cos_full, sin_signed, cos_full, sin_signed)


def _attn_ctx_kernel(sink_ref, q_ref, kx_ref, vx_ref, o_ref):
    g = pl.program_id(1)
    scale = HEAD_DIM ** -0.5 * LOG2_E
    q = q_ref[...].astype(f32)
    qs = jnp.concatenate(
        [(q[:, j * HEAD_DIM:(j + 1) * HEAD_DIM] * scale).astype(bf16) for j in range(ATTN_GROUP)], axis=0)
    s_ctx = _dot_nt(qs, kx_ref[...])
    out = _softmax_pv([s_ctx], [vx_ref[...]], _sink_column(sink_ref, g, CTX_LEN) * LOG2_E)
    for j in range(ATTN_GROUP):
        o_ref[:, j * HEAD_DIM:(j + 1) * HEAD_DIM] = out[j * CTX_LEN:(j + 1) * CTX_LEN, :].astype(o_ref.dtype)


def _attn_ctx(p, sink):
    gw = ATTN_GROUP * HEAD_DIM
    kcol = COL_KA // HEAD_DIM
    vcol = COL_VA // HEAD_DIM
    return pl.pallas_call(
        _attn_ctx_kernel,
        grid=(BATCH, ATTN_KV_HEADS),
        in_specs=[
            pl.BlockSpec(memory_space=pltpu.SMEM),
            pl.BlockSpec((CTX_LEN, gw), lambda b, g: (CTX_BLK0 + b, g)),
            pl.BlockSpec((CTX_LEN, HEAD_DIM), lambda b, g: (CTX_BLK0 + b, kcol + g)),
            pl.BlockSpec((CTX_LEN, HEAD_DIM), lambda b, g: (CTX_BLK0 + b, vcol + g)),
        ],
        out_specs=pl.BlockSpec((CTX_LEN, gw), lambda b, g: (b, g)),
        out_shape=jax.ShapeDtypeStruct((N_CTX, ATTN_Q_WIDTH), bf16),
        compiler_params=_cparams(("parallel", "parallel")),
        name="attn_ctx",
    )(sink, p, p, p)


POOL_HALO = 8
POOL_ROWS = 256


def _pool_kernel(u_ref, w_ref, s_ref, o_ref, pad_ref, *, seq_len):
    zeros = jnp.zeros((POOL_HALO, POOL_GROUP_DIM), f32)
    for gi in range(POOL_GROUPS):
        radius = POOL_SPANS[gi] // 2
        cols = slice(gi * POOL_GROUP_DIM, (gi + 1) * POOL_GROUP_DIM)
        pad_ref[0:POOL_HALO, :] = zeros
        pad_ref[POOL_HALO + seq_len:POOL_HALO + seq_len + POOL_HALO, :] = zeros
        pad_ref[POOL_HALO:POOL_HALO + seq_len, :] = u_ref[:, cols].astype(f32)
        w = w_ref[gi].astype(bf16)
        sc = s_ref[:, cols]
        for c in range(seq_len // POOL_ROWS):
            base = c * POOL_ROWS
            acc = pad_ref[POOL_HALO + base:POOL_HALO + base + POOL_ROWS, :]
            center = acc
            for d in range(1, radius + 1):
                acc = acc + pad_ref[POOL_HALO + base - d:POOL_HALO + base - d + POOL_ROWS, :]
                acc = acc + pad_ref[POOL_HALO + base + d:POOL_HALO + base + d + POOL_ROWS, :]
            t = lax.broadcasted_iota(i32, (POOL_ROWS, 1), 0) + base
            lo = jnp.maximum(t - radius, 0)
            hi = jnp.minimum(t + radius + 1, seq_len)
            count = (hi - lo).astype(f32)
            pooled = (acc / count - center).astype(bf16)
            o_ref[base:base + POOL_ROWS, cols] = (_dot(pooled, w) * sc).astype(o_ref.dtype)


def _pool(p, pool_w, pool_scale, *, n_seq, seq_len, seq0):
    ucol = COL_U // POOL_WIDTH
    return pl.pallas_call(
        functools.partial(_pool_kernel, seq_len=seq_len),
        grid=(n_seq,),
        in_specs=[
            pl.BlockSpec((seq_len, POOL_WIDTH), lambda s: (seq0 + s, ucol)),
            pl.BlockSpec((POOL_GROUPS, POOL_GROUP_DIM, POOL_GROUP_DIM), lambda s: (0, 0, 0)),
            pl.BlockSpec((1, POOL_WIDTH), lambda s: (0, 0)),
        ],
        out_specs=pl.BlockSpec((seq_len, POOL_WIDTH), lambda s: (s, 0)),
        out_shape=jax.ShapeDtypeStruct((n_seq * seq_len, POOL_WIDTH), bf16),
        scratch_shapes=[pltpu.VMEM((seq_len + 2 * POOL_HALO, POOL_GROUP_DIM), f32)],
        compiler_params=_cparams(("parallel",)),
        name="pool",
    )(p, pool_w, pool_scale)


def _ret_tables(lg_ref, intra_ref, qdec_ref, kdec_ref, *, reverse):
    c = RET_CHUNK
    i = lax.broadcasted_iota(i32, (c, c), 0).astype(f32)
    j = lax.broadcasted_iota(i32, (c, c), 1).astype(f32)
    pos = lax.broadcasted_iota(i32, (c, V7X_LANES), 0).astype(f32)
    for h in range(RET_HEADS):
        lg = lg_ref[h]
        if reverse:
            diff = j - i
            qd = c - pos
            kd = pos
        else:
            diff = i - j
            qd = pos + 1.0
            kd = c - 1.0 - pos
        intra_ref[h] = jnp.where(diff >= 0, jnp.exp(lg * jnp.maximum(diff, 0.0)), 0.0)
        qdec_ref[h] = jnp.exp(lg * qd)
        kdec_ref[h] = jnp.exp(lg * kd)


def _ret_chunk(q_ref, k_ref, v_ref, lg_ref, intra_ref, qdec_ref, kdec_ref, state_ref):
    outs = []
    k_scale = RET_QK_DIM ** -0.5
    for h in range(RET_HEADS):
        q = q_ref[:, h * RET_QK_DIM:(h + 1) * RET_QK_DIM].astype(bf16)
        k = k_ref[:, h * RET_QK_DIM:(h + 1) * RET_QK_DIM].astype(f32) * k_scale
        v = v_ref[:, h * RET_V_DIM:(h + 1) * RET_V_DIM].astype(bf16)
        scores = _dot_nt(q, k.astype(bf16)) * intra_ref[h]
        state = state_ref[h]
        out = _dot(scores.astype(bf16), v) + _dot(q, state.astype(bf16)) * qdec_ref[h]
        kd = (k * kdec_ref[h][:, :RET_QK_DIM]).astype(bf16)
        chunk_decay = jnp.exp(jnp.full((1, RET_V_DIM), lg_ref[h] * RET_CHUNK, f32))
        state_ref[h] = state * chunk_decay + _dot_tn(kd, v)
        outs.append(out)
    return outs


def _ret_bwd_kernel(lg_ref, ql_ref, kl_ref, vl_ref, qc_ref, kc_ref, vc_ref, ol_ref, oc_ref,
                    intra_ref, qdec_ref, kdec_ref, state_ref):
    t = pl.program_id(1)

    @pl.when((pl.program_id(0) == 0) & (t == 0))
    def _():
        _ret_tables(lg_ref, intra_ref, qdec_ref, kdec_ref, reverse=True)

    @pl.when(t == 0)
    def _():
        state_ref[...] = jnp.zeros_like(state_ref)
        outs = _ret_chunk(qc_ref, kc_ref, vc_ref, lg_ref, intra_ref, qdec_ref, kdec_ref, state_ref)
        for h in range(RET_HEADS):
            oc_ref[:, h * RET_V_DIM:(h + 1) * RET_V_DIM] = outs[h]

    @pl.when(t > 0)
    def _():
        outs = _ret_chunk(ql_ref, kl_ref, vl_ref, lg_ref, intra_ref, qdec_ref, kdec_ref, state_ref)
        for h in range(RET_HEADS):
            ol_ref[:, h * RET_V_DIM:(h + 1) * RET_V_DIM] = outs[h]


def _ret_finish(outs, yb_ref, g_ref, o_ref):
    for h in range(RET_HEADS):
        cols = slice(h * RET_V_DIM, (h + 1) * RET_V_DIM)
        y = outs[h] + yb_ref[:, cols]
        mu = jnp.mean(y, axis=-1, keepdims=True)
        yc = y - mu
        var = jnp.mean(yc * yc, axis=-1, keepdims=True)
        yn = yc * lax.rsqrt(var + NORM_EPS)
        g = g_ref[:, cols].astype(f32)
        o_ref[:, cols] = (g * jax.nn.sigmoid(g) * yn).astype(o_ref.dtype)


def _ret_fwd_kernel(lg_ref, ql_ref, kl_ref, vl_ref, gl_ref, ybl_ref, qc_ref, kc_ref, vc_ref, gc_ref, ybc_ref,
                    ol_ref, oc_ref, intra_ref, qdec_ref, kdec_ref, state_ref):
    t = pl.program_id(1)

    @pl.when((pl.program_id(0) == 0) & (t == 0))
    def _():
        _ret_tables(lg_ref, intra_ref, qdec_ref, kdec_ref, reverse=False)

    @pl.when(t == 0)
    def _():
        state_ref[...] = jnp.zeros_like(state_ref)
        outs = _ret_chunk(qc_ref, kc_ref, vc_ref, lg_ref, intra_ref, qdec_ref, kdec_ref, state_ref)
        _ret_finish(outs, ybc_ref, gc_ref, oc_ref)

    @pl.when(t > 0)
    def _():
        outs = _ret_chunk(ql_ref, kl_ref, vl_ref, lg_ref, intra_ref, qdec_ref, kdec_ref, state_ref)
        _ret_finish(outs, ybl_ref, gl_ref, ol_ref)


def _ret_scratch():
    c = RET_CHUNK
    return [
        pltpu.VMEM((RET_HEADS, c, c), f32),
        pltpu.VMEM((RET_HEADS, c, V7X_LANES), f32),
        pltpu.VMEM((RET_HEADS, c, V7X_LANES), f32),
        pltpu.VMEM((RET_HEADS, RET_QK_DIM, RET_V_DIM), f32),
    ]


def _retention(p, lg_fwd, lg_bwd):
    c = RET_CHUNK
    nc = SEQ // c
    qcol, kcol = COL_QR // RET_QK_WIDTH, COL_KR // RET_QK_WIDTH
    vcol, gcol = COL_VR // RET_V_WIDTH, COL_GR // RET_V_WIDTH

    def lat_row(reverse):
        def f(b, t):
            s = jnp.maximum(t, 1)
            return b * nc + ((nc - s) if reverse else (s - 1))
        return f

    def specs(reverse, with_gate):
        row = lat_row(reverse)
        lat = [pl.BlockSpec((c, RET_QK_WIDTH), lambda b, t: (row(b, t), qcol)),
               pl.BlockSpec((c, RET_QK_WIDTH), lambda b, t: (row(b, t), kcol)),
               pl.BlockSpec((c, RET_V_WIDTH), lambda b, t: (row(b, t), vcol))]
        ctx = [pl.BlockSpec((c, RET_QK_WIDTH), lambda b, t: (CTX_BLK0 + b, qcol)),
               pl.BlockSpec((c, RET_QK_WIDTH), lambda b, t: (CTX_BLK0 + b, kcol)),
               pl.BlockSpec((c, RET_V_WIDTH), lambda b, t: (CTX_BLK0 + b, vcol))]
        if with_gate:
            lat += [pl.BlockSpec((c, RET_V_WIDTH), lambda b, t: (row(b, t), gcol)),
                    pl.BlockSpec((c, RET_V_WIDTH), lambda b, t: (row(b, t), 0))]
            ctx += [pl.BlockSpec((c, RET_V_WIDTH), lambda b, t: (CTX_BLK0 + b, gcol)),
                    pl.BlockSpec((c, RET_V_WIDTH), lambda b, t: (b, 0))]
        outs = [pl.BlockSpec((c, RET_V_WIDTH), lambda b, t: (row(b, t), 0)),
                pl.BlockSpec((c, RET_V_WIDTH), lambda b, t: (b, 0))]
        return [pl.BlockSpec(memory_space=pltpu.SMEM)] + lat + ctx, outs

    in_specs, out_specs = specs(True, False)
    yb_lat, yb_ctx = pl.pallas_call(
        _ret_bwd_kernel,
        grid=(BATCH, nc + 1),
        in_specs=in_specs,
        out_specs=out_specs,
        out_shape=[jax.ShapeDtypeStruct((N_LAT, RET_V_WIDTH), f32),
                   jax.ShapeDtypeStruct((N_CTX, RET_V_WIDTH), f32)],
        scratch_shapes=_ret_scratch(),
        compiler_params=_cparams(("arbitrary", "arbitrary")),
        name="retention_bwd",
    )(lg_bwd, p, p, p, p, p, p)

    in_specs, out_specs = specs(False, True)
    return pl.pallas_call(
        _ret_fwd_kernel,
        grid=(BATCH, nc + 1),
        in_specs=in_specs,
        out_specs=out_specs,
        out_shape=[jax.ShapeDtypeStruct((N_LAT, RET_V_WIDTH), bf16),
                   jax.ShapeDtypeStruct((N_CTX, RET_V_WIDTH), bf16)],
        scratch_shapes=_ret_scratch(),
        compiler_params=_cparams(("arbitrary", "arbitrary")),
        name="retention_fwd",
    )(lg_fwd, p, p, p, p, yb_lat, p, p, p, p, yb_ctx)


def _merge_out_kernel(ya_ref, yp_ref, yr_ref, ga_ref, gp_ref, gr_ref, wa_ref, wp_ref, wr_ref, wo_ref,
                      x_ref, mod_ref, o_ref, m_ref, *, n_col):
    j = pl.program_id(1)

    @pl.when(j < n_col)
    def _():
        acc = _sigmoid(ga_ref[...].astype(f32)) * _dot(ya_ref[...], wa_ref[...].astype(bf16))
        acc = acc + _sigmoid(gp_ref[...].astype(f32)) * _dot(yp_ref[...], wp_ref[...].astype(bf16))
        acc = acc + _sigmoid(gr_ref[...].astype(f32)) * _dot(yr_ref[...], wr_ref[...].astype(bf16))
        m_ref[j] = acc.astype(m_ref.dtype)

    @pl.when(j >= n_col)
    def _():
        y = None
        for c in range(n_col):
            part = _dot(m_ref[c], wo_ref[c * TN_MERGE:(c + 1) * TN_MERGE, :].astype(bf16))
            y = part if y is None else y + part
        o_ref[...] = x_ref[...] + mod_ref[0, 2:3, :] * y


def _merge_out(ya, yp, yr, p, w_a, w_p, w_r, w_out, x, mod_l, *, layer, tm, row_block0, n_tiles, mod_sel):
    n_col = D_MODEL // TN_MERGE
    gate0 = COL_GATES // TN_MERGE

    def first(j):
        return jnp.minimum(j, n_col - 1)

    def second(j):
        return jnp.maximum(j - n_col, 0)

    def gate_spec(br):
        return pl.BlockSpec((tm, TN_MERGE), lambda i, j: (row_block0 + i, gate0 + br * n_col + first(j)))

    return pl.pallas_call(
        functools.partial(_merge_out_kernel, n_col=n_col),
        grid=(n_tiles, 2 * n_col),
        in_specs=[
            pl.BlockSpec((tm, ATTN_Q_WIDTH), lambda i, j: (i, 0)),
            pl.BlockSpec((tm, POOL_WIDTH), lambda i, j: (i, 0)),
            pl.BlockSpec((tm, RET_V_WIDTH), lambda i, j: (i, 0)),
            gate_spec(0), gate_spec(1), gate_spec(2),
            pl.BlockSpec((None, ATTN_Q_WIDTH, TN_MERGE), lambda i, j: (layer, 0, first(j))),
            pl.BlockSpec((None, POOL_WIDTH, TN_MERGE), lambda i, j: (layer, 0, first(j))),
            pl.BlockSpec((None, RET_V_WIDTH, TN_MERGE), lambda i, j: (layer, 0, first(j))),
            pl.BlockSpec((None, D_MODEL, TN_MERGE), lambda i, j: (layer, 0, second(j))),
            pl.BlockSpec((tm, TN_MERGE), lambda i, j: (row_block0 + i, second(j))),
            pl.BlockSpec((1, N_MOD, TN_MERGE), lambda i, j: (mod_sel(i), 0, second(j))),
        ],
        out_specs=pl.BlockSpec((tm, TN_MERGE), lambda i, j: (row_block0 + i, second(j))),
        out_shape=jax.ShapeDtypeStruct(x.shape, f32),
        scratch_shapes=[pltpu.VMEM((n_col, tm, TN_MERGE), bf16)],
        input_output_aliases={10: 0},
        compiler_params=_cparams(("parallel", "arbitrary")),
        name="merge_out",
    )(ya, yp, yr, p, p, p, w_a, w_p, w_r, w_out, x, mod_l)


ROUTE_LANES = V7X_LANES
LANE_E1, LANE_E2, LANE_R1, LANE_R2, LANE_W1, LANE_W2 = 0, 1, 2, 3, 4, 5
PLAN_ROWS = 8
HI16 = 0xFFFF0000


def _pack_bf16_pairs(v):
    n = v.shape[1] // 2
    bits = lax.bitcast_convert_type(v.astype(bf16).astype(f32), u32)
    return (bits[:, :n] >> 16) | (bits[:, n:] & jnp.uint32(HI16))


def _unpack_bf16_pairs(words):
    return (lax.bitcast_convert_type(words << 16, f32),
            lax.bitcast_convert_type(words & jnp.uint32(HI16), f32))


def _route_kernel(x_ref, mod_ref, g_ref, w_ref, b_ref, h_ref, info_ref, plan_ref, cnt_ref,
                  hf_ref, tri_ref, run_ref):
    tm = TM_ROUTE
    step = pl.program_id(0)

    @pl.when(step == 0)
    def _():
        run_ref[...] = jnp.zeros_like(run_ref)
        r = lax.broadcasted_iota(i32, (tm, tm), 0)
        c = lax.broadcasted_iota(i32, (tm, tm), 1)
        tri_ref[...] = jnp.where(c < r, 1.0, 0.0).astype(bf16)

    g = g_ref[...]
    shift = mod_ref[0, 3:4, :]
    scale = mod_ref[0, 4:5, :]

    def body(r, carry):
        rows = pl.ds(pl.multiple_of(r * ROW_CHUNK, ROW_CHUNK), ROW_CHUNK)
        hf_ref[rows, :] = _norm_mod_rows(x_ref[rows, :], g, shift, scale).astype(bf16)
        return carry

    lax.fori_loop(0, tm // ROW_CHUNK, body, 0)

    for c in range(tm // ROW_CHUNK):
        words = _pack_bf16_pairs(hf_ref[c * ROW_CHUNK:(c + 1) * ROW_CHUNK, :])
        for s in range(XS_SUB):
            h_ref[pl.ds(c * ROW_CHUNK * XS_SUB + s, ROW_CHUNK, stride=XS_SUB), :] = (
                words[:, s * V7X_LANES:(s + 1) * V7X_LANES])

    logits = _dot(hf_ref[...], w_ref[...].astype(bf16)) + b_ref[...]
    lane = lax.broadcasted_iota(i32, (tm, ROUTE_LANES), 1)
    neg = jnp.float32(-jnp.inf)
    lane_f = lane.astype(f32)

    def first_argmax(vals):
        top = jnp.max(vals, axis=-1, keepdims=True)
        idx = jnp.min(jnp.where(vals == top, lane_f, float(ROUTE_LANES)), axis=-1, keepdims=True)
        return top, idx.astype(i32)

    g_logits = jnp.where(lane < MOE_GROUPS, logits, neg)
    g_top, grp = first_argmax(g_logits)
    p_grp = 1.0 / jnp.sum(jnp.exp(g_logits - g_top), axis=-1, keepdims=True)

    e_lane = lane - MOE_GROUPS
    in_grp = (e_lane >= 0) & (e_lane < MOE_EXPERTS) & (lax.shift_right_arithmetic(e_lane, 3) == grp)
    e_logits = jnp.where(in_grp, logits, neg)
    top1, idx1 = first_argmax(e_logits)
    e_logits2 = jnp.where(lane == idx1, neg, e_logits)
    top2, idx2 = first_argmax(e_logits2)
    a = jnp.exp(top2 - top1)
    w1 = (1.0 / (1.0 + a)) * p_grp
    w2 = (a / (1.0 + a)) * p_grp
    e1 = idx1 - MOE_GROUPS
    e2 = idx2 - MOE_GROUPS

    hot1 = lane == e1
    hot2 = lane == e2
    o1 = jnp.where(hot1, 1.0, 0.0)
    o2 = jnp.where(hot2, 1.0, 0.0)
    tri = tri_ref[...]
    before1 = _dot(tri, o1.astype(bf16))
    before2 = _dot(tri, o2.astype(bf16))
    c1 = jnp.sum(o1, axis=0, keepdims=True)
    c2 = jnp.sum(o2, axis=0, keepdims=True)
    run = run_ref[...]
    rank1 = jnp.sum(jnp.where(hot1, before1 + run, 0.0), axis=-1, keepdims=True)
    rank2 = jnp.sum(jnp.where(hot2, before2 + run + c1, 0.0), axis=-1, keepdims=True)
    run = run + c1 + c2
    run_ref[...] = run
    cnt_ref[...] = jnp.broadcast_to(run, cnt_ref.shape)

    info = jnp.where(lane == LANE_E1, e1.astype(f32), 0.0)
    info = jnp.where(lane == LANE_E2, e2.astype(f32), info)
    info = jnp.where(lane == LANE_W1, w1, info)
    info = jnp.where(lane == LANE_W2, w2, info)
    info = jnp.where(lane == LANE_R1, rank1, info)
    info = jnp.where(lane == LANE_R2, rank2, info)
    info_ref[...] = info
    plan_ref[0] = jnp.transpose(info)[0:PLAN_ROWS, :].astype(i32)


def _route(x, mod_l, g2, w_route, b_route, *, n_tiles, mod_sel):
    tm = TM_ROUTE
    return pl.pallas_call(
        _route_kernel,
        grid=(n_tiles,),
        in_specs=[
            pl.BlockSpec((tm, D_MODEL), lambda i: (i, 0)),
            pl.BlockSpec((1, N_MOD, D_MODEL), lambda i: (mod_sel(i), 0, 0)),
            pl.BlockSpec((1, D_MODEL), lambda i: (0, 0)),
            pl.BlockSpec((D_MODEL, ROUTE_LANES), lambda i: (0, 0)),
            pl.BlockSpec((1, ROUTE_LANES), lambda i: (0, 0)),
        ],
        out_specs=[
            pl.BlockSpec((tm * XS_SUB, V7X_LANES), lambda i: (i, 0)),
            pl.BlockSpec((tm, ROUTE_LANES), lambda i: (i, 0)),
            pl.BlockSpec((1, PLAN_ROWS, tm), lambda i: (i, 0, 0)),
            pl.BlockSpec((8, ROUTE_LANES), lambda i: (0, 0)),
        ],
        out_shape=[
            jax.ShapeDtypeStruct((n_tiles * tm * XS_SUB, V7X_LANES), u32),
            jax.ShapeDtypeStruct((n_tiles * tm, ROUTE_LANES), f32),
            jax.ShapeDtypeStruct((n_tiles, PLAN_ROWS, tm), i32),
            jax.ShapeDtypeStruct((8, ROUTE_LANES), f32),
        ],
        scratch_shapes=[
            pltpu.VMEM((tm, D_MODEL), bf16),
            pltpu.VMEM((tm, tm), bf16),
            pltpu.VMEM((1, ROUTE_LANES), f32),
        ],
        compiler_params=_cparams(("arbitrary",)),
        name="moe_route",
    )(x, mod_l, g2, w_route, b_route)


def _dest_kernel(pend_ref, plan_ref, o_ref):
    plan = plan_ref[...]
    start = jnp.zeros_like(plan)
    for e in range(MOE_EXPERTS):
        start = jnp.where(plan == e, pend_ref[e], start)
    o_ref[...] = start + pltpu.roll(plan, plan.shape[0] - MOE_TOP_K, 0)


def _dest(pend0, plan):
    rows = plan.shape[0] * PLAN_ROWS
    return pl.pallas_call(
        _dest_kernel,
        grid_spec=pltpu.PrefetchScalarGridSpec(
            num_scalar_prefetch=1,
            grid=(1,),
            in_specs=[pl.BlockSpec((rows, TM_ROUTE), lambda i, pend: (0, 0))],
            out_specs=pl.BlockSpec((rows, TM_ROUTE), lambda i, pend: (0, 0)),
        ),
        out_shape=jax.ShapeDtypeStruct((rows, TM_ROUTE), i32),
        compiler_params=_cparams(("arbitrary",)),
        name="moe_dest",
    )(pend0, plan.reshape(rows, TM_ROUTE))


def _dispatch_kernel(pend_ref, dest_ref, h_ref, xs_ref, zero_ref, sem):
    tm = TM_ROUTE
    step = pl.program_id(0)

    blk_rows = MOE_BLK * XS_SUB

    def zero_copy(e):
        start = pl.multiple_of((pend_ref[e + 1] - MOE_BLK) * XS_SUB, blk_rows)
        return pltpu.make_async_copy(zero_ref, xs_ref.at[pl.ds(start, blk_rows)], sem)

    def tail_copy(blk):
        start = pl.multiple_of(blk * blk_rows, blk_rows)
        return pltpu.make_async_copy(zero_ref, xs_ref.at[pl.ds(start, blk_rows)], sem)

    @pl.when(step == 0)
    def _():
        zero_ref[...] = jnp.zeros_like(zero_ref)
        first_unused = pend_ref[MOE_EXPERTS] // MOE_BLK
        for phase in range(2):
            def body(e, carry):
                @pl.when(pend_ref[e + 1] > pend_ref[e])
                def _():
                    if phase == 0:
                        zero_copy(e).start()
                    else:
                        zero_copy(e).wait()
                return carry
            lax.fori_loop(0, MOE_EXPERTS, body, 0)

            def tail(blk, carry):
                if phase == 0:
                    tail_copy(blk).start()
                else:
                    tail_copy(blk).wait()
                return carry
            lax.fori_loop(first_unused, MOE_ROWS // MOE_BLK, tail, 0)

    def row_copy(r, k):
        d = dest_ref[k, r]
        src = h_ref.at[pl.ds(pl.multiple_of(r * XS_SUB, XS_SUB), XS_SUB)]
        return pltpu.make_async_copy(src, xs_ref.at[pl.ds(pl.multiple_of(d * XS_SUB, XS_SUB), XS_SUB)], sem)

    def start(r, carry):
        row_copy(r, 0).start(priority=0)
        row_copy(r, 1).start(priority=1)
        return carry

    lax.fori_loop(0, tm, start, 0, unroll=4)
    for k in range(MOE_TOP_K):
        pltpu.make_async_copy(h_ref, xs_ref.at[pl.ds(0, tm * XS_SUB)], sem).wait()


def _dispatch(pend0, dest, h_packed, *, n_tiles):
    tm = TM_ROUTE
    return pl.pallas_call(
        _dispatch_kernel,
        grid_spec=pltpu.PrefetchScalarGridSpec(
            num_scalar_prefetch=1,
            grid=(n_tiles,),
            in_specs=[
                pl.BlockSpec((PLAN_ROWS, tm), lambda i, pend: (i, 0), memory_space=pltpu.SMEM),
                pl.BlockSpec((tm * XS_SUB, V7X_LANES), lambda i, pend: (i, 0)),
            ],
            out_specs=pl.BlockSpec(memory_space=pl.ANY),
            scratch_shapes=[pltpu.VMEM((MOE_BLK * XS_SUB, V7X_LANES), u32), pltpu.SemaphoreType.DMA(())],
        ),
        out_shape=jax.ShapeDtypeStruct((MOE_ROWS * XS_SUB, V7X_LANES), u32),
        compiler_params=_cparams(("arbitrary",)),
        name="moe_dispatch",
    )(pend0, dest, h_packed)


WEIGHT_SLOTS = 3


def _expert_kernel(be_ref, nv_ref, first_ref, slot_ref, next_ref, next2_ref, xs_ref, wg_hbm, wu_hbm, wd_hbm,
                   ys_ref, wg_buf, wu_buf, wd_buf, wg16, wu16, wd16, sems, *, layer):
    i = pl.program_id(0)

    def weight_copies(e, s):
        return (pltpu.make_async_copy(wg_hbm.at[layer, e], wg_buf.at[s], sems.at[s, 0]),
                pltpu.make_async_copy(wu_hbm.at[layer, e], wu_buf.at[s], sems.at[s, 1]),
                pltpu.make_async_copy(wd_hbm.at[layer, e], wd_buf.at[s], sems.at[s, 2]))

    def swiglu_block(wg, wu, wd):
        parts = [_unpack_bf16_pairs(xs_ref[pl.ds(t, MOE_BLK, stride=XS_SUB), :]) for t in range(XS_SUB)]
        x = jnp.concatenate([p[0] for p in parts] + [p[1] for p in parts], axis=1).astype(bf16)
        gate = _dot(x, wg)
        up = _dot(x, wu)
        h = gate * _sigmoid(gate) * up
        words = _pack_bf16_pairs(_dot(h.astype(bf16), wd))
        for t in range(XS_SUB):
            ys_ref[pl.ds(t, MOE_BLK, stride=XS_SUB), :] = words[:, t * V7X_LANES:(t + 1) * V7X_LANES]

    @pl.when(i < nv_ref[0])
    def _():
        s = slot_ref[i]

        @pl.when(first_ref[i] == 1)
        def _():
            @pl.when(i == 0)
            def _():
                for cp in weight_copies(be_ref[0], 0):
                    cp.start()

                @pl.when(next_ref[0] >= 0)
                def _():
                    for cp in weight_copies(next_ref[0], 1):
                        cp.start()
            for cp in weight_copies(be_ref[i], s):
                cp.wait()

            @pl.when(next2_ref[i] >= 0)
            def _():
                for cp in weight_copies(next2_ref[i], (s + 2) % WEIGHT_SLOTS):
                    cp.start()

            wg, wu, wd = wg_buf[s].astype(bf16), wu_buf[s].astype(bf16), wd_buf[s].astype(bf16)
            wg16[...] = wg
            wu16[...] = wu
            wd16[...] = wd
            swiglu_block(wg, wu, wd)

        @pl.when(first_ref[i] == 0)
        def _():
            swiglu_block(wg16[...], wu16[...], wd16[...])

    @pl.when(i >= nv_ref[0])
    def _():
        ys_ref[...] = jnp.zeros_like(ys_ref)


def _experts(block_expert, n_valid, first, slot, nxt, nxt2, xs, w_gate, w_up, w_down, *, layer):
    n_blocks = MOE_ROWS // MOE_BLK

    def row(i, be, nv, *_):
        return jnp.minimum(i, nv[0] - 1)

    return pl.pallas_call(
        functools.partial(_expert_kernel, layer=layer),
        grid_spec=pltpu.PrefetchScalarGridSpec(
            num_scalar_prefetch=6,
            grid=(n_blocks,),
            in_specs=[
                pl.BlockSpec((MOE_BLK * XS_SUB, V7X_LANES), lambda i, *pf: (row(i, *pf), 0)),
                pl.BlockSpec(memory_space=pl.ANY),
                pl.BlockSpec(memory_space=pl.ANY),
                pl.BlockSpec(memory_space=pl.ANY),
            ],
            out_specs=pl.BlockSpec((MOE_BLK * XS_SUB, V7X_LANES), lambda i, *pf: (i, 0)),
            scratch_shapes=[
                pltpu.VMEM((WEIGHT_SLOTS, D_MODEL, MOE_D_FF), f32),
                pltpu.VMEM((WEIGHT_SLOTS, D_MODEL, MOE_D_FF), f32),
                pltpu.VMEM((WEIGHT_SLOTS, MOE_D_FF, D_MODEL), f32),
                pltpu.VMEM((D_MODEL, MOE_D_FF), bf16),
                pltpu.VMEM((D_MODEL, MOE_D_FF), bf16),
                pltpu.VMEM((MOE_D_FF, D_MODEL), bf16),
                pltpu.SemaphoreType.DMA((WEIGHT_SLOTS, 3)),
            ],
        ),
        out_shape=jax.ShapeDtypeStruct((MOE_ROWS * XS_SUB, V7X_LANES), u32),
        compiler_params=_cparams(("arbitrary",)),
        name="moe_experts",
    )(block_expert, n_valid, first, slot, nxt, nxt2, xs, w_gate, w_up, w_down)


def _combine_kernel(dest0_ref, destn_ref, x_ref, info_ref, mod_ref, modn_ref, gn_ref, ys_ref,
                    o_ref, hn_ref, buf_ref, sems, *, n_tiles, modulated):
    tm = TM_COMBINE
    i = pl.program_id(0)

    def start_tile(dest_ref, slot):
        def row_copy(r, k):
            d = dest_ref[k, r]
            src = ys_ref.at[pl.ds(pl.multiple_of(d * XS_SUB, XS_SUB), XS_SUB)]
            dst = buf_ref.at[slot, k, pl.ds(pl.multiple_of(r * XS_SUB, XS_SUB), XS_SUB)]
            return pltpu.make_async_copy(src, dst, sems.at[slot])

        def body(r, carry):
            row_copy(r, 0).start(priority=0)
            row_copy(r, 1).start(priority=1)
            return carry

        lax.fori_loop(0, tm, body, 0, unroll=4)

    @pl.when(i == 0)
    def _():
        start_tile(dest0_ref, 0)

    slot = i % 2

    @pl.when(i + 1 < n_tiles)
    def _():
        start_tile(destn_ref, 1 - slot)

    for k in range(MOE_TOP_K):
        pltpu.make_async_copy(ys_ref.at[pl.ds(0, tm * XS_SUB)], buf_ref.at[slot, k], sems.at[slot]).wait()

    info = info_ref[...]
    w1 = info[:, LANE_W1:LANE_W1 + 1]
    w2 = info[:, LANE_W2:LANE_W2 + 1]
    for t in range(XS_SUB):
        rows = pl.ds(t, tm, stride=XS_SUB)
        lo1, hi1 = _unpack_bf16_pairs(buf_ref[slot, 0, rows, :])
        lo2, hi2 = _unpack_bf16_pairs(buf_ref[slot, 1, rows, :])
        for half, (y1, y2) in enumerate(((lo1, lo2), (hi1, hi2))):
            cols = slice(half * PACKED + t * V7X_LANES, half * PACKED + (t + 1) * V7X_LANES)
            o_ref[:, cols] = x_ref[:, cols] + mod_ref[0, 5:6, cols] * (y1 * w1 + y2 * w2)

    g = gn_ref[...]
    for c in range(tm // ROW_CHUNK):
        rows = slice(c * ROW_CHUNK, (c + 1) * ROW_CHUNK)
        if modulated:
            y = _norm_mod_rows(o_ref[rows, :], g, modn_ref[0, 0:1, :], modn_ref[0, 1:2, :])
        else:
            xr = o_ref[rows, :]
            y = xr * lax.rsqrt(jnp.mean(xr * xr, axis=-1, keepdims=True) + NORM_EPS) * g
        hn_ref[rows, :] = y.astype(hn_ref.dtype)


def _combine(dest, x, info, mod_l, mod_next, g_next, ys, *, n_tiles, mod_sel):
    tm = TM_COMBINE
    per = TM_ROUTE // tm
    modulated = mod_next is not None
    if not modulated:
        mod_next = mod_l

    def dest_spec(tile):
        return pl.BlockSpec((PLAN_ROWS, tm), lambda i: (tile(i) // per, tile(i) % per), memory_space=pltpu.SMEM)

    return pl.pallas_call(
        functools.partial(_combine_kernel, n_tiles=n_tiles, modulated=modulated),
        grid=(n_tiles,),
        in_specs=[
            dest_spec(lambda i: 0),
            dest_spec(lambda i: jnp.minimum(i + 1, n_tiles - 1)),
            pl.BlockSpec((tm, D_MODEL), lambda i: (i, 0)),
            pl.BlockSpec((tm, ROUTE_LANES), lambda i: (i, 0)),
            pl.BlockSpec((1, N_MOD, D_MODEL), lambda i: (mod_sel(i), 0, 0)),
            pl.BlockSpec((1, N_MOD, D_MODEL), lambda i: (mod_sel(i), 0, 0)),
            pl.BlockSpec((1, D_MODEL), lambda i: (0, 0)),
            pl.BlockSpec(memory_space=pl.ANY),
        ],
        out_specs=[pl.BlockSpec((tm, D_MODEL), lambda i: (i, 0)),
                   pl.BlockSpec((tm, D_MODEL), lambda i: (i, 0))],
        scratch_shapes=[pltpu.VMEM((2, MOE_TOP_K, tm * XS_SUB, V7X_LANES), u32),
                        pltpu.SemaphoreType.DMA((2,))],
        out_shape=[jax.ShapeDtypeStruct(x.shape, f32),
                   jax.ShapeDtypeStruct((n_tiles * tm, D_MODEL), bf16 if modulated else f32)],
        input_output_aliases={2: 0},
        compiler_params=_cparams(("arbitrary",)),
        name="moe_combine",
    )(dest, dest, x, info, mod_l, mod_next, g_next, ys)


def _moe(x, mod_l, g2, w_route, b_route, w_gate, w_up, w_down, mod_next, g_next, *, layer, with_ctx):
    n_tok = N_TOK if with_ctx else N_LAT
    rt = n_tok // TM_ROUTE
    lat_rt = N_LAT // TM_ROUTE
    h_packed, info, plan, cnt = _route(
        x, mod_l, g2, w_route, b_route, n_tiles=rt,
        mod_sel=lambda i: jnp.where(i < lat_rt, i // (SEQ // TM_ROUTE), BATCH))

    counts = cnt[0, :MOE_EXPERTS].astype(i32)
    padded = (counts + MOE_BLK - 1) // MOE_BLK * MOE_BLK
    pend = jnp.cumsum(padded)
    pend0 = jnp.concatenate([jnp.zeros((1,), i32), pend]).astype(i32)
    n_blocks = MOE_ROWS // MOE_BLK
    blk_id = jnp.arange(n_blocks, dtype=i32)

    def expert_at(row):
        return jnp.minimum(jnp.sum((pend[None, :] <= row[:, None]).astype(i32), axis=1), MOE_EXPERTS - 1)

    block_expert = expert_at(blk_id * MOE_BLK)
    n_valid = (pend[-1:] // MOE_BLK).astype(i32)
    valid = blk_id < n_valid[0]
    changed = jnp.concatenate([jnp.ones((1,), bool), block_expert[1:] != block_expert[:-1]])
    first = (changed & valid).astype(i32)
    slot = ((jnp.cumsum(first) - 1) % WEIGHT_SLOTS).astype(i32)

    def following(expert):
        end = jnp.sum(jnp.where(jnp.arange(MOE_EXPERTS, dtype=i32)[None, :] == expert[:, None],
                                pend[None, :], 0), axis=1)
        return jnp.where((expert >= 0) & (end < pend[-1]), expert_at(end), -1).astype(i32)

    nxt = following(block_expert)
    nxt2 = following(nxt)

    dest = _dest(pend0, plan)
    xs = _dispatch(pend0, dest, h_packed, n_tiles=rt)
    ys = _experts(block_expert, n_valid, first, slot, nxt, nxt2, xs, w_gate, w_up, w_down, layer=layer)
    ct = n_tok // TM_COMBINE
    lat_ct = N_LAT // TM_COMBINE
    return _combine(dest, x, info, mod_l, mod_next, g_next, ys, n_tiles=ct,
                    mod_sel=lambda i: jnp.where(i < lat_ct, i // (SEQ // TM_COMBINE), BATCH))


def _rope_tables():
    rows = SEQ // GRID_W
    row = jnp.repeat(jnp.arange(rows, dtype=f32), GRID_W)
    col = jnp.tile(jnp.arange(GRID_W, dtype=f32), rows)
    n_freq = HEAD_DIM // 4
    inv_freq = ROPE_BASE ** (-jnp.arange(n_freq, dtype=f32) / n_freq)
    ang = jnp.concatenate([row[:, None] * inv_freq, col[:, None] * inv_freq], axis=-1)
    cos, sin = jnp.cos(ang), jnp.sin(ang)
    return jnp.concatenate([cos, cos], axis=-1), jnp.concatenate([-sin, sin], axis=-1)


def kernel(x, c, ctx, c_ctx, w_mod, b_mod, norm1_g, norm2_g, w_in, attn_sink, pool_w, pool_scale, ret_decay_fwd, ret_decay_bwd, w_br_attn, w_br_pool, w_br_ret, w_out, w_route_group, b_route_group, w_route_expert, b_route_expert, w_expert_gate, w_expert_up, w_expert_down, final_norm_g):
    assert x.shape == (BATCH, SEQ, D_MODEL) and ctx.shape == (BATCH, CTX_LEN, D_MODEL)
    cos_full, sin_signed = _rope_tables()

    cond8 = jnp.concatenate([c, c_ctx[None, :], jnp.zeros((8 - BATCH - 1, D_MODEL), f32)], axis=0)
    mod = _modulation(cond8, w_mod, b_mod).reshape(DEPTH, 8, N_MOD, D_MODEL)

    lg_fwd = jax.nn.log_sigmoid(ret_decay_fwd.astype(f32))
    lg_bwd = jax.nn.log_sigmoid(ret_decay_bwd.astype(f32))
    pad = ROUTE_LANES - MOE_GROUPS - MOE_EXPERTS
    w_route = jnp.concatenate(
        [w_route_group, w_route_expert, jnp.zeros((DEPTH, D_MODEL, pad), f32)], axis=-1)
    b_route = jnp.concatenate([b_route_group, b_route_expert, jnp.zeros((DEPTH, pad), f32)], axis=-1)

    tok = jnp.concatenate([x.reshape(N_LAT, D_MODEL), ctx.reshape(N_CTX, D_MODEL)], axis=0)
    ctx_block0 = N_LAT // TM_CTX
    sel_ctx = lambda i: BATCH

    def sel_rows(tm):
        return lambda i: jnp.where(i < N_LAT // tm, i // (SEQ // tm), BATCH)

    h1 = _norm(tok, mod[0], norm1_g[0][None, :], n_tiles=N_TOK // TM_NORM, mod_sel=sel_rows(TM_NORM))
    for l in range(DEPTH):
        with_ctx = l < DEPTH - 1
        mod_l = mod[l]
        p = _proj(h1, w_in, layer=l)

        ya_lat = _attn_lat(p, attn_sink[l], cos_full, sin_signed)
        yp_lat = _pool(p, pool_w[l], pool_scale[l][None, :], n_seq=BATCH, seq_len=SEQ, seq0=0)
        yr_lat, yr_ctx = _retention(p, lg_fwd[l], lg_bwd[l])

        tok = _merge_out(ya_lat, yp_lat, yr_lat, p, w_br_attn, w_br_pool, w_br_ret, w_out, tok, mod_l,
                         layer=l, tm=TM_MERGE, row_block0=0, n_tiles=N_LAT // TM_MERGE,
                         mod_sel=sel_rows(TM_MERGE))
        if with_ctx:
            ya_ctx = _attn_ctx(p, attn_sink[l])
            yp_ctx = _pool(p, pool_w[l], pool_scale[l][None, :], n_seq=BATCH, seq_len=CTX_LEN, seq0=CTX_BLK0)
            tok = _merge_out(ya_ctx, yp_ctx, yr_ctx, p, w_br_attn, w_br_pool, w_br_ret, w_out, tok, mod_l,
                             layer=l, tm=TM_CTX, row_block0=ctx_block0, n_tiles=1, mod_sel=sel_ctx)

        mod_next = mod[l + 1] if with_ctx else None
        g_next = norm1_g[l + 1][None, :] if with_ctx else final_norm_g[None, :]
        tok, h1 = _moe(tok, mod_l, norm2_g[l][None, :], w_route[l], b_route[l][None, :],
                       w_expert_gate, w_expert_up, w_expert_down, mod_next, g_next, layer=l, with_ctx=with_ctx)

    return h1.reshape(BATCH, SEQ, D_MODEL)
```
